```python
import math
import jax, jax.numpy as jnp
from jax import lax
import numpy as np

D_MODEL = 1024
BATCH = 16
SEQ = 4096
DEPTH = 2

CHUNK = 64
HEAD_DIM = 64
N_HEADS = D_MODEL // HEAD_DIM
N_SB_HEADS = N_HEADS // 2
N_RET_HEADS = N_HEADS - N_SB_HEADS
D_SB = N_SB_HEADS * HEAD_DIM
D_RET = N_RET_HEADS * HEAD_DIM
D_IN_EVEN = 3 * D_SB + 4 * D_RET
SB_BLOCK = 128
N_C_HEADS = N_HEADS
LEFT_CHUNKS = 8
BAND = (LEFT_CHUNKS + 1) * CHUNK
MAX_REL = 4 * CHUNK
D_FF_DENSE = 256 * (-(-(8 * D_MODEL // 3) // 256))
D_FF_EXPERT = 7 * D_MODEL // 2
N_EXPERTS = 8
TOP_K = 2
ROPE_BASE = 10000.0
LN_EPS = 1e-5
N_EVEN = (DEPTH + 1) // 2
N_ODD = DEPTH // 2
DEEPNORM_ALPHA = (2 * DEPTH) ** 0.25
DEEPNORM_BETA = (8 * DEPTH) ** -0.25

kernel_name = "hybrid_stickbreak_retention_chunkattn_moe"


def layer_norm(x, g, b):
    xf = x.astype(jnp.float32)
    mu = jnp.mean(xf, axis=-1, keepdims=True)
    var = jnp.mean(jnp.square(xf - mu), axis=-1, keepdims=True)
    return ((xf - mu) * lax.rsqrt(var + LN_EPS) * g + b).astype(x.dtype)


def rotary(x, pos):
    half = HEAD_DIM // 2
    inv_freq = ROPE_BASE ** (-jnp.arange(half, dtype=jnp.float32) / half)
    ang = pos.astype(jnp.float32)[:, None] * inv_freq[None, :]
    cos = jnp.cos(ang)[None, :, None, :]
    sin = jnp.sin(ang)[None, :, None, :]
    x1, x2 = x[..., :half], x[..., half:]
    return jnp.concatenate([x1 * cos - x2 * sin, x1 * sin + x2 * cos], axis=-1)


def stick_breaking(q, k, v):
    S = q.shape[2]
    scale = HEAD_DIM ** -0.5
    outs = []
    for blk in range(S // SB_BLOCK):
        t0 = blk * SB_BLOCK
        t1 = t0 + SB_BLOCK
        z = jnp.einsum('bhqd,bhkd->bhqk', q[:, :, t0:t1], k[:, :, :t1]) * scale
        valid = jnp.arange(t1)[None, :] < (t0 + jnp.arange(SB_BLOCK))[:, None]
        log_1m = jnp.where(valid, jax.nn.log_sigmoid(-z), 0.0)
        suffix = lax.cumsum(log_1m, axis=3, reverse=True) - log_1m
        w = jnp.where(valid, jnp.exp(jax.nn.log_sigmoid(z) + suffix), 0.0)
        outs.append(jnp.einsum('bhqk,bhkd->bhqd', w, v[:, :, :t1]))
    return jnp.concatenate(outs, axis=2)


def retention(q, k, v):
    B, S, H, d = q.shape
    n_chunks = S // CHUNK
    gamma = 1.0 - 2.0 ** (-5.0 - jnp.arange(H, dtype=jnp.float32))
    log_g = jnp.log(gamma)
    i = jnp.arange(CHUNK, dtype=jnp.float32)
    diff = i[:, None] - i[None, :]
    decay_in = jnp.where(diff >= 0, jnp.exp(log_g[:, None, None] * jnp.maximum(diff, 0.0)), 0.0)
    decay_q = jnp.exp(log_g[:, None] * (i + 1.0))[..., None]
    decay_k = jnp.exp(log_g[:, None] * (CHUNK - 1.0 - i))[..., None]
    decay_c = jnp.exp(log_g * CHUNK)[:, None, None]

    def to_chunks(t):
        return t.reshape(B, n_chunks, CHUNK, H, d).transpose(1, 0, 3, 2, 4)

    def step(state, inp):
        qc, kc, vc = inp
        inner = jnp.einsum('bhid,bhjd->bhij', qc, kc) * decay_in
        out = (jnp.einsum('bhij,bhjd->bhid', inner, vc)
               + jnp.einsum('bhid,bhde->bhie', qc, state) * decay_q)
        state = state * decay_c + jnp.einsum('bhjd,bhje->bhde', kc * decay_k, vc)
        return state, out

    state0 = jnp.zeros((B, H, d, d), jnp.float32)
    _, o = lax.scan(step, state0, (to_chunks(q), to_chunks(k), to_chunks(v)))
    return o.transpose(1, 0, 3, 2, 4).reshape(B, S, H, d)


def chunk_attention(q, k, v, bias_table):
    B, S, H, d = q.shape
    n_chunks = S // CHUNK
    pad = LEFT_CHUNKS * CHUNK
    scale = HEAD_DIM ** -0.5
    k_pad = jnp.pad(k, ((0, 0), (pad, 0), (0, 0), (0, 0)))
    v_pad = jnp.pad(v, ((0, 0), (pad, 0), (0, 0), (0, 0)))
    rel = jnp.arange(CHUNK)[:, None] + pad - jnp.arange(BAND)[None, :]
    rel_idx = jnp.clip(rel, -MAX_REL, MAX_REL) + MAX_REL
    bias = bias_table[:, rel_idx].astype(jnp.float32)

    def one_chunk(c):
        start = c * CHUNK
        q_c = lax.dynamic_slice_in_dim(q, start, CHUNK, axis=1)
        k_b = lax.dynamic_slice_in_dim(k_pad, start, BAND, axis=1)
        v_b = lax.dynamic_slice_in_dim(v_pad, start, BAND, axis=1)
        s = jnp.einsum('bqhd,bkhd->bhqk', q_c, k_b).astype(jnp.float32) * scale + bias
        key_pos = start - pad + jnp.arange(BAND)
        s = jnp.where((key_pos >= 0)[None, None, None, :], s, -jnp.inf)
        p = jax.nn.softmax(s, axis=-1)
        return jnp.einsum('bhqk,bkhd->bqhd', p.astype(v.dtype), v_b)

    o = lax.map(one_chunk, jnp.arange(n_chunks))
    return o.transpose(1, 0, 2, 3, 4).reshape(B, S, H * d)


def even_mixer(x, w_in, ret_gn_g, w_out):
    B, S, _ = x.shape
    h = x @ w_in
    splits = [D_SB, 2 * D_SB, 3 * D_SB, 3 * D_SB + D_RET, 3 * D_SB + 2 * D_RET, 3 * D_SB + 3 * D_RET]
    q_sb, k_sb, v_sb, q_r, k_r, v_r, g_r = jnp.split(h, splits, axis=-1)

    def sb_heads(t):
        return t.reshape(B, S, N_SB_HEADS, HEAD_DIM).transpose(0, 2, 1, 3).astype(jnp.float32)

    o_sb = stick_breaking(sb_heads(q_sb), sb_heads(k_sb), sb_heads(v_sb))
    o_sb = o_sb.transpose(0, 2, 1, 3).reshape(B, S, D_SB)

    def ret_heads(t):
        return t.reshape(B, S, N_RET_HEADS, HEAD_DIM).astype(jnp.float32)

    pos = jnp.arange(S)
    qr = rotary(ret_heads(q_r), pos) * HEAD_DIM ** -0.5
    kr = rotary(ret_heads(k_r), pos)
    o_r = retention(qr, kr, ret_heads(v_r))
    mu = jnp.mean(o_r, axis=-1, keepdims=True)
    var = jnp.mean(jnp.square(o_r - mu), axis=-1, keepdims=True)
    o_r = ((o_r - mu) * lax.rsqrt(var + LN_EPS)).reshape(B, S, D_RET) * ret_gn_g
    o_r = jax.nn.silu(g_r.astype(jnp.float32)) * o_r

    o = jnp.concatenate([o_sb, o_r], axis=-1).astype(x.dtype)
    return o @ w_out


def odd_mixer(x, w_qkv, rel_bias, w_out):
    B, S, _ = x.shape
    q, k, v = jnp.split(x @ w_qkv, 3, axis=-1)
    shp = (B, S, N_C_HEADS, HEAD_DIM)
    o = chunk_attention(q.reshape(shp), k.reshape(shp), v.reshape(shp), rel_bias)
    return o.astype(x.dtype) @ w_out


def swiglu(x, w_gate, w_up, w_down):
    return (jax.nn.silu(x @ w_gate) * (x @ w_up)) @ w_down


def moe_swiglu(x, w_router, b_router, w_gate, w_up, w_down):
    B, S, D = x.shape
    xt = x.reshape(B * S, D)
    logits = xt.astype(jnp.float32) @ w_router.astype(jnp.float32) + b_router.astype(jnp.float32)
    top_val, top_idx = lax.top_k(logits, TOP_K)
    gates = jax.nn.softmax(top_val, axis=-1)
    combine = jnp.sum(jax.nn.one_hot(top_idx, N_EXPERTS, dtype=jnp.float32) * gates[..., None], axis=1)
    y = jnp.zeros((B * S, D), jnp.float32)
    for e in range(N_EXPERTS):
        h = jax.nn.silu(xt @ w_gate[e]) * (xt @ w_up[e])
        y = y + combine[:, e:e + 1] * (h @ w_down[e]).astype(jnp.float32)
    return y.astype(x.dtype).reshape(B, S, D)


def setup_inputs(seed: int = 0) -> dict:
    key = jax.random.key(seed)
    ks = jax.random.split(key, 24)

    def nrm(k, shape, scale):
        return jax.random.normal(k, shape, jnp.float32) * scale

    d = D_MODEL
    beta = DEEPNORM_BETA
    return {
        'x': nrm(ks[0], (BATCH, SEQ, d), 1.0),
        'even_ln1_g': 1.0 + nrm(ks[1], (N_EVEN, d), 0.02),
        'even_ln1_b': nrm(ks[2], (N_EVEN, d), 0.02),
        'even_w_in': nrm(ks[3], (N_EVEN, d, D_IN_EVEN), d ** -0.5),
        'even_ret_gn_g': 1.0 + nrm(ks[4], (N_EVEN, D_RET), 0.02),
        'even_w_out': nrm(ks[5], (N_EVEN, d, d), d ** -0.5 * beta),
        'even_ln2_g': 1.0 + nrm(ks[6], (N_EVEN, d), 0.02),
        'even_ln2_b': nrm(ks[7], (N_EVEN, d), 0.02),
        'even_w_gate': nrm(ks[8], (N_EVEN, d, D_FF_DENSE), d ** -0.5),
        'even_w_up': nrm(ks[9], (N_EVEN, d, D_FF_DENSE), d ** -0.5),
        'even_w_down': nrm(ks[10], (N_EVEN, D_FF_DENSE, d), D_FF_DENSE ** -0.5 * beta),
        'odd_ln1_g': 1.0 + nrm(ks[11], (N_ODD, d), 0.02),
        'odd_ln1_b': nrm(ks[12], (N_ODD, d), 0.02),
        'odd_w_qkv': nrm(ks[13], (N_ODD, d, 3 * d), d ** -0.5),
        'odd_rel_bias': nrm(ks[14], (N_ODD, N_C_HEADS, 2 * MAX_REL + 1), 0.5),
        'odd_w_out': nrm(ks[15], (N_ODD, d, d), d ** -0.5 * beta),
        'odd_ln2_g': 1.0 + nrm(ks[16], (N_ODD, d), 0.02),
        'odd_ln2_b': nrm(ks[17], (N_ODD, d), 0.02),
        'odd_w_router': nrm(ks[18], (N_ODD, d, N_EXPERTS), d ** -0.5),
        'odd_b_router': nrm(ks[19], (N_ODD, N_EXPERTS), 0.01),
        'odd_w_gate': nrm(ks[20], (N_ODD, N_EXPERTS, d, D_FF_EXPERT), d ** -0.5),
        'odd_w_up': nrm(ks[21], (N_ODD, N_EXPERTS, d, D_FF_EXPERT), d ** -0.5),
        'odd_w_down': nrm(ks[22], (N_ODD, N_EXPERTS, D_FF_EXPERT, d), D_FF_EXPERT ** -0.5 * beta),
    }


def reference(x, even_ln1_g, even_ln1_b, even_w_in, even_ret_gn_g, even_w_out,
              even_ln2_g, even_ln2_b, even_w_gate, even_w_up, even_w_down,
              odd_ln1_g, odd_ln1_b, odd_w_qkv, odd_rel_bias, odd_w_out,
              odd_ln2_g, odd_ln2_b, odd_w_router, odd_b_router,
              odd_w_gate, odd_w_up, odd_w_down):
    alpha = DEEPNORM_ALPHA
    for layer in range(DEPTH):
        i = layer // 2
        if layer % 2 == 0:
            mix = even_mixer(x, even_w_in[i], even_ret_gn_g[i], even_w_out[i])
            x = layer_norm(alpha * x + mix, even_ln1_g[i], even_ln1_b[i])
            ffn = swiglu(x, even_w_gate[i], even_w_up[i], even_w_down[i])
            x = layer_norm(alpha * x + ffn, even_ln2_g[i], even_ln2_b[i])
        else:
            mix = odd_mixer(x, odd_w_qkv[i], odd_rel_bias[i], odd_w_out[i])
            x = layer_norm(alpha * x + mix, odd_ln1_g[i], odd_ln1_b[i])
            ffn = moe_swiglu(x, odd_w_router[i], odd_b_router[i],
                             odd_w_gate[i], odd_w_up[i], odd_w_down[i])
            x = layer_norm(alpha * x + ffn, odd_ln2_g[i], odd_ln2_b[i])
    return x
```

```python
import functools

import jax
import jax.numpy as jnp
from jax import lax
from jax.experimental import pallas as pl
from jax.experimental.pallas import tpu as pltpu

HEAD_DIM = 64
LANES = 128
CHUNK = 64
LEFT_CHUNKS = 8
MAX_REL = 4 * CHUNK
TOP_K = 2
ROPE_BASE = 10000.0
LN_EPS = 1e-5
NEG_BIG = -1e30
VMEM_LIMIT = 56 * 1024 * 1024

BF16 = jnp.bfloat16
F32 = jnp.float32


def _params(*sem):
    return pltpu.CompilerParams(dimension_semantics=sem, vmem_limit_bytes=VMEM_LIMIT)


def _tile(n, target):
    if n <= target:
        return n
    t = target - target % LANES
    while t >= LANES:
        if n % t == 0:
            return t
        t -= LANES
    return n


def _dot(a, b):
    return jnp.dot(a, b, preferred_element_type=F32)


def _dot_nt(a, b):
    return lax.dot_general(a, b, (((1,), (1,)), ((), ())), preferred_element_type=F32)


def _split_dot(a, b_bf16):
    hi = a.astype(BF16)
    lo = (a - hi.astype(F32)).astype(BF16)
    return _dot(hi, b_bf16) + _dot(lo, b_bf16)


def _layer_norm(r, g, b):
    mu = jnp.mean(r, axis=-1, keepdims=True)
    d = r - mu
    var = jnp.mean(d * d, axis=-1, keepdims=True)
    return d * lax.rsqrt(var + LN_EPS) * g + b


def _proj_kernel(x_ref, w_ref, o_ref, xb_ref):
    @pl.when(pl.program_id(1) == 0)
    def _():
        xb_ref[...] = x_ref[...].astype(BF16)

    o_ref[...] = _dot(xb_ref[...], w_ref[...]).astype(o_ref.dtype)


def _proj(x, w, tm_target=1024, tn_target=512):
    t, k = x.shape
    n = w.shape[1]
    tm, tn = _tile(t, tm_target), _tile(n, tn_target)
    return pl.pallas_call(
        _proj_kernel,
        out_shape=jax.ShapeDtypeStruct((t, n), BF16),
        grid=(t // tm, n // tn),
        in_specs=[pl.BlockSpec((tm, k), lambda i, j: (i, 0)),
                  pl.BlockSpec((k, tn), lambda i, j: (0, j))],
        out_specs=pl.BlockSpec((tm, tn), lambda i, j: (i, j)),
        scratch_shapes=[pltpu.VMEM((tm, k), BF16)],
        compiler_params=_params("parallel", "arbitrary"),
        name="proj",
    )(x, w)


def _out_ln_kernel(alpha, n_a, *refs):
    a_refs = refs[:n_a]
    w_ref, x_ref, g_ref, b_ref, o_ref = refs[n_a:]
    ka = a_refs[0].shape[1]
    y = _dot(a_refs[0][...], w_ref[0:ka, :])
    for idx in range(1, n_a):
        y = y + _dot(a_refs[idx][...], w_ref[idx * ka:(idx + 1) * ka, :])
    o_ref[...] = _layer_norm(alpha * x_ref[...] + y, g_ref[...], b_ref[...])


def _out_ln(a_list, w, x, g, b, alpha, tm_target=512):
    t, d = x.shape
    tm = _tile(t, tm_target)
    ka = a_list[0].shape[1]
    row = lambda i: (i, 0)
    fixed = lambda i: (0, 0)
    return pl.pallas_call(
        functools.partial(_out_ln_kernel, alpha, len(a_list)),
        out_shape=jax.ShapeDtypeStruct((t, d), F32),
        grid=(t // tm,),
        in_specs=[pl.BlockSpec((tm, ka), row) for _ in a_list]
        + [pl.BlockSpec(w.shape, fixed), pl.BlockSpec((tm, d), row),
           pl.BlockSpec((1, d), fixed), pl.BlockSpec((1, d), fixed)],
        out_specs=pl.BlockSpec((tm, d), row),
        compiler_params=_params("parallel"),
        name="out_ln",
    )(*a_list, w, x, g.reshape(1, d), b.reshape(1, d))


def _sb_kernel(q_ref, k_ref, v_ref, o_ref, acc_ref):
    tq = q_ref.shape[1]
    qi = pl.program_id(2)
    lane = lax.broadcasted_iota(jnp.int32, (1, LANES), 1)
    head0 = lane < HEAD_DIM
    row = lax.broadcasted_iota(jnp.int32, (tq, tq), 0)
    col = lax.broadcasted_iota(jnp.int32, (tq, tq), 1)
    after = (row > col).astype(BF16)
    strictly_causal = col < row
    q = q_ref[0] * jnp.asarray(HEAD_DIM ** -0.5, BF16)

    def block(qh, kb, carry, diagonal):
        start = pl.multiple_of(kb * tq, tq)
        k = k_ref[0, pl.ds(start, tq), :]
        v = v_ref[0, pl.ds(start, tq), :]
        z = _dot_nt(qh, k)
        log_1m = -(jnp.maximum(z, 0.0) + jnp.log1p(jnp.exp(-jnp.abs(z))))
        if diagonal:
            log_1m = jnp.where(strictly_causal, log_1m, 0.0)
        suffix = _dot(log_1m.astype(BF16), after)
        w = jnp.exp((z + log_1m) + suffix + carry)
        if diagonal:
            w = jnp.where(strictly_causal, w, 0.0)
        return _dot(w.astype(BF16), v), carry + suffix[:, 0:1] + log_1m[:, 0:1]

    for h in range(2):
        qh = jnp.where(head0 if h == 0 else jnp.logical_not(head0), q, jnp.zeros_like(q))
        pv, carry = block(qh, qi, jnp.zeros((tq, 1), F32), True)
        acc_ref[h] = pv

        def body(it, carry, qh=qh, h=h):
            pv, carry = block(qh, qi - 1 - it, carry, False)
            acc_ref[h] += pv
            return carry

        lax.fori_loop(0, qi, body, carry)

    o_ref[0] = jnp.where(head0, acc_ref[0], acc_ref[1]).astype(o_ref.dtype)


def _stick_breaking(h3, n_heads, col0, tq_target=256):
    bsz, s, _ = h3.shape
    tq = _tile(s, tq_target)
    pairs = n_heads // 2
    return pl.pallas_call(
        _sb_kernel,
        out_shape=jax.ShapeDtypeStruct((bsz, s, pairs * LANES), BF16),
        grid=(bsz, pairs, s // tq),
        in_specs=[pl.BlockSpec((1, tq, LANES), lambda b, p, i: (b, i, col0 + p)),
                  pl.BlockSpec((1, s, LANES), lambda b, p, i: (b, 0, col0 + pairs + p)),
                  pl.BlockSpec((1, s, LANES), lambda b, p, i: (b, 0, col0 + 2 * pairs + p))],
        out_specs=pl.BlockSpec((1, tq, LANES), lambda b, p, i: (b, i, p)),
        scratch_shapes=[pltpu.VMEM((2, tq, LANES), F32)],
        compiler_params=_params("parallel", "parallel", "arbitrary"),
        name="stick_breaking",
    )(h3, h3, h3)


def _ret_kernel(q_ref, k_ref, v_ref, gate_ref, cos_ref, sin_ref, din_ref, dq_ref, dk_ref,
                dc_ref, gn_ref, o_ref, state_ref):
    @pl.when(pl.program_id(2) == 0)
    def _():
        state_ref[...] = jnp.zeros_like(state_ref)

    lane = lax.broadcasted_iota(jnp.int32, (1, LANES), 1)
    head0 = lane < HEAD_DIM
    first_half = (lane & (HEAD_DIM // 2)) == 0
    row = lax.broadcasted_iota(jnp.int32, (LANES, LANES), 0) < HEAD_DIM
    col = lax.broadcasted_iota(jnp.int32, (LANES, LANES), 1) < HEAD_DIM
    same_head = row == col
    group_mean = jnp.where(same_head, 1.0 / HEAD_DIM, 0.0).astype(BF16)
    cos = cos_ref[...]
    sin = sin_ref[...]

    def rotary(x):
        half = HEAD_DIM // 2
        swapped = jnp.where(first_half, pltpu.roll(x, LANES - half, 1), pltpu.roll(x, half, 1))
        return x * cos + swapped * sin

    q = rotary(q_ref[0].astype(F32)) * (HEAD_DIM ** -0.5)
    k = rotary(k_ref[0].astype(F32))
    v = v_ref[0]
    qb = q.astype(BF16)
    kb = k.astype(BF16)

    intra = []
    for h in range(2):
        qh = jnp.where(head0 if h == 0 else jnp.logical_not(head0), qb, jnp.zeros_like(qb))
        inner = _dot_nt(qh, kb) * din_ref[h]
        intra.append(_dot(inner.astype(BF16), v))
    state = state_ref[...]
    cross = _dot(qb, state.astype(BF16)) * dq_ref[0]
    o = jnp.where(head0, intra[0], intra[1]) + cross

    kd = (k * dk_ref[0]).T.astype(BF16)
    state_ref[...] = state * dc_ref[0] + jnp.where(same_head, _dot(kd, v), 0.0)

    mu = _split_dot(o, group_mean)
    d = o - mu
    var = _split_dot(d * d, group_mean)
    y = d * lax.rsqrt(var + LN_EPS) * gn_ref[...]
    g = gate_ref[0].astype(F32)
    o_ref[0] = (g / (1.0 + jnp.exp(-g)) * y).astype(o_ref.dtype)


def _retention(h3, gn_g, n_heads, col0, chunk_target=256):
    bsz, s, _ = h3.shape
    c = _tile(s, chunk_target)
    pairs = n_heads // 2
    half = HEAD_DIM // 2

    pos = jnp.arange(s, dtype=F32)
    inv_freq = ROPE_BASE ** (-jnp.arange(half, dtype=F32) / half)
    ang = pos[:, None] * inv_freq[None, :]
    cos = jnp.tile(jnp.cos(ang), (1, LANES // half))
    sin = jnp.tile(jnp.concatenate([-jnp.sin(ang), jnp.sin(ang)], axis=1), (1, LANES // HEAD_DIM))

    gamma = 1.0 - 2.0 ** (-5.0 - jnp.arange(n_heads, dtype=F32))
    log_g = jnp.log(gamma)
    i = jnp.arange(c, dtype=F32)
    diff = i[:, None] - i[None, :]
    d_in = jnp.where(diff >= 0, jnp.exp(log_g[:, None, None] * jnp.maximum(diff, 0.0)), 0.0)
    per_lane = lambda t: jnp.repeat(t.reshape(pairs, 2, -1), HEAD_DIM, axis=1)
    d_q = jnp.swapaxes(per_lane(jnp.exp(log_g[:, None] * (i + 1.0))), 1, 2)
    d_k = jnp.swapaxes(per_lane(jnp.exp(log_g[:, None] * (c - 1.0 - i))), 1, 2)
    d_c = jnp.swapaxes(per_lane(jnp.exp(log_g * c)[:, None]), 1, 2)

    blk = lambda off: pl.BlockSpec((1, c, LANES), lambda b, p, t: (b, t, col0 + off * pairs + p))
    return pl.pallas_call(
        _ret_kernel,
        out_shape=jax.ShapeDtypeStruct((bsz, s, pairs * LANES), BF16),
        grid=(bsz, pairs, s // c),
        in_specs=[blk(0), blk(1), blk(2), blk(3),
                  pl.BlockSpec((c, LANES), lambda b, p, t: (t, 0)),
                  pl.BlockSpec((c, LANES), lambda b, p, t: (t, 0)),
                  pl.BlockSpec((2, c, c), lambda b, p, t: (p, 0, 0)),
                  pl.BlockSpec((1, c, LANES), lambda b, p, t: (p, 0, 0)),
                  pl.BlockSpec((1, c, LANES), lambda b, p, t: (p, 0, 0)),
                  pl.BlockSpec((1, 1, LANES), lambda b, p, t: (p, 0, 0)),
                  pl.BlockSpec((1, LANES), lambda b, p, t: (0, p))],
        out_specs=pl.BlockSpec((1, c, LANES), lambda b, p, t: (b, t, p)),
        scratch_shapes=[pltpu.VMEM((LANES, LANES), F32)],
        compiler_params=_params("parallel", "parallel", "arbitrary"),
        name="retention",
    )(h3, h3, h3, h3, cos, sin, d_in, d_q, d_k, d_c, gn_g.reshape(1, -1))


def _chunk_attn_kernel(n_kb, q_ref, k_ref, v_ref, bias_ref, o_ref):
    tq = q_ref.shape[1]
    t0 = pl.program_id(2) * tq
    lane = lax.broadcasted_iota(jnp.int32, (1, LANES), 1)
    head0 = lane < HEAD_DIM
    q = q_ref[0] * jnp.asarray(HEAD_DIM ** -0.5, BF16)

    starts = []
    for kb in range(n_kb):
        start = t0 + (kb - (n_kb - 1)) * tq
        starts.append((start, pl.multiple_of(jnp.maximum(start, 0), tq)))

    outs = []
    for h in range(2):
        qh = jnp.where(head0 if h == 0 else jnp.logical_not(head0), q, jnp.zeros_like(q))
        scores = []
        for kb, (start, clamped) in enumerate(starts):
            k = k_ref[0, pl.ds(clamped, tq), :]
            before_seq = jnp.where(start >= 0, 0.0, NEG_BIG)
            scores.append(_dot_nt(qh, k) + bias_ref[h, :, kb * tq:(kb + 1) * tq] + before_seq)
        s = jnp.concatenate(scores, axis=1)
        p = jnp.exp(s - jnp.max(s, axis=-1, keepdims=True))
        denom = jnp.sum(p, axis=-1, keepdims=True)
        pb = p.astype(BF16)
        pv = None
        for kb, (start, clamped) in enumerate(starts):
            v = v_ref[0, pl.ds(clamped, tq), :]
            part = _dot(pb[:, kb * tq:(kb + 1) * tq], v)
            pv = part if pv is None else pv + part
        outs.append(pv / denom)

    o_ref[0] = jnp.where(head0, outs[0], outs[1]).astype(o_ref.dtype)


def _chunk_attention(qkv3, rel_bias, n_heads, tq=128):
    bsz, s, _ = qkv3.shape
    pairs = n_heads // 2
    n_kb = (LEFT_CHUNKS * CHUNK) // tq + 1
    width = n_kb * tq
    i = jnp.arange(tq)[:, None]
    j = jnp.arange(width)[None, :]
    rel = i + LEFT_CHUNKS * CHUNK - j
    q_chunk = i // CHUNK + LEFT_CHUNKS
    k_chunk = j // CHUNK
    visible = (k_chunk >= q_chunk - LEFT_CHUNKS) & (k_chunk <= q_chunk)
    bias = rel_bias[:, jnp.clip(rel, -MAX_REL, MAX_REL) + MAX_REL].astype(F32)
    bias = jnp.where(visible[None], bias, NEG_BIG)

    return pl.pallas_call(
        functools.partial(_chunk_attn_kernel, n_kb),
        out_shape=jax.ShapeDtypeStruct((bsz, s, pairs * LANES), BF16),
        grid=(bsz, pairs, s // tq),
        in_specs=[pl.BlockSpec((1, tq, LANES), lambda b, p, t: (b, t, p)),
                  pl.BlockSpec((1, s, LANES), lambda b, p, t: (b, 0, pairs + p)),
                  pl.BlockSpec((1, s, LANES), lambda b, p, t: (b, 0, 2 * pairs + p)),
                  pl.BlockSpec((2, tq, width), lambda b, p, t: (p, 0, 0))],
        out_specs=pl.BlockSpec((1, tq, LANES), lambda b, p, t: (b, t, p)),
        compiler_params=_params("parallel", "parallel", "arbitrary"),
        name="chunk_attention",
    )(qkv3, qkv3, qkv3, bias)


def _swiglu_ln_kernel(alpha, x_ref, wg_ref, wu_ref, wd_ref, g_ref, b_ref, o_ref, xb_ref, acc_ref):
    f = pl.program_id(1)

    @pl.when(f == 0)
    def _():
        xb_ref[...] = x_ref[...].astype(BF16)
        acc_ref[...] = jnp.zeros_like(acc_ref)

    xb = xb_ref[...]
    gate = _dot(xb, wg_ref[...])
    up = _dot(xb, wu_ref[...])
    hidden = (gate / (1.0 + jnp.exp(-gate)) * up).astype(BF16)
    acc_ref[...] += _dot(hidden, wd_ref[...])

    @pl.when(f == pl.num_programs(1) - 1)
    def _():
        o_ref[...] = _layer_norm(alpha * x_ref[...] + acc_ref[...], g_ref[...], b_ref[...])


def _swiglu_ln(x, wg, wu, wd, g, b, alpha, tm_target=1024, tf_target=256):
    t, d = x.shape
    ff = wg.shape[1]
    tm, tf = _tile(t, tm_target), _tile(ff, tf_target)
    return pl.pallas_call(
        functools.partial(_swiglu_ln_kernel, alpha),
        out_shape=jax.ShapeDtypeStruct((t, d), F32),
        grid=(t // tm, ff // tf),
        in_specs=[pl.BlockSpec((tm, d), lambda i, f: (i, 0)),
                  pl.BlockSpec((d, tf), lambda i, f: (0, f)),
                  pl.BlockSpec((d, tf), lambda i, f: (0, f)),
                  pl.BlockSpec((tf, d), lambda i, f: (f, 0)),
                  pl.BlockSpec((1, d), lambda i, f: (0, 0)),
                  pl.BlockSpec((1, d), lambda i, f: (0, 0))],
        out_specs=pl.BlockSpec((tm, d), lambda i, f: (i, 0)),
        scratch_shapes=[pltpu.VMEM((tm, d), BF16), pltpu.VMEM((tm, d), F32)],
        compiler_params=_params("parallel", "arbitrary"),
        name="swiglu_ln",
    )(x, wg, wu, wd, g.reshape(1, d), b.reshape(1, d))


def _router_kernel(n_experts, x_ref, w_ref, b_ref, o_ref):
    x = x_ref[...]
    w = w_ref[...]
    x_hi = x.astype(BF16)
    x_lo = (x - x_hi.astype(F32)).astype(BF16)
    w_hi = w.astype(BF16)
    w_lo = (w - w_hi.astype(F32)).astype(BF16)
    logits = _dot(x_hi, w_hi) + (_dot(x_hi, w_lo) + _dot(x_lo, w_hi)) + b_ref[...]
    lane = lax.broadcasted_iota(jnp.int32, logits.shape, 1).astype(F32)
    neg_inf = jnp.asarray(-jnp.inf, F32)
    logits = jnp.where(lane < n_experts, logits, neg_inf)
    top1 = jnp.max(logits, axis=-1, keepdims=True)
    idx1 = jnp.min(jnp.where(logits == top1, lane, float(LANES)), axis=-1, keepdims=True)
    rest = jnp.where(lane == idx1, neg_inf, logits)
    top2 = jnp.max(rest, axis=-1, keepdims=True)
    idx2 = jnp.min(jnp.where(rest == top2, lane, float(LANES)), axis=-1, keepdims=True)
    e2 = jnp.exp(top2 - top1)
    denom = 1.0 + e2
    o_ref[...] = jnp.where(lane == idx1, 1.0 / denom, 0.0) + jnp.where(lane == idx2, e2 / denom, 0.0)


def _router(x, w_router, b_router, tm_target=1024):
    t, d = x.shape
    n_experts = w_router.shape[1]
    tm = _tile(t, tm_target)
    w = jnp.pad(w_router.astype(F32), ((0, 0), (0, LANES - n_experts)))
    b = jnp.pad(b_router.astype(F32), (0, LANES - n_experts)).reshape(1, LANES)
    return pl.pallas_call(
        functools.partial(_router_kernel, n_experts),
        out_shape=jax.ShapeDtypeStruct((t, LANES), F32),
        grid=(t // tm,),
        in_specs=[pl.BlockSpec((tm, d), lambda i: (i, 0)),
                  pl.BlockSpec((d, LANES), lambda i: (0, 0)),
                  pl.BlockSpec((1, LANES), lambda i: (0, 0))],
        out_specs=pl.BlockSpec((tm, LANES), lambda i: (i, 0)),
        compiler_params=_params("parallel"),
        name="router",
    )(x, w, b)


def _moe_ln_kernel(alpha, x_ref, c_ref, wg_ref, wu_ref, wd_ref, g_ref, b_ref, o_ref, xb_ref, acc_ref):
    e = pl.program_id(1)
    f = pl.program_id(2)
    first = jnp.logical_and(e == 0, f == 0)
    last = jnp.logical_and(e == pl.num_programs(1) - 1, f == pl.num_programs(2) - 1)

    @pl.when(first)
    def _():
        xb_ref[...] = x_ref[...].astype(BF16)
        acc_ref[...] = jnp.zeros_like(acc_ref)

    lane = lax.broadcasted_iota(jnp.int32, c_ref.shape, 1)
    weight = jnp.sum(jnp.where(lane == e, c_ref[...], 0.0), axis=-1, keepdims=True)
    xb = xb_ref[...]
    gate = _dot(xb, wg_ref[0])
    up = _dot(xb, wu_ref[0])
    hidden = (gate / (1.0 + jnp.exp(-gate)) * up).astype(BF16)
    acc_ref[...] += weight * _dot(hidden, wd_ref[0])

    @pl.when(last)
    def _():
        o_ref[...] = _layer_norm(alpha * x_ref[...] + acc_ref[...], g_ref[...], b_ref[...])


def _moe_ln(x, combine, wg, wu, wd, g, b, alpha, tm_target=1024, tf_target=512):
    t, d = x.shape
    n_experts, _, ff = wg.shape
    tm, tf = _tile(t, tm_target), _tile(ff, tf_target)
    return pl.pallas_call(
        functools.partial(_moe_ln_kernel, alpha),
        out_shape=jax.ShapeDtypeStruct((t, d), F32),
        grid=(t // tm, n_experts, ff // tf),
        in_specs=[pl.BlockSpec((tm, d), lambda i, e, f: (i, 0)),
                  pl.BlockSpec((tm, LANES), lambda i, e, f: (i, 0)),
                  pl.BlockSpec((1, d, tf), lambda i, e, f: (e, 0, f)),
                  pl.BlockSpec((1, d, tf), lambda i, e, f: (e, 0, f)),
                  pl.BlockSpec((1, tf, d), lambda i, e, f: (e, f, 0)),
                  pl.BlockSpec((1, d), lambda i, e, f: (0, 0)),
                  pl.BlockSpec((1, d), lambda i, e, f: (0, 0))],
        out_specs=pl.BlockSpec((tm, d), lambda i, e, f: (i, 0)),
        scratch_shapes=[pltpu.VMEM((tm, d), BF16), pltpu.VMEM((tm, d), F32)],
        compiler_params=_params("parallel", "arbitrary", "arbitrary"),
        name="moe_ln",
    )(x, combine, wg, wu, wd, g.reshape(1, d), b.reshape(1, d))


def _even_layer(x, bsz, s, alpha, ln1_g, ln1_b, w_in, ret_gn_g, w_out, ln2_g, ln2_b, w_gate, w_up, w_down):
    d = x.shape[1]
    n_heads = d // HEAD_DIM
    n_sb = n_heads // 2
    n_ret = n_heads - n_sb
    h = _proj(x, w_in.astype(BF16))
    h3 = h.reshape(bsz, s, h.shape[1])
    o_sb = _stick_breaking(h3, n_sb, 0)
    o_ret = _retention(h3, ret_gn_g, n_ret, 3 * n_sb * HEAD_DIM // LANES)
    x = _out_ln([o_sb.reshape(bsz * s, -1), o_ret.reshape(bsz * s, -1)], w_out.astype(BF16),
                x, ln1_g, ln1_b, alpha)
    return _swiglu_ln(x, w_gate.astype(BF16), w_up.astype(BF16), w_down.astype(BF16), ln2_g, ln2_b, alpha)


def _odd_layer(x, bsz, s, alpha, ln1_g, ln1_b, w_qkv, rel_bias, w_out, ln2_g, ln2_b, w_router, b_router,
               w_gate, w_up, w_down):
    d = x.shape[1]
    qkv = _proj(x, w_qkv.astype(BF16))
    o = _chunk_attention(qkv.reshape(bsz, s, 3 * d), rel_bias, d // HEAD_DIM)
    x = _out_ln([o.reshape(bsz * s, d)], w_out.astype(BF16), x, ln1_g, ln1_b, alpha)
    combine = _router(x, w_router, b_router)
    return _moe_ln(x, combine, w_gate.astype(BF16), w_up.astype(BF16), w_down.astype(BF16),
                   ln2_g, ln2_b, alpha)


def kernel(x, even_ln1_g, even_ln1_b, even_w_in, even_ret_gn_g, even_w_out, even_ln2_g, even_ln2_b,
           even_w_gate, even_w_up, even_w_down, odd_ln1_g, odd_ln1_b, odd_w_qkv, odd_rel_bias, odd_w_out,
           odd_ln2_g, odd_ln2_b, odd_w_router, odd_b_router, odd_w_gate, odd_w_up, odd_w_down):
    bsz, s, d = x.shape
    depth = even_w_in.shape[0] + odd_w_qkv.shape[0]
    alpha = (2 * depth) ** 0.25
    xt = x.reshape(bsz * s, d)
    for layer in range(depth):
        i = layer // 2
        if layer % 2 == 0:
            xt = _even_layer(xt, bsz, s, alpha, even_ln1_g[i], even_ln1_b[i], even_w_in[i], even_ret_gn_g[i],
                             even_w_out[i], even_ln2_g[i], even_ln2_b[i], even_w_gate[i], even_w_up[i],
                             even_w_down[i])
        else:
            xt = _odd_layer(xt, bsz, s, alpha, odd_ln1_g[i], odd_ln1_b[i], odd_w_qkv[i], odd_rel_bias[i],
                            odd_w_out[i], odd_ln2_g[i], odd_ln2_b[i], odd_w_router[i], odd_b_router[i],
                            odd_w_gate[i], odd_w_up[i], odd_w_down[i])
    return xt.reshape(bsz, s, d)
```

```python
import functools

import jax
import jax.numpy as jnp
from jax import lax
from jax.experimental import pallas as pl
from jax.experimental.pallas import tpu as pltpu

HEAD_DIM = 64
LANES = 128
CHUNK = 64
LEFT_CHUNKS = 8
MAX_REL = 4 * CHUNK
TOP_K = 2
ROPE_BASE = 10000.0
LN_EPS = 1e-5
NEG_BIG = -1e30
VMEM_LIMIT = 56 * 1024 * 1024

BF16 = jnp.bfloat16
F32 = jnp.float32


def _params(*sem):
    return pltpu.CompilerParams(dimension_semantics=sem, vmem_limit_bytes=VMEM_LIMIT)


def _tile(n, target):
    if n <= target:
        return n
    t = target - target % LANES
    while t >= LANES:
        if n % t == 0:
            return t
        t -= LANES
    return n


def _dot(a, b):
    return jnp.dot(a, b, preferred_element_type=F32)


def _dot_nt(a, b):
    return lax.dot_general(a, b, (((1,), (1,)), ((), ())), preferred_element_type=F32)


def _split_dot(a, b_bf16):
    hi = a.astype(BF16)
    lo = (a - hi.astype(F32)).astype(BF16)
    return _dot(hi, b_bf16) + _dot(lo, b_bf16)


def _layer_norm(r, g, b):
    mu = jnp.mean(r, axis=-1, keepdims=True)
    d = r - mu
    var = jnp.mean(d * d, axis=-1, keepdims=True)
    return d * lax.rsqrt(var + LN_EPS) * g + b


def _proj_kernel(x_ref, w_ref, o_ref, xb_ref):
    @pl.when(pl.program_id(1) == 0)
    def _():
        xb_ref[...] = x_ref[...].astype(BF16)

    o_ref[...] = _dot(xb_ref[...], w_ref[...]).astype(o_ref.dtype)


def _proj(x, w, tm_target=1024, tn_target=512):
    t, k = x.shape
    n = w.shape[1]
    tm, tn = _tile(t, tm_target), _tile(n, tn_target)
    return pl.pallas_call(
        _proj_kernel,
        out_shape=jax.ShapeDtypeStruct((t, n), BF16),
        grid=(t // tm, n // tn),
        in_specs=[pl.BlockSpec((tm, k), lambda i, j: (i, 0)),
                  pl.BlockSpec((k, tn), lambda i, j: (0, j))],
        out_specs=pl.BlockSpec((tm, tn), lambda i, j: (i, j)),
        scratch_shapes=[pltpu.VMEM((tm, k), BF16)],
        compiler_params=_params("parallel", "arbitrary"),
        name="proj",
    )(x, w)


def _out_ln_kernel(alpha, n_a, *refs):
    a_refs = refs[:n_a]
    w_ref, x_ref, g_ref, b_ref, o_ref = refs[n_a:]
    ka = a_refs[0].shape[1]
    y = _dot(a_refs[0][...], w_ref[0:ka, :])
    for idx in range(1, n_a):
        y = y + _dot(a_refs[idx][...], w_ref[idx * ka:(idx + 1) * ka, :])
    o_ref[...] = _layer_norm(alpha * x_ref[...] + y, g_ref[...], b_ref[...])


def _out_ln(a_list, w, x, g, b, alpha, tm_target=512):
    t, d = x.shape
    tm = _tile(t, tm_target)
    ka = a_list[0].shape[1]
    row = lambda i: (i, 0)
    fixed = lambda i: (0, 0)
    return pl.pallas_call(
        functools.partial(_out_ln_kernel, alpha, len(a_list)),
        out_shape=jax.ShapeDtypeStruct((t, d), F32),
        grid=(t // tm,),
        in_specs=[pl.BlockSpec((tm, ka), row) for _ in a_list]
        + [pl.BlockSpec(w.shape, fixed), pl.BlockSpec((tm, d), row),
           pl.BlockSpec((1, d), fixed), pl.BlockSpec((1, d), fixed)],
        out_specs=pl.BlockSpec((tm, d), row),
        compiler_params=_params("parallel"),
        name="out_ln",
    )(*a_list, w, x, g.reshape(1, d), b.reshape(1, d))


def _sb_kernel(q_ref, k_ref, v_ref, o_ref, acc_ref, z_ref):
    tq = q_ref.shape[1]
    qi = pl.program_id(2)
    lane = lax.broadcasted_iota(jnp.int32, (1, LANES), 1)
    head0 = lane < HEAD_DIM
    row = lax.broadcasted_iota(jnp.int32, (tq, tq), 0)
    col = lax.broadcasted_iota(jnp.int32, (tq, tq), 1)
    after = (row > col).astype(BF16)
    strictly_causal = col < row
    q = q_ref[0] * jnp.asarray(HEAD_DIM ** -0.5, BF16)
    q_heads = (jnp.where(head0, q, jnp.zeros_like(q)), jnp.where(head0, jnp.zeros_like(q), q))

    def key_block(ref, kb):
        return ref[0, pl.ds(pl.multiple_of(kb * tq, tq), tq), :]

    def step(kb, carries, diagonal=False):
        log_betas, log_1ms, suffixes = [], [], []
        for h in range(2):
            z = z_ref[h]
            neg_z = -z
            log_1m = jnp.minimum(neg_z, 0.0) - jnp.log(1.0 + jnp.exp(jnp.minimum(z, neg_z)))
            log_betas.append(z + log_1m)
            if diagonal:
                log_1m = jnp.where(strictly_causal, log_1m, 0.0)
            log_1ms.append(log_1m[:, 0:1])
            suffixes.append(_dot(log_1m.astype(BF16), after))
        k_next = key_block(k_ref, jnp.maximum(kb - 1, 0))
        for h in range(2):
            z_ref[h] = _dot_nt(q_heads[h], k_next)
        v = key_block(v_ref, kb)
        new_carries = []
        for h in range(2):
            w = jnp.exp(log_betas[h] + suffixes[h] + carries[h])
            if diagonal:
                w = jnp.where(strictly_causal, w, 0.0)
            pv = _dot(w.astype(BF16), v)
            if diagonal:
                acc_ref[h] = pv
            else:
                acc_ref[h] += pv
            new_carries.append(carries[h] + suffixes[h][:, 0:1] + log_1ms[h])
        return tuple(new_carries)

    k_diag = key_block(k_ref, qi)
    for h in range(2):
        z_ref[h] = _dot_nt(q_heads[h], k_diag)
    zero = jnp.zeros((tq, 1), F32)
    carries = step(qi, (zero, zero), True)
    lax.fori_loop(0, qi, lambda it, carries: step(qi - 1 - it, carries), carries)
    o_ref[0] = jnp.where(head0, acc_ref[0], acc_ref[1]).astype(o_ref.dtype)


def _stick_breaking(h3, n_heads, col0, tq_target=256):
    bsz, s, _ = h3.shape
    tq = _tile(s, tq_target)
    pairs = n_heads // 2
    return pl.pallas_call(
        _sb_kernel,
        out_shape=jax.ShapeDtypeStruct((bsz, s, pairs * LANES), BF16),
        grid=(bsz, pairs, s // tq),
        in_specs=[pl.BlockSpec((1, tq, LANES), lambda b, p, i: (b, i, col0 + p)),
                  pl.BlockSpec((1, s, LANES), lambda b, p, i: (b, 0, col0 + pairs + p)),
                  pl.BlockSpec((1, s, LANES), lambda b, p, i: (b, 0, col0 + 2 * pairs + p))],
        out_specs=pl.BlockSpec((1, tq, LANES), lambda b, p, i: (b, i, p)),
        scratch_shapes=[pltpu.VMEM((2, tq, LANES), F32), pltpu.VMEM((2, tq, tq), F32)],
        compiler_params=_params("parallel", "parallel", "arbitrary"),
        name="stick_breaking",
    )(h3, h3, h3)


def _ret_kernel(q_ref, k_ref, v_ref, gate_ref, cos_ref, sin_ref, din_ref, dq_ref, dk_ref,
                dc_ref, gn_ref, o_ref, state_ref):
    @pl.when(pl.program_id(2) == 0)
    def _():
        state_ref[...] = jnp.zeros_like(state_ref)

    lane = lax.broadcasted_iota(jnp.int32, (1, LANES), 1)
    head0 = lane < HEAD_DIM
    first_half = (lane & (HEAD_DIM // 2)) == 0
    row = lax.broadcasted_iota(jnp.int32, (LANES, LANES), 0) < HEAD_DIM
    col = lax.broadcasted_iota(jnp.int32, (LANES, LANES), 1) < HEAD_DIM
    same_head = row == col
    group_mean = jnp.where(same_head, 1.0 / HEAD_DIM, 0.0).astype(BF16)
    cos = cos_ref[...]
    sin = sin_ref[...]

    def rotary(x):
        half = HEAD_DIM // 2
        swapped = jnp.where(first_half, pltpu.roll(x, LANES - half, 1), pltpu.roll(x, half, 1))
        return x * cos + swapped * sin

    q = rotary(q_ref[0].astype(F32)) * (HEAD_DIM ** -0.5)
    k = rotary(k_ref[0].astype(F32))
    v = v_ref[0]
    qb = q.astype(BF16)
    kb = k.astype(BF16)

    intra = []
    for h in range(2):
        qh = jnp.where(head0 if h == 0 else jnp.logical_not(head0), qb, jnp.zeros_like(qb))
        inner = _dot_nt(qh, kb) * din_ref[h]
        intra.append(_dot(inner.astype(BF16), v))
    state = state_ref[...]
    cross = _dot(qb, state.astype(BF16)) * dq_ref[0]
    o = jnp.where(head0, intra[0], intra[1]) + cross

    kd = (k * dk_ref[0]).T.astype(BF16)
    state_ref[...] = state * dc_ref[0] + jnp.where(same_head, _dot(kd, v), 0.0)

    mu = _split_dot(o, group_mean)
    d = o - mu
    var = _split_dot(d * d, group_mean)
    y = d * lax.rsqrt(var + LN_EPS) * gn_ref[...]
    g = gate_ref[0].astype(F32)
    o_ref[0] = (g / (1.0 + jnp.exp(-g)) * y).astype(o_ref.dtype)


def _retention(h3, gn_g, n_heads, col0, chunk_target=256):
    bsz, s, _ = h3.shape
    c = _tile(s, chunk_target)
    pairs = n_heads // 2
    half = HEAD_DIM // 2

    pos = jnp.arange(s, dtype=F32)
    inv_freq = ROPE_BASE ** (-jnp.arange(half, dtype=F32) / half)
    ang = pos[:, None] * inv_freq[None, :]
    cos = jnp.tile(jnp.cos(ang), (1, LANES // half))
    sin = jnp.tile(jnp.concatenate([-jnp.sin(ang), jnp.sin(ang)], axis=1), (1, LANES // HEAD_DIM))

    gamma = 1.0 - 2.0 ** (-5.0 - jnp.arange(n_heads, dtype=F32))
    log_g = jnp.log(gamma)
    i = jnp.arange(c, dtype=F32)
    diff = i[:, None] - i[None, :]
    d_in = jnp.where(diff >= 0, jnp.exp(log_g[:, None, None] * jnp.maximum(diff, 0.0)), 0.0)
    per_lane = lambda t: jnp.repeat(t.reshape(pairs, 2, -1), HEAD_DIM, axis=1)
    d_q = jnp.swapaxes(per_lane(jnp.exp(log_g[:, None] * (i + 1.0))), 1, 2)
    d_k = jnp.swapaxes(per_lane(jnp.exp(log_g[:, None] * (c - 1.0 - i))), 1, 2)
    d_c = jnp.swapaxes(per_lane(jnp.exp(log_g * c)[:, None]), 1, 2)

    blk = lambda off: pl.BlockSpec((1, c, LANES), lambda b, p, t: (b, t, col0 + off * pairs + p))
    return pl.pallas_call(
        _ret_kernel,
        out_shape=jax.ShapeDtypeStruct((bsz, s, pairs * LANES), BF16),
        grid=(bsz, pairs, s // c),
        in_specs=[blk(0), blk(1), blk(2), blk(3),
                  pl.BlockSpec((c, LANES), lambda b, p, t: (t, 0)),
                  pl.BlockSpec((c, LANES), lambda b, p, t: (t, 0)),
                  pl.BlockSpec((2, c, c), lambda b, p, t: (p, 0, 0)),
                  pl.BlockSpec((1, c, LANES), lambda b, p, t: (p, 0, 0)),
                  pl.BlockSpec((1, c, LANES), lambda b, p, t: (p, 0, 0)),
                  pl.BlockSpec((1, 1, LANES), lambda b, p, t: (p, 0, 0)),
                  pl.BlockSpec((1, LANES), lambda b, p, t: (0, p))],
        out_specs=pl.BlockSpec((1, c, LANES), lambda b, p, t: (b, t, p)),
        scratch_shapes=[pltpu.VMEM((LANES, LANES), F32)],
        compiler_params=_params("parallel", "parallel", "arbitrary"),
        name="retention",
    )(h3, h3, h3, h3, cos, sin, d_in, d_q, d_k, d_c, gn_g.reshape(1, -1))


def _chunk_attn_kernel(q_ref, k_ref, v_ref, bias_ref, o_ref):
    tq = q_ref.shape[1]
    t0 = pl.program_id(2) * tq
    lane = lax.broadcasted_iota(jnp.int32, (1, LANES), 1)
    head0 = lane < HEAD_DIM
    q = q_ref[0] * jnp.asarray(HEAD_DIM ** -0.5, BF16)
    q_heads = (jnp.where(head0, q, jnp.zeros_like(q)), jnp.where(head0, jnp.zeros_like(q), q))

    width = bias_ref.shape[3]
    start = pl.multiple_of(jnp.maximum(t0 - LEFT_CHUNKS * CHUNK, 0), tq)
    k = k_ref[0, pl.ds(start, width), :]
    v = v_ref[0, pl.ds(start, width), :]
    logits = [_dot_nt(qh, k) for qh in q_heads]
    probs, denoms = [], []
    for h in range(2):
        s = logits[h] + bias_ref[0, 0, h * tq:(h + 1) * tq, :]
        p = jnp.exp(s - jnp.max(s, axis=-1, keepdims=True))
        denoms.append(jnp.sum(p, axis=-1, keepdims=True))
        probs.append(p.astype(BF16))
    outs = [_dot(probs[h], v) / denoms[h] for h in range(2)]
    o_ref[0] = jnp.where(head0, outs[0], outs[1]).astype(o_ref.dtype)


def _window_bias(rel_bias, tq, width):
    n_heads = rel_bias.shape[0]
    left = LEFT_CHUNKS * CHUNK
    rel_min, rel_max = left - (width - 1), left + tq - 1
    assert rel_min >= -MAX_REL and rel_max >= MAX_REL
    by_rel = jnp.concatenate(
        [rel_bias[:, rel_min + MAX_REL:], jnp.broadcast_to(rel_bias[:, -1:], (n_heads, rel_max - MAX_REL))], axis=1)
    m = rel_max - rel_min + 1
    rev = jnp.concatenate([by_rel[:, ::-1], jnp.zeros((n_heads, 1), by_rel.dtype)], axis=1)
    rows = jnp.tile(rev, (1, tq))[:, :tq * m].reshape(n_heads, tq, m)
    bias = rows[:, :, tq - 1:tq - 1 + width].astype(F32)
    q_chunk = jnp.arange(tq)[:, None] // CHUNK + LEFT_CHUNKS
    k_chunk = jnp.arange(width)[None, :] // CHUNK
    visible = (k_chunk >= q_chunk - LEFT_CHUNKS) & (k_chunk <= q_chunk)
    return jnp.where(visible[None], bias, NEG_BIG)


def _chunk_attention(qkv3, rel_bias, n_heads, tq=256):
    bsz, s, _ = qkv3.shape
    pairs = n_heads // 2
    n_shift = (LEFT_CHUNKS * CHUNK) // tq
    width = (n_shift + 1) * tq
    assert s >= width
    bias = _window_bias(rel_bias, tq, width)
    bias = jnp.stack([jnp.pad(bias[:, :, v * tq:], ((0, 0), (0, 0), (0, v * tq)), constant_values=NEG_BIG)
                      for v in range(n_shift + 1)], axis=1).reshape(pairs, 2, n_shift + 1, tq, width)
    bias = bias.transpose(0, 2, 1, 3, 4).reshape(pairs, n_shift + 1, 2 * tq, width)

    return pl.pallas_call(
        _chunk_attn_kernel,
        out_shape=jax.ShapeDtypeStruct((bsz, s, pairs * LANES), BF16),
        grid=(bsz, pairs, s // tq),
        in_specs=[pl.BlockSpec((1, tq, LANES), lambda b, p, t: (b, t, p)),
                  pl.BlockSpec((1, s, LANES), lambda b, p, t: (b, 0, pairs + p)),
                  pl.BlockSpec((1, s, LANES), lambda b, p, t: (b, 0, 2 * pairs + p)),
                  pl.BlockSpec((1, 1, 2 * tq, width), lambda b, p, t: (p, jnp.maximum(n_shift - t, 0), 0, 0))],
        out_specs=pl.BlockSpec((1, tq, LANES), lambda b, p, t: (b, t, p)),
        compiler_params=_params("parallel", "parallel", "arbitrary"),
        name="chunk_attention",
    )(qkv3, qkv3, qkv3, bias)


def _swiglu_ln_kernel(alpha, x_ref, wg_ref, wu_ref, wd_ref, g_ref, b_ref, o_ref, xb_ref, acc_ref):
    f = pl.program_id(1)

    @pl.when(f == 0)
    def _():
        xb_ref[...] = x_ref[...].astype(BF16)
        acc_ref[...] = jnp.zeros_like(acc_ref)

    xb = xb_ref[...]
    gate = _dot(xb, wg_ref[...])
    up = _dot(xb, wu_ref[...])
    hidden = (gate / (1.0 + jnp.exp(-gate)) * up).astype(BF16)
    acc_ref[...] += _dot(hidden, wd_ref[...])

    @pl.when(f == pl.num_programs(1) - 1)
    def _():
        o_ref[...] = _layer_norm(alpha * x_ref[...] + acc_ref[...], g_ref[...], b_ref[...])


def _swiglu_ln(x, wg, wu, wd, g, b, alpha, tm_target=1024, tf_target=256):
    t, d = x.shape
    ff = wg.shape[1]
    tm, tf = _tile(t, tm_target), _tile(ff, tf_target)
    return pl.pallas_call(
        functools.partial(_swiglu_ln_kernel, alpha),
        out_shape=jax.ShapeDtypeStruct((t, d), F32),
        grid=(t // tm, ff // tf),
        in_specs=[pl.BlockSpec((tm, d), lambda i, f: (i, 0)),
                  pl.BlockSpec((d, tf), lambda i, f: (0, f)),
                  pl.BlockSpec((d, tf), lambda i, f: (0, f)),
                  pl.BlockSpec((tf, d), lambda i, f: (f, 0)),
                  pl.BlockSpec((1, d), lambda i, f: (0, 0)),
                  pl.BlockSpec((1, d), lambda i, f: (0, 0))],
        out_specs=pl.BlockSpec((tm, d), lambda i, f: (i, 0)),
        scratch_shapes=[pltpu.VMEM((tm, d), BF16), pltpu.VMEM((tm, d), F32)],
        compiler_params=_params("parallel", "arbitrary"),
        name="swiglu_ln",
    )(x, wg, wu, wd, g.reshape(1, d), b.reshape(1, d))


def _router_kernel(n_experts, x_ref, w_ref, b_ref, o_ref):
    x = x_ref[...]
    w = w_ref[...]
    x_hi = x.astype(BF16)
    x_lo = (x - x_hi.astype(F32)).astype(BF16)
    w_hi = w.astype(BF16)
    w_lo = (w - w_hi.astype(F32)).astype(BF16)
    logits = _dot(x_hi, w_hi) + (_dot(x_hi, w_lo) + _dot(x_lo, w_hi)) + b_ref[...]
    lane = lax.broadcasted_iota(jnp.int32, logits.shape, 1).astype(F32)
    neg_inf = jnp.asarray(-jnp.inf, F32)
    logits = jnp.where(lane < n_experts, logits, neg_inf)
    top1 = jnp.max(logits, axis=-1, keepdims=True)
    idx1 = jnp.min(jnp.where(logits == top1, lane, float(LANES)), axis=-1, keepdims=True)
    rest = jnp.where(lane == idx1, neg_inf, logits)
    top2 = jnp.max(rest, axis=-1, keepdims=True)
    idx2 = jnp.min(jnp.where(rest == top2, lane, float(LANES)), axis=-1, keepdims=True)
    e2 = jnp.exp(top2 - top1)
    denom = 1.0 + e2
    o_ref[...] = jnp.where(lane == idx1, 1.0 / denom, 0.0) + jnp.where(lane == idx2, e2 / denom, 0.0)


def _router(x, w_router, b_router, tm_target=1024):
    t, d = x.shape
    n_experts = w_router.shape[1]
    tm = _tile(t, tm_target)
    w = jnp.pad(w_router.astype(F32), ((0, 0), (0, LANES - n_experts)))
    b = jnp.pad(b_router.astype(F32), (0, LANES - n_experts)).reshape(1, LANES)
    return pl.pallas_call(
        functools.partial(_router_kernel, n_experts),
        out_shape=jax.ShapeDtypeStruct((t, LANES), F32),
        grid=(t // tm,),
        in_specs=[pl.BlockSpec((tm, d), lambda i: (i, 0)),
                  pl.BlockSpec((d, LANES), lambda i: (0, 0)),
                  pl.BlockSpec((1, LANES), lambda i: (0, 0))],
        out_specs=pl.BlockSpec((tm, LANES), lambda i: (i, 0)),
        compiler_params=_params("parallel"),
        name="router",
    )(x, w, b)


def _moe_ln_kernel(alpha, x_ref, c_ref, wg_ref, wu_ref, wd_ref, g_ref, b_ref, o_ref, xb_ref, acc_ref):
    e = pl.program_id(1)
    f = pl.program_id(2)
    first = jnp.logical_and(e == 0, f == 0)
    last = jnp.logical_and(e == pl.num_programs(1) - 1, f == pl.num_programs(2) - 1)

    @pl.when(first)
    def _():
        xb_ref[...] = x_ref[...].astype(BF16)
        acc_ref[...] = jnp.zeros_like(acc_ref)

    lane = lax.broadcasted_iota(jnp.int32, c_ref.shape, 1)
    weight = jnp.sum(jnp.where(lane == e, c_ref[...], 0.0), axis=-1, keepdims=True)
    xb = xb_ref[...]
    gate = _dot(xb, wg_ref[0])
    up = _dot(xb, wu_ref[0])
    hidden = (gate / (1.0 + jnp.exp(-gate)) * up).astype(BF16)
    acc_ref[...] += weight * _dot(hidden, wd_ref[0])

    @pl.when(last)
    def _():
        o_ref[...] = _layer_norm(alpha * x_ref[...] + acc_ref[...], g_ref[...], b_ref[...])


def _moe_ln(x, combine, wg, wu, wd, g, b, alpha, tm_target=1024, tf_target=512):
    t, d = x.shape
    n_experts, _, ff = wg.shape
    tm, tf = _tile(t, tm_target), _tile(ff, tf_target)
    return pl.pallas_call(
        functools.partial(_moe_ln_kernel, alpha),
        out_shape=jax.ShapeDtypeStruct((t, d), F32),
        grid=(t // tm, n_experts, ff // tf),
        in_specs=[pl.BlockSpec((tm, d), lambda i, e, f: (i, 0)),
                  pl.BlockSpec((tm, LANES), lambda i, e, f: (i, 0)),
                  pl.BlockSpec((1, d, tf), lambda i, e, f: (e, 0, f)),
                  pl.BlockSpec((1, d, tf), lambda i, e, f: (e, 0, f)),
                  pl.BlockSpec((1, tf, d), lambda i, e, f: (e, f, 0)),
                  pl.BlockSpec((1, d), lambda i, e, f: (0, 0)),
                  pl.BlockSpec((1, d), lambda i, e, f: (0, 0))],
        out_specs=pl.BlockSpec((tm, d), lambda i, e, f: (i, 0)),
        scratch_shapes=[pltpu.VMEM((tm, d), BF16), pltpu.VMEM((tm, d), F32)],
        compiler_params=_params("parallel", "arbitrary", "arbitrary"),
        name="moe_ln",
    )(x, combine, wg, wu, wd, g.reshape(1, d), b.reshape(1, d))


def _even_layer(x, bsz, s, alpha, ln1_g, ln1_b, w_in, ret_gn_g, w_out, ln2_g, ln2_b, w_gate, w_up, w_down):
    d = x.shape[1]
    n_heads = d // HEAD_DIM
    n_sb = n_heads // 2
    n_ret = n_heads - n_sb
    h = _proj(x, w_in.astype(BF16))
    h3 = h.reshape(bsz, s, h.shape[1])
    o_sb = _stick_breaking(h3, n_sb, 0)
    o_ret = _retention(h3, ret_gn_g, n_ret, 3 * n_sb * HEAD_DIM // LANES)
    x = _out_ln([o_sb.reshape(bsz * s, -1), o_ret.reshape(bsz * s, -1)], w_out.astype(BF16),
                x, ln1_g, ln1_b, alpha)
    return _swiglu_ln(x, w_gate.astype(BF16), w_up.astype(BF16), w_down.astype(BF16), ln2_g, ln2_b, alpha)


def _odd_layer(x, bsz, s, alpha, ln1_g, ln1_b, w_qkv, rel_bias, w_out, ln2_g, ln2_b, w_router, b_router,
               w_gate, w_up, w_down):
    d = x.shape[1]
    qkv = _proj(x, w_qkv.astype(BF16))
    o = _chunk_attention(qkv.reshape(bsz, s, 3 * d), rel_bias, d // HEAD_DIM)
    x = _out_ln([o.reshape(bsz * s, d)], w_out.astype(BF16), x, ln1_g, ln1_b, alpha)
    combine = _router(x, w_router, b_router)
    return _moe_ln(x, combine, w_gate.astype(BF16), w_up.astype(BF16), w_down.astype(BF16),
                   ln2_g, ln2_b, alpha)


def kernel(x, even_ln1_g, even_ln1_b, even_w_in, even_ret_gn_g, even_w_out, even_ln2_g, even_ln2_b,
           even_w_gate, even_w_up, even_w_down, odd_ln1_g, odd_ln1_b, odd_w_qkv, odd_rel_bias, odd_w_out,
           odd_ln2_g, odd_ln2_b, odd_w_router, odd_b_router, odd_w_gate, odd_w_up, odd_w_down):
    bsz, s, d = x.shape
    depth = even_w_in.shape[0] + odd_w_qkv.shape[0]
    alpha = (2 * depth) ** 0.25
    xt = x.reshape(bsz * s, d)
    for layer in range(depth):
        i = layer // 2
        if layer % 2 == 0:
            xt = _even_layer(xt, bsz, s, alpha, even_ln1_g[i], even_ln1_b[i], even_w_in[i], even_ret_gn_g[i],
                             even_w_out[i], even_ln2_g[i], even_ln2_b[i], even_w_gate[i], even_w_up[i],
                             even_w_down[i])
        else:
            xt = _odd_layer(xt, bsz, s, alpha, odd_ln1_g[i], odd_ln1_b[i], odd_w_qkv[i], odd_rel_bias[i],
                            odd_w_out[i], odd_ln2_g[i], odd_ln2_b[i], odd_w_router[i], odd_b_router[i],
                            odd_w_gate[i], odd_w_up[i], odd_w_down[i])
    return xt.reshape(bsz, s, d)
```

```python
import functools

import jax
import jax.numpy as jnp
from jax import lax
from jax.experimental import pallas as pl
from jax.experimental.pallas import tpu as pltpu

HEAD_DIM = 64
LANES = 128
CHUNK = 64
LEFT_CHUNKS = 8
MAX_REL = 4 * CHUNK
TOP_K = 2
ROPE_BASE = 10000.0
LN_EPS = 1e-5
NEG_BIG = -1e30
VMEM_LIMIT = 56 * 1024 * 1024

BF16 = jnp.bfloat16
F32 = jnp.float32


def _params(*sem):
    return pltpu.CompilerParams(dimension_semantics=sem, vmem_limit_bytes=VMEM_LIMIT)


def _tile(n, target):
    if n <= target:
        return n
    t = target - target % LANES
    while t >= LANES:
        if n % t == 0:
            return t
        t -= LANES
    return n


def _dot(a, b):
    return jnp.dot(a, b, preferred_element_type=F32)


def _dot_nt(a, b):
    return lax.dot_general(a, b, (((1,), (1,)), ((), ())), preferred_element_type=F32)


def _split_dot(a, b_bf16):
    hi = a.astype(BF16)
    lo = (a - hi.astype(F32)).astype(BF16)
    return _dot(hi, b_bf16) + _dot(lo, b_bf16)


def _layer_norm(r, g, b):
    mu = jnp.mean(r, axis=-1, keepdims=True)
    d = r - mu
    var = jnp.mean(d * d, axis=-1, keepdims=True)
    return d * lax.rsqrt(var + LN_EPS) * g + b


def _proj_kernel(x_ref, w_ref, o_ref, xb_ref):
    @pl.when(pl.program_id(1) == 0)
    def _():
        xb_ref[...] = x_ref[...].astype(BF16)

    o_ref[...] = _dot(xb_ref[...], w_ref[...]).astype(o_ref.dtype)


def _proj(x, w, tm_target=1024, tn_target=512):
    t, k = x.shape
    n = w.shape[1]
    tm, tn = _tile(t, tm_target), _tile(n, tn_target)
    return pl.pallas_call(
        _proj_kernel,
        out_shape=jax.ShapeDtypeStruct((t, n), BF16),
        grid=(t // tm, n // tn),
        in_specs=[pl.BlockSpec((tm, k), lambda i, j: (i, 0)),
                  pl.BlockSpec((k, tn), lambda i, j: (0, j))],
        out_specs=pl.BlockSpec((tm, tn), lambda i, j: (i, j)),
        scratch_shapes=[pltpu.VMEM((tm, k), BF16)],
        compiler_params=_params("parallel", "arbitrary"),
        name="proj",
    )(x, w)


def _out_ln_kernel(alpha, n_a, *refs):
    a_refs = refs[:n_a]
    w_ref, x_ref, g_ref, b_ref = refs[n_a:n_a + 4]
    o_refs = refs[n_a + 4:]
    ka = a_refs[0].shape[1]
    y = _dot(a_refs[0][...], w_ref[0:ka, :])
    for idx in range(1, n_a):
        y = y + _dot(a_refs[idx][...], w_ref[idx * ka:(idx + 1) * ka, :])
    out = _layer_norm(alpha * x_ref[...] + y, g_ref[...], b_ref[...])
    for o_ref in o_refs:
        o_ref[...] = out.astype(o_ref.dtype)


def _out_ln(a_list, w, x, g, b, alpha, also_bf16=False, tm_target=512):
    t, d = x.shape
    tm = _tile(t, tm_target)
    ka = a_list[0].shape[1]
    row = lambda i: (i, 0)
    fixed = lambda i: (0, 0)
    dtypes = [F32, BF16] if also_bf16 else [F32]
    outs = pl.pallas_call(
        functools.partial(_out_ln_kernel, alpha, len(a_list)),
        out_shape=[jax.ShapeDtypeStruct((t, d), dt) for dt in dtypes],
        grid=(t // tm,),
        in_specs=[pl.BlockSpec((tm, ka), row) for _ in a_list]
        + [pl.BlockSpec(w.shape, fixed), pl.BlockSpec((tm, d), row),
           pl.BlockSpec((1, d), fixed), pl.BlockSpec((1, d), fixed)],
        out_specs=[pl.BlockSpec((tm, d), row) for _ in dtypes],
        compiler_params=_params("parallel"),
        name="out_ln",
    )(*a_list, w, x, g.reshape(1, d), b.reshape(1, d))
    return outs if also_bf16 else outs[0]


def _sb_kernel(q_ref, k_ref, v_ref, o_ref, acc_ref, z_ref):
    tq = q_ref.shape[1]
    qi = pl.program_id(2)
    lane = lax.broadcasted_iota(jnp.int32, (1, LANES), 1)
    head0 = lane < HEAD_DIM
    row = lax.broadcasted_iota(jnp.int32, (tq, tq), 0)
    col = lax.broadcasted_iota(jnp.int32, (tq, tq), 1)
    after = (row > col).astype(BF16)
    strictly_causal = col < row
    q = q_ref[0] * jnp.asarray(HEAD_DIM ** -0.5, BF16)
    q_heads = (jnp.where(head0, q, jnp.zeros_like(q)), jnp.where(head0, jnp.zeros_like(q), q))

    def key_block(ref, kb):
        return ref[0, pl.ds(pl.multiple_of(kb * tq, tq), tq), :]

    def step(kb, carries, diagonal=False):
        log_betas, log_1ms, suffixes = [], [], []
        for h in range(2):
            z = z_ref[h]
            neg_z = -z
            log_1m = jnp.minimum(neg_z, 0.0) - jnp.log(1.0 + jnp.exp(jnp.minimum(z, neg_z)))
            log_betas.append(z + log_1m)
            if diagonal:
                log_1m = jnp.where(strictly_causal, log_1m, 0.0)
            log_1ms.append(log_1m[:, 0:1])
            suffixes.append(_dot(log_1m.astype(BF16), after))
        k_next = key_block(k_ref, jnp.maximum(kb - 1, 0))
        for h in range(2):
            z_ref[h] = _dot_nt(q_heads[h], k_next)
        v = key_block(v_ref, kb)
        new_carries = []
        for h in range(2):
            w = jnp.exp(log_betas[h] + suffixes[h] + carries[h])
            if diagonal:
                w = jnp.where(strictly_causal, w, 0.0)
            pv = _dot(w.astype(BF16), v)
            if diagonal:
                acc_ref[h] = pv
            else:
                acc_ref[h] += pv
            new_carries.append(carries[h] + suffixes[h][:, 0:1] + log_1ms[h])
        return tuple(new_carries)

    k_diag = key_block(k_ref, qi)
    for h in range(2):
        z_ref[h] = _dot_nt(q_heads[h], k_diag)
    zero = jnp.zeros((tq, 1), F32)
    carries = step(qi, (zero, zero), True)
    lax.fori_loop(0, qi, lambda it, carries: step(qi - 1 - it, carries), carries)
    o_ref[0] = jnp.where(head0, acc_ref[0], acc_ref[1]).astype(o_ref.dtype)


def _stick_breaking(h3, n_heads, col0, tq_target=256):
    bsz, s, _ = h3.shape
    tq = _tile(s, tq_target)
    pairs = n_heads // 2
    return pl.pallas_call(
        _sb_kernel,
        out_shape=jax.ShapeDtypeStruct((bsz, s, pairs * LANES), BF16),
        grid=(bsz, pairs, s // tq),
        in_specs=[pl.BlockSpec((1, tq, LANES), lambda b, p, i: (b, i, col0 + p)),
                  pl.BlockSpec((1, s, LANES), lambda b, p, i: (b, 0, col0 + pairs + p)),
                  pl.BlockSpec((1, s, LANES), lambda b, p, i: (b, 0, col0 + 2 * pairs + p))],
        out_specs=pl.BlockSpec((1, tq, LANES), lambda b, p, i: (b, i, p)),
        scratch_shapes=[pltpu.VMEM((2, tq, LANES), F32), pltpu.VMEM((2, tq, tq), F32)],
        compiler_params=_params("parallel", "parallel", "arbitrary"),
        name="stick_breaking",
    )(h3, h3, h3)


def _ret_kernel(q_ref, k_ref, v_ref, gate_ref, cos_ref, sin_ref, din_ref, dq_ref, dk_ref,
                dc_ref, gn_ref, o_ref, state_ref):
    @pl.when(pl.program_id(2) == 0)
    def _():
        state_ref[...] = jnp.zeros_like(state_ref)

    lane = lax.broadcasted_iota(jnp.int32, (1, LANES), 1)
    head0 = lane < HEAD_DIM
    first_half = (lane & (HEAD_DIM // 2)) == 0
    row = lax.broadcasted_iota(jnp.int32, (LANES, LANES), 0) < HEAD_DIM
    col = lax.broadcasted_iota(jnp.int32, (LANES, LANES), 1) < HEAD_DIM
    same_head = row == col
    group_mean = jnp.where(same_head, 1.0 / HEAD_DIM, 0.0).astype(BF16)
    cos = cos_ref[...]
    sin = sin_ref[...]

    def rotary(x):
        half = HEAD_DIM // 2
        swapped = jnp.where(first_half, pltpu.roll(x, LANES - half, 1), pltpu.roll(x, half, 1))
        return x * cos + swapped * sin

    q = rotary(q_ref[0].astype(F32)) * (HEAD_DIM ** -0.5)
    k = rotary(k_ref[0].astype(F32))
    v = v_ref[0]
    qb = q.astype(BF16)
    kb = k.astype(BF16)

    intra = []
    for h in range(2):
        qh = jnp.where(head0 if h == 0 else jnp.logical_not(head0), qb, jnp.zeros_like(qb))
        inner = _dot_nt(qh, kb) * din_ref[h]
        intra.append(_dot(inner.astype(BF16), v))
    state = state_ref[...]
    cross = _dot(qb, state.astype(BF16)) * dq_ref[0]
    o = jnp.where(head0, intra[0], intra[1]) + cross

    kd = (k * dk_ref[0]).T.astype(BF16)
    state_ref[...] = state * dc_ref[0] + jnp.where(same_head, _dot(kd, v), 0.0)

    mu = _split_dot(o, group_mean)
    d = o - mu
    var = _split_dot(d * d, group_mean)
    y = d * lax.rsqrt(var + LN_EPS) * gn_ref[...]
    g = gate_ref[0].astype(F32)
    o_ref[0] = (g / (1.0 + jnp.exp(-g)) * y).astype(o_ref.dtype)


def _retention(h3, gn_g, n_heads, col0, chunk_target=256):
    bsz, s, _ = h3.shape
    c = _tile(s, chunk_target)
    pairs = n_heads // 2
    half = HEAD_DIM // 2

    pos = jnp.arange(s, dtype=F32)
    inv_freq = ROPE_BASE ** (-jnp.arange(half, dtype=F32) / half)
    ang = pos[:, None] * inv_freq[None, :]
    cos = jnp.tile(jnp.cos(ang), (1, LANES // half))
    sin = jnp.tile(jnp.concatenate([-jnp.sin(ang), jnp.sin(ang)], axis=1), (1, LANES // HEAD_DIM))

    gamma = 1.0 - 2.0 ** (-5.0 - jnp.arange(n_heads, dtype=F32))
    log_g = jnp.log(gamma)
    i = jnp.arange(c, dtype=F32)
    diff = i[:, None] - i[None, :]
    d_in = jnp.where(diff >= 0, jnp.exp(log_g[:, None, None] * jnp.maximum(diff, 0.0)), 0.0)
    per_lane = lambda t: jnp.repeat(t.reshape(pairs, 2, -1), HEAD_DIM, axis=1)
    d_q = jnp.swapaxes(per_lane(jnp.exp(log_g[:, None] * (i + 1.0))), 1, 2)
    d_k = jnp.swapaxes(per_lane(jnp.exp(log_g[:, None] * (c - 1.0 - i))), 1, 2)
    d_c = jnp.swapaxes(per_lane(jnp.exp(log_g * c)[:, None]), 1, 2)

    blk = lambda off: pl.BlockSpec((1, c, LANES), lambda b, p, t: (b, t, col0 + off * pairs + p))
    return pl.pallas_call(
        _ret_kernel,
        out_shape=jax.ShapeDtypeStruct((bsz, s, pairs * LANES), BF16),
        grid=(bsz, pairs, s // c),
        in_specs=[blk(0), blk(1), blk(2), blk(3),
                  pl.BlockSpec((c, LANES), lambda b, p, t: (t, 0)),
                  pl.BlockSpec((c, LANES), lambda b, p, t: (t, 0)),
                  pl.BlockSpec((2, c, c), lambda b, p, t: (p, 0, 0)),
                  pl.BlockSpec((1, c, LANES), lambda b, p, t: (p, 0, 0)),
                  pl.BlockSpec((1, c, LANES), lambda b, p, t: (p, 0, 0)),
                  pl.BlockSpec((1, 1, LANES), lambda b, p, t: (p, 0, 0)),
                  pl.BlockSpec((1, LANES), lambda b, p, t: (0, p))],
        out_specs=pl.BlockSpec((1, c, LANES), lambda b, p, t: (b, t, p)),
        scratch_shapes=[pltpu.VMEM((LANES, LANES), F32)],
        compiler_params=_params("parallel", "parallel", "arbitrary"),
        name="retention",
    )(h3, h3, h3, h3, cos, sin, d_in, d_q, d_k, d_c, gn_g.reshape(1, -1))


def _chunk_attn_kernel(q_ref, k_ref, v_ref, bias_ref, o_ref):
    tq = q_ref.shape[1]
    t0 = pl.program_id(2) * tq
    lane = lax.broadcasted_iota(jnp.int32, (1, LANES), 1)
    head0 = lane < HEAD_DIM
    q = q_ref[0] * jnp.asarray(HEAD_DIM ** -0.5, BF16)
    q_heads = (jnp.where(head0, q, jnp.zeros_like(q)), jnp.where(head0, jnp.zeros_like(q), q))

    width = bias_ref.shape[3]
    start = pl.multiple_of(jnp.maximum(t0 - LEFT_CHUNKS * CHUNK, 0), tq)
    k = k_ref[0, pl.ds(start, width), :]
    v = v_ref[0, pl.ds(start, width), :]
    logits = [_dot_nt(qh, k) for qh in q_heads]
    probs, denoms = [], []
    for h in range(2):
        s = logits[h] + bias_ref[0, 0, h * tq:(h + 1) * tq, :]
        p = jnp.exp(s - jnp.max(s, axis=-1, keepdims=True))
        denoms.append(jnp.sum(p, axis=-1, keepdims=True))
        probs.append(p.astype(BF16))
    outs = [_dot(probs[h], v) / denoms[h] for h in range(2)]
    o_ref[0] = jnp.where(head0, outs[0], outs[1]).astype(o_ref.dtype)


def _window_bias(rel_bias, tq, width):
    n_heads = rel_bias.shape[0]
    left = LEFT_CHUNKS * CHUNK
    rel_min, rel_max = left - (width - 1), left + tq - 1
    assert rel_min >= -MAX_REL and rel_max >= MAX_REL
    by_rel = jnp.concatenate(
        [rel_bias[:, rel_min + MAX_REL:], jnp.broadcast_to(rel_bias[:, -1:], (n_heads, rel_max - MAX_REL))], axis=1)
    m = rel_max - rel_min + 1
    rev = jnp.concatenate([by_rel[:, ::-1], jnp.zeros((n_heads, 1), by_rel.dtype)], axis=1)
    rows = jnp.tile(rev, (1, tq))[:, :tq * m].reshape(n_heads, tq, m)
    bias = rows[:, :, tq - 1:tq - 1 + width].astype(F32)
    q_chunk = jnp.arange(tq)[:, None] // CHUNK + LEFT_CHUNKS
    k_chunk = jnp.arange(width)[None, :] // CHUNK
    visible = (k_chunk >= q_chunk - LEFT_CHUNKS) & (k_chunk <= q_chunk)
    return jnp.where(visible[None], bias, NEG_BIG)


def _chunk_attention(qkv3, rel_bias, n_heads, tq=256):
    bsz, s, _ = qkv3.shape
    pairs = n_heads // 2
    n_shift = (LEFT_CHUNKS * CHUNK) // tq
    width = (n_shift + 1) * tq
    assert s >= width
    bias = _window_bias(rel_bias, tq, width)
    bias = jnp.stack([jnp.pad(bias[:, :, v * tq:], ((0, 0), (0, 0), (0, v * tq)), constant_values=NEG_BIG)
                      for v in range(n_shift + 1)], axis=1).reshape(pairs, 2, n_shift + 1, tq, width)
    bias = bias.transpose(0, 2, 1, 3, 4).reshape(pairs, n_shift + 1, 2 * tq, width)

    return pl.pallas_call(
        _chunk_attn_kernel,
        out_shape=jax.ShapeDtypeStruct((bsz, s, pairs * LANES), BF16),
        grid=(bsz, pairs, s // tq),
        in_specs=[pl.BlockSpec((1, tq, LANES), lambda b, p, t: (b, t, p)),
                  pl.BlockSpec((1, s, LANES), lambda b, p, t: (b, 0, pairs + p)),
                  pl.BlockSpec((1, s, LANES), lambda b, p, t: (b, 0, 2 * pairs + p)),
                  pl.BlockSpec((1, 1, 2 * tq, width), lambda b, p, t: (p, jnp.maximum(n_shift - t, 0), 0, 0))],
        out_specs=pl.BlockSpec((1, tq, LANES), lambda b, p, t: (b, t, p)),
        compiler_params=_params("parallel", "parallel", "arbitrary"),
        name="chunk_attention",
    )(qkv3, qkv3, qkv3, bias)


def _swiglu_ln_kernel(alpha, x_ref, wg_ref, wu_ref, wd_ref, g_ref, b_ref, o_ref, xb_ref, acc_ref):
    f = pl.program_id(1)

    @pl.when(f == 0)
    def _():
        xb_ref[...] = x_ref[...].astype(BF16)
        acc_ref[...] = jnp.zeros_like(acc_ref)

    xb = xb_ref[...]
    gate = _dot(xb, wg_ref[...])
    up = _dot(xb, wu_ref[...])
    hidden = (gate / (1.0 + jnp.exp(-gate)) * up).astype(BF16)
    acc_ref[...] += _dot(hidden, wd_ref[...])

    @pl.when(f == pl.num_programs(1) - 1)
    def _():
        o_ref[...] = _layer_norm(alpha * x_ref[...] + acc_ref[...], g_ref[...], b_ref[...])


def _swiglu_ln(x, wg, wu, wd, g, b, alpha, tm_target=1024, tf_target=256):
    t, d = x.shape
    ff = wg.shape[1]
    tm, tf = _tile(t, tm_target), _tile(ff, tf_target)
    return pl.pallas_call(
        functools.partial(_swiglu_ln_kernel, alpha),
        out_shape=jax.ShapeDtypeStruct((t, d), F32),
        grid=(t // tm, ff // tf),
        in_specs=[pl.BlockSpec((tm, d), lambda i, f: (i, 0)),
                  pl.BlockSpec((d, tf), lambda i, f: (0, f)),
                  pl.BlockSpec((d, tf), lambda i, f: (0, f)),
                  pl.BlockSpec((tf, d), lambda i, f: (f, 0)),
                  pl.BlockSpec((1, d), lambda i, f: (0, 0)),
                  pl.BlockSpec((1, d), lambda i, f: (0, 0))],
        out_specs=pl.BlockSpec((tm, d), lambda i, f: (i, 0)),
        scratch_shapes=[pltpu.VMEM((tm, d), BF16), pltpu.VMEM((tm, d), F32)],
        compiler_params=_params("parallel", "arbitrary"),
        name="swiglu_ln",
    )(x, wg, wu, wd, g.reshape(1, d), b.reshape(1, d))


ROUTE_E1, ROUTE_E2, ROUTE_RANK1, ROUTE_RANK2, ROUTE_G1, ROUTE_G2 = range(6)
ROUTE_FIELDS = 8
MOE_TILE = 512


def _route_kernel(n_experts, x_ref, w_ref, b_ref, col_ref, row_ref, cnt_ref, run_ref):
    @pl.when(pl.program_id(0) == 0)
    def _():
        run_ref[...] = jnp.zeros_like(run_ref)

    x = x_ref[...]
    w = w_ref[...]
    tm = x.shape[0]
    x_hi = x.astype(BF16)
    x_lo = (x - x_hi.astype(F32)).astype(BF16)
    w_hi = w.astype(BF16)
    w_lo = (w - w_hi.astype(F32)).astype(BF16)
    logits = _dot(x_hi, w_hi) + (_dot(x_hi, w_lo) + _dot(x_lo, w_hi)) + b_ref[...]
    lane = lax.broadcasted_iota(jnp.int32, logits.shape, 1).astype(F32)
    neg_inf = jnp.asarray(-jnp.inf, F32)
    logits = jnp.where(lane < n_experts, logits, neg_inf)
    top1 = jnp.max(logits, axis=-1, keepdims=True)
    idx1 = jnp.min(jnp.where(logits == top1, lane, float(LANES)), axis=-1, keepdims=True)
    rest = jnp.where(lane == idx1, neg_inf, logits)
    top2 = jnp.max(rest, axis=-1, keepdims=True)
    idx2 = jnp.min(jnp.where(rest == top2, lane, float(LANES)), axis=-1, keepdims=True)
    e2 = jnp.exp(top2 - top1)
    denom = 1.0 + e2

    chosen = jnp.where(jnp.logical_or(lane == idx1, lane == idx2), 1.0, 0.0)
    r = lax.broadcasted_iota(jnp.int32, (tm, tm), 0)
    c = lax.broadcasted_iota(jnp.int32, (tm, tm), 1)
    earlier = (c < r).astype(BF16)
    before = _dot(earlier, chosen.astype(BF16)) + run_ref[...]
    rank1 = jnp.sum(jnp.where(lane == idx1, before, 0.0), axis=-1, keepdims=True)
    rank2 = jnp.sum(jnp.where(lane == idx2, before, 0.0), axis=-1, keepdims=True)
    total = jnp.sum(chosen, axis=0, keepdims=True)
    cnt_ref[0] = total
    run_ref[...] += total

    fields = {ROUTE_E1: idx1, ROUTE_E2: idx2, ROUTE_RANK1: rank1, ROUTE_RANK2: rank2,
              ROUTE_G1: 1.0 / denom, ROUTE_G2: e2 / denom}
    meta = jnp.zeros_like(logits)
    for field, value in fields.items():
        meta = jnp.where(lane == field, value, meta)
    col_ref[...] = meta
    row_ref[0] = meta.T[0:ROUTE_FIELDS, :]


def _route(x, w_router, b_router, tm):
    t, d = x.shape
    n_experts = w_router.shape[1]
    nb = t // tm
    w = jnp.pad(w_router.astype(F32), ((0, 0), (0, LANES - n_experts)))
    b = jnp.pad(b_router.astype(F32), (0, LANES - n_experts)).reshape(1, LANES)
    return pl.pallas_call(
        functools.partial(_route_kernel, n_experts),
        out_shape=[jax.ShapeDtypeStruct((t, LANES), F32),
                   jax.ShapeDtypeStruct((nb, ROUTE_FIELDS, tm), F32),
                   jax.ShapeDtypeStruct((nb, 1, LANES), F32)],
        grid=(nb,),
        in_specs=[pl.BlockSpec((tm, d), lambda i: (i, 0)),
                  pl.BlockSpec((d, LANES), lambda i: (0, 0)),
                  pl.BlockSpec((1, LANES), lambda i: (0, 0))],
        out_specs=[pl.BlockSpec((tm, LANES), lambda i: (i, 0)),
                   pl.BlockSpec((1, ROUTE_FIELDS, tm), lambda i: (i, 0, 0)),
                   pl.BlockSpec((1, 1, LANES), lambda i: (i, 0, 0))],
        scratch_shapes=[pltpu.VMEM((1, LANES), F32)],
        compiler_params=_params("arbitrary"),
        name="route",
    )(x, w, b)


def _moe_plan(cnt, n_experts, tile, n_tiles):
    nb = cnt.shape[0]
    counts = cnt[:, 0, :n_experts].astype(jnp.int32)
    cum = jnp.cumsum(counts, axis=0) - counts
    total = jnp.sum(counts, axis=0)
    tiles_e = (total + tile - 1) // tile
    tile_end = jnp.cumsum(tiles_e)
    row_off = (tile_end - tiles_e) * tile
    start = row_off[None, :] + cum
    j0 = start // tile
    j1 = (start + counts - 1) // tile
    has = counts > 0
    tiles = jnp.stack([j0, j0 + 1], axis=-1)
    valid = jnp.stack([has, has & (j1 > j0)], axis=-1)

    def ordered(tiles, valid):
        tiles, valid = tiles.reshape(-1), valid.reshape(-1)
        n = tiles.shape[0]
        last = lax.cummax(jnp.where(valid, jnp.arange(n), -1), axis=0)
        prev = jnp.concatenate([jnp.full((1,), -1, last.dtype), last[:-1]])
        filled = jnp.where(last >= 0, tiles[jnp.maximum(last, 0)], 0)
        prev_tile = jnp.where(prev >= 0, tiles[jnp.maximum(prev, 0)], -1)
        first = valid & (tiles != prev_tile)
        return filled.astype(jnp.int32), valid.astype(jnp.int32) + first.astype(jnp.int32)

    by_expert = ordered(tiles.transpose(1, 0, 2), valid.transpose(1, 0, 2))
    by_block = ordered(tiles, valid)
    n_used = tile_end[-1:].astype(jnp.int32)
    tile_expert = jnp.minimum(jnp.searchsorted(tile_end, jnp.arange(n_tiles), side="right"),
                              n_experts - 1).astype(jnp.int32)
    return dict(by_expert=by_expert, by_block=by_block, row_off=row_off.astype(jnp.int32),
                n_used=n_used, tile_expert=tile_expert)


def _expert_field(meta, expert, first_field, absent):
    return jnp.where(meta[ROUTE_E1] == expert, meta[first_field],
                     jnp.where(meta[ROUTE_E2] == expert, meta[first_field + 1], absent))


def _expert_rows(meta, expert, offset):
    rank = _expert_field(meta, expert.astype(F32), ROUTE_RANK1, -1.0)
    return jnp.where(rank >= 0.0, rank + offset.astype(F32), -1.0).astype(jnp.int32)


def _gather_kernel(tile_ref, flag_ref, off_ref, row_ref, col_ref, x_ref, xs_ref, gs_ref):
    expert = pl.program_id(0)
    pair = (expert * pl.num_programs(1) + pl.program_id(1)) * 2 + pl.program_id(2)
    flag = flag_ref[pair]
    tile, tm = xs_ref.shape[0], x_ref.shape[0]

    @pl.when(flag > 0)
    def _():
        by_row = [row_ref[0, f:f + 1, :] for f in range(ROUTE_FIELDS)]
        by_col = [col_ref[:, f:f + 1] for f in range(ROUTE_FIELDS)]
        rows = tile_ref[pair] * tile + lax.broadcasted_iota(jnp.int32, (tile, tm), 0)
        onehot = jnp.where(rows == _expert_rows(by_row, expert, off_ref[expert]), 1.0, 0.0).astype(BF16)
        xg = _dot(onehot, x_ref[...])
        gate = _expert_field(by_col, expert.astype(F32), ROUTE_G1, 0.0)
        gate_hi = gate.astype(BF16).astype(F32)
        lane = lax.broadcasted_iota(jnp.int32, (tm, LANES), 1)
        gate_pair = jnp.where(lane == 0, gate_hi, jnp.where(lane == 1, gate - gate_hi, 0.0))
        gg = _dot(onehot, gate_pair.astype(BF16))

        @pl.when(flag > 1)
        def _():
            xs_ref[...] = xg.astype(BF16)
            gs_ref[...] = gg

        @pl.when(flag == 1)
        def _():
            xs_ref[...] = (xs_ref[...].astype(F32) + xg).astype(BF16)
            gs_ref[...] += gg


def _moe_gather(plan, meta_row, meta_col, xb, n_experts, tile, n_tiles):
    t, d = xb.shape
    nb, _, tm = meta_row.shape
    tiles, flags = plan["by_expert"]
    pair = lambda e, b, j: (e * nb + b) * 2 + j
    return pl.pallas_call(
        _gather_kernel,
        out_shape=[jax.ShapeDtypeStruct((n_tiles * tile, d), BF16),
                   jax.ShapeDtypeStruct((n_tiles * tile, LANES), F32)],
        grid_spec=pltpu.PrefetchScalarGridSpec(
            num_scalar_prefetch=3,
            grid=(n_experts, nb, 2),
            in_specs=[pl.BlockSpec((1, ROUTE_FIELDS, tm), lambda e, b, j, *_: (b, 0, 0)),
                      pl.BlockSpec((tm, LANES), lambda e, b, j, *_: (b, 0)),
                      pl.BlockSpec((tm, d), lambda e, b, j, *_: (b, 0))],
            out_specs=[pl.BlockSpec((tile, d), lambda e, b, j, tiles, *_: (tiles[pair(e, b, j)], 0)),
                       pl.BlockSpec((tile, LANES), lambda e, b, j, tiles, *_: (tiles[pair(e, b, j)], 0))]),
        compiler_params=_params("arbitrary", "arbitrary", "arbitrary"),
        name="moe_gather",
    )(tiles, flags, plan["row_off"], meta_row, meta_col, xb)


def _moe_ffn_kernel(te_ref, used_ref, xs_ref, gs_ref, wg_ref, wu_ref, wd_ref, o_ref, acc_ref):
    f = pl.program_id(1)

    @pl.when(pl.program_id(0) < used_ref[0])
    def _():
        @pl.when(f == 0)
        def _():
            acc_ref[...] = jnp.zeros_like(acc_ref)

        xs = xs_ref[...]
        gate = _dot(xs, wg_ref[0])
        up = _dot(xs, wu_ref[0])
        hidden = (gate / (1.0 + jnp.exp(-gate)) * up).astype(BF16)
        acc_ref[...] += _dot(hidden, wd_ref[0])

        @pl.when(f == pl.num_programs(1) - 1)
        def _():
            o_ref[...] = (acc_ref[...] * (gs_ref[:, 0:1] + gs_ref[:, 1:2])).astype(o_ref.dtype)


def _moe_ffn(plan, xs, gs, wg, wu, wd, tile, tf_target=1792):
    rows, d = xs.shape
    ff = wg.shape[2]
    tf = _tile(ff, tf_target)
    nf = ff // tf

    def tile_of(i, used):
        return jnp.minimum(i, used[0] - 1)

    def chunk_of(i, f, used):
        return jnp.where(i < used[0], f, nf - 1)

    return pl.pallas_call(
        _moe_ffn_kernel,
        out_shape=jax.ShapeDtypeStruct((rows, d), BF16),
        grid_spec=pltpu.PrefetchScalarGridSpec(
            num_scalar_prefetch=2,
            grid=(rows // tile, nf),
            in_specs=[pl.BlockSpec((tile, d), lambda i, f, te, used: (tile_of(i, used), 0)),
                      pl.BlockSpec((tile, LANES), lambda i, f, te, used: (tile_of(i, used), 0)),
                      pl.BlockSpec((1, d, tf), lambda i, f, te, used: (te[tile_of(i, used)], 0, chunk_of(i, f, used))),
                      pl.BlockSpec((1, d, tf), lambda i, f, te, used: (te[tile_of(i, used)], 0, chunk_of(i, f, used))),
                      pl.BlockSpec((1, tf, d), lambda i, f, te, used: (te[tile_of(i, used)], chunk_of(i, f, used), 0))],
            out_specs=pl.BlockSpec((tile, d), lambda i, f, te, used: (tile_of(i, used), 0)),
            scratch_shapes=[pltpu.VMEM((tile, d), F32)]),
        compiler_params=_params("arbitrary", "arbitrary"),
        name="moe_ffn",
    )(plan["tile_expert"], plan["n_used"], xs, gs, wg, wu, wd)


def _combine_ln_kernel(alpha, tile_ref, flag_ref, off_ref, col_ref, os_ref, x_ref, g_ref, b_ref, o_ref, acc_ref):
    expert, j = pl.program_id(1), pl.program_id(2)
    pair = (pl.program_id(0) * pl.num_programs(1) + expert) * 2 + j
    tm, tile = x_ref.shape[0], os_ref.shape[0]

    @pl.when(jnp.logical_and(expert == 0, j == 0))
    def _():
        acc_ref[...] = jnp.zeros_like(acc_ref)

    @pl.when(flag_ref[pair] > 0)
    def _():
        by_col = [col_ref[:, f:f + 1] for f in range(ROUTE_FIELDS)]
        cols = tile_ref[pair] * tile + lax.broadcasted_iota(jnp.int32, (tm, tile), 1)
        onehot = jnp.where(cols == _expert_rows(by_col, expert, off_ref[expert]), 1.0, 0.0).astype(BF16)
        acc_ref[...] += _dot(onehot, os_ref[...])

    @pl.when(jnp.logical_and(expert == pl.num_programs(1) - 1, j == 1))
    def _():
        o_ref[...] = _layer_norm(alpha * x_ref[...] + acc_ref[...], g_ref[...], b_ref[...])


def _moe_combine_ln(plan, meta_col, o_sorted, x, g, b, alpha, n_experts, tile, tm):
    t, d = x.shape
    nb = t // tm
    tiles, flags = plan["by_block"]
    pair = lambda i, e, j: (i * n_experts + e) * 2 + j
    return pl.pallas_call(
        functools.partial(_combine_ln_kernel, alpha),
        out_shape=jax.ShapeDtypeStruct((t, d), F32),
        grid_spec=pltpu.PrefetchScalarGridSpec(
            num_scalar_prefetch=3,
            grid=(nb, n_experts, 2),
            in_specs=[pl.BlockSpec((tm, LANES), lambda i, e, j, *_: (i, 0)),
                      pl.BlockSpec((tile, d), lambda i, e, j, tiles, *_: (tiles[pair(i, e, j)], 0)),
                      pl.BlockSpec((tm, d), lambda i, e, j, *_: (i, 0)),
                      pl.BlockSpec((1, d), lambda i, e, j, *_: (0, 0)),
                      pl.BlockSpec((1, d), lambda i, e, j, *_: (0, 0))],
            out_specs=pl.BlockSpec((tm, d), lambda i, e, j, *_: (i, 0)),
            scratch_shapes=[pltpu.VMEM((tm, d), F32)]),
        compiler_params=_params("arbitrary", "arbitrary", "arbitrary"),
        name="moe_combine_ln",
    )(tiles, flags, plan["row_off"], meta_col, o_sorted, x, g.reshape(1, d), b.reshape(1, d))


def _moe_ln(x, xb, w_router, b_router, wg, wu, wd, g, b, alpha):
    t, _ = x.shape
    n_experts = wg.shape[0]
    tile = _tile(t, MOE_TILE)
    n_tiles = TOP_K * t // tile + n_experts
    meta_col, meta_row, cnt = _route(x, w_router, b_router, tile)
    plan = _moe_plan(cnt, n_experts, tile, n_tiles)
    xs, gs = _moe_gather(plan, meta_row, meta_col, xb, n_experts, tile, n_tiles)
    o_sorted = _moe_ffn(plan, xs, gs, wg, wu, wd, tile)
    return _moe_combine_ln(plan, meta_col, o_sorted, x, g, b, alpha, n_experts, tile, tile)


def _even_layer(x, bsz, s, alpha, ln1_g, ln1_b, w_in, ret_gn_g, w_out, ln2_g, ln2_b, w_gate, w_up, w_down):
    d = x.shape[1]
    n_heads = d // HEAD_DIM
    n_sb = n_heads // 2
    n_ret = n_heads - n_sb
    h = _proj(x, w_in.astype(BF16))
    h3 = h.reshape(bsz, s, h.shape[1])
    o_sb = _stick_breaking(h3, n_sb, 0)
    o_ret = _retention(h3, ret_gn_g, n_ret, 3 * n_sb * HEAD_DIM // LANES)
    x = _out_ln([o_sb.reshape(bsz * s, -1), o_ret.reshape(bsz * s, -1)], w_out.astype(BF16),
                x, ln1_g, ln1_b, alpha)
    return _swiglu_ln(x, w_gate.astype(BF16), w_up.astype(BF16), w_down.astype(BF16), ln2_g, ln2_b, alpha)


def _odd_layer(x, bsz, s, alpha, ln1_g, ln1_b, w_qkv, rel_bias, w_out, ln2_g, ln2_b, w_router, b_router,
               w_gate, w_up, w_down):
    d = x.shape[1]
    qkv = _proj(x, w_qkv.astype(BF16))
    o = _chunk_attention(qkv.reshape(bsz, s, 3 * d), rel_bias, d // HEAD_DIM)
    x, xb = _out_ln([o.reshape(bsz * s, d)], w_out.astype(BF16), x, ln1_g, ln1_b, alpha, also_bf16=True)
    return _moe_ln(x, xb, w_router, b_router, w_gate.astype(BF16), w_up.astype(BF16), w_down.astype(BF16),
                   ln2_g, ln2_b, alpha)


def kernel(x, even_ln1_g, even_ln1_b, even_w_in, even_ret_gn_g, even_w_out, even_ln2_g, even_ln2_b,
           even_w_gate, even_w_up, even_w_down, odd_ln1_g, odd_ln1_b, odd_w_qkv, odd_rel_bias, odd_w_out,
           odd_ln2_g, odd_ln2_b, odd_w_router, odd_b_router, odd_w_gate, odd_w_up, odd_w_down):
    bsz, s, d = x.shape
    depth = even_w_in.shape[0] + odd_w_qkv.shape[0]
    alpha = (2 * depth) ** 0.25
    xt = x.reshape(bsz * s, d)
    for layer in range(depth):
        i = layer // 2
        if layer % 2 == 0:
            xt = _even_layer(xt, bsz, s, alpha, even_ln1_g[i], even_ln1_b[i], even_w_in[i], even_ret_gn_g[i],
                             even_w_out[i], even_ln2_g[i], even_ln2_b[i], even_w_gate[i], even_w_up[i],
                             even_w_down[i])
        else:
            xt = _odd_layer(xt, bsz, s, alpha, odd_ln1_g[i], odd_ln1_b[i], odd_w_qkv[i], odd_rel_bias[i],
                            odd_w_out[i], odd_ln2_g[i], odd_ln2_b[i], odd_w_router[i], odd_b_router[i],
                            odd_w_gate[i], odd_w_up[i], odd_w_down[i])
    return xt.reshape(bsz, s, d)
```

```python
import functools

import jax
import jax.numpy as jnp
from jax import lax
from jax.experimental import pallas as pl
from jax.experimental.pallas import tpu as pltpu

HEAD_DIM = 64
LANES = 128
CHUNK = 64
LEFT_CHUNKS = 8
MAX_REL = 4 * CHUNK
TOP_K = 2
ROPE_BASE = 10000.0
LN_EPS = 1e-5
NEG_BIG = -1e30
VMEM_LIMIT = 56 * 1024 * 1024

BF16 = jnp.bfloat16
F32 = jnp.float32


def _params(*sem):
    return pltpu.CompilerParams(dimension_semantics=sem, vmem_limit_bytes=VMEM_LIMIT)


def _tile(n, target):
    if n <= target:
        return n
    t = target - target % LANES
    while t >= LANES:
        if n % t == 0:
            return t
        t -= LANES
    return n


def _dot(a, b):
    return jnp.dot(a, b, preferred_element_type=F32)


def _dot_nt(a, b):
    return lax.dot_general(a, b, (((1,), (1,)), ((), ())), preferred_element_type=F32)


def _split_dot(a, b_bf16):
    hi = a.astype(BF16)
    lo = (a - hi.astype(F32)).astype(BF16)
    return _dot(hi, b_bf16) + _dot(lo, b_bf16)


def _layer_norm(r, g, b):
    mu = jnp.mean(r, axis=-1, keepdims=True)
    d = r - mu
    var = jnp.mean(d * d, axis=-1, keepdims=True)
    return d * lax.rsqrt(var + LN_EPS) * g + b


def _proj_kernel(x_ref, w_ref, o_ref, xb_ref):
    @pl.when(pl.program_id(1) == 0)
    def _():
        xb_ref[...] = x_ref[...].astype(BF16)

    o_ref[...] = _dot(xb_ref[...], w_ref[...]).astype(o_ref.dtype)


def _proj(x, w, tm_target=1024, tn_target=1792):
    t, k = x.shape
    n = w.shape[1]
    tm, tn = _tile(t, tm_target), _tile(n, tn_target)
    return pl.pallas_call(
        _proj_kernel,
        out_shape=jax.ShapeDtypeStruct((t, n), BF16),
        grid=(t // tm, n // tn),
        in_specs=[pl.BlockSpec((tm, k), lambda i, j: (i, 0)),
                  pl.BlockSpec((k, tn), lambda i, j: (0, j))],
        out_specs=pl.BlockSpec((tm, tn), lambda i, j: (i, j)),
        scratch_shapes=[pltpu.VMEM((tm, k), BF16)],
        compiler_params=_params("parallel", "arbitrary"),
        name="proj",
    )(x, w)


def _out_ln_kernel(alpha, n_a, *refs):
    a_refs = refs[:n_a]
    w_ref, x_ref, g_ref, b_ref = refs[n_a:n_a + 4]
    o_refs = refs[n_a + 4:]
    ka = a_refs[0].shape[1]
    y = _dot(a_refs[0][...], w_ref[0:ka, :])
    for idx in range(1, n_a):
        y = y + _dot(a_refs[idx][...], w_ref[idx * ka:(idx + 1) * ka, :])
    out = _layer_norm(alpha * x_ref[...] + y, g_ref[...], b_ref[...])
    for o_ref in o_refs:
        o_ref[...] = out.astype(o_ref.dtype)


def _out_ln(a_list, w, x, g, b, alpha, also_bf16=False, tm_target=512):
    t, d = x.shape
    tm = _tile(t, tm_target)
    ka = a_list[0].shape[1]
    row = lambda i: (i, 0)
    fixed = lambda i: (0, 0)
    dtypes = [F32, BF16] if also_bf16 else [F32]
    outs = pl.pallas_call(
        functools.partial(_out_ln_kernel, alpha, len(a_list)),
        out_shape=[jax.ShapeDtypeStruct((t, d), dt) for dt in dtypes],
        grid=(t // tm,),
        in_specs=[pl.BlockSpec((tm, ka), row) for _ in a_list]
        + [pl.BlockSpec(w.shape, fixed), pl.BlockSpec((tm, d), row),
           pl.BlockSpec((1, d), fixed), pl.BlockSpec((1, d), fixed)],
        out_specs=[pl.BlockSpec((tm, d), row) for _ in dtypes],
        compiler_params=_params("parallel"),
        name="out_ln",
    )(*a_list, w, x, g.reshape(1, d), b.reshape(1, d))
    return outs if also_bf16 else outs[0]


def _sb_kernel(q_ref, k_ref, v_ref, o_ref, acc_ref, z_ref):
    tq = q_ref.shape[1]
    qi = pl.program_id(2)
    lane = lax.broadcasted_iota(jnp.int32, (1, LANES), 1)
    head0 = lane < HEAD_DIM
    row = lax.broadcasted_iota(jnp.int32, (tq, tq), 0)
    col = lax.broadcasted_iota(jnp.int32, (tq, tq), 1)
    after = (row > col).astype(BF16)
    strictly_causal = col < row
    q = q_ref[0] * jnp.asarray(HEAD_DIM ** -0.5, BF16)
    q_heads = (jnp.where(head0, q, jnp.zeros_like(q)), jnp.where(head0, jnp.zeros_like(q), q))

    def key_block(ref, kb):
        return ref[0, pl.ds(pl.multiple_of(kb * tq, tq), tq), :]

    def step(kb, carries, diagonal=False):
        log_betas, log_1ms, suffixes = [], [], []
        for h in range(2):
            z = z_ref[h]
            neg_z = -z
            log_1m = jnp.minimum(neg_z, 0.0) - jnp.log(1.0 + jnp.exp(jnp.minimum(z, neg_z)))
            log_betas.append(z + log_1m)
            if diagonal:
                log_1m = jnp.where(strictly_causal, log_1m, 0.0)
            log_1ms.append(log_1m[:, 0:1])
            suffixes.append(_dot(log_1m.astype(BF16), after))
        k_next = key_block(k_ref, jnp.maximum(kb - 1, 0))
        for h in range(2):
            z_ref[h] = _dot_nt(q_heads[h], k_next)
        v = key_block(v_ref, kb)
        new_carries = []
        for h in range(2):
            w = jnp.exp(log_betas[h] + suffixes[h] + carries[h])
            if diagonal:
                w = jnp.where(strictly_causal, w, 0.0)
            pv = _dot(w.astype(BF16), v)
            if diagonal:
                acc_ref[h] = pv
            else:
                acc_ref[h] += pv
            new_carries.append(carries[h] + suffixes[h][:, 0:1] + log_1ms[h])
        return tuple(new_carries)

    k_diag = key_block(k_ref, qi)
    for h in range(2):
        z_ref[h] = _dot_nt(q_heads[h], k_diag)
    zero = jnp.zeros((tq, 1), F32)
    carries = step(qi, (zero, zero), True)
    lax.fori_loop(0, qi, lambda it, carries: step(qi - 1 - it, carries), carries)
    o_ref[0] = jnp.where(head0, acc_ref[0], acc_ref[1]).astype(o_ref.dtype)


def _stick_breaking(h3, n_heads, col0, tq_target=256):
    bsz, s, _ = h3.shape
    tq = _tile(s, tq_target)
    pairs = n_heads // 2
    return pl.pallas_call(
        _sb_kernel,
        out_shape=jax.ShapeDtypeStruct((bsz, s, pairs * LANES), BF16),
        grid=(bsz, pairs, s // tq),
        in_specs=[pl.BlockSpec((1, tq, LANES), lambda b, p, i: (b, i, col0 + p)),
                  pl.BlockSpec((1, s, LANES), lambda b, p, i: (b, 0, col0 + pairs + p)),
                  pl.BlockSpec((1, s, LANES), lambda b, p, i: (b, 0, col0 + 2 * pairs + p))],
        out_specs=pl.BlockSpec((1, tq, LANES), lambda b, p, i: (b, i, p)),
        scratch_shapes=[pltpu.VMEM((2, tq, LANES), F32), pltpu.VMEM((2, tq, tq), F32)],
        compiler_params=_params("parallel", "parallel", "arbitrary"),
        name="stick_breaking",
    )(h3, h3, h3)


def _ret_kernel(q_ref, k_ref, v_ref, gate_ref, cos_ref, sin_ref, din_ref, dq_ref, dk_ref,
                dc_ref, gn_ref, o_ref, state_ref):
    @pl.when(pl.program_id(2) == 0)
    def _():
        state_ref[...] = jnp.zeros_like(state_ref)

    lane = lax.broadcasted_iota(jnp.int32, (1, LANES), 1)
    head0 = lane < HEAD_DIM
    first_half = (lane & (HEAD_DIM // 2)) == 0
    row = lax.broadcasted_iota(jnp.int32, (LANES, LANES), 0) < HEAD_DIM
    col = lax.broadcasted_iota(jnp.int32, (LANES, LANES), 1) < HEAD_DIM
    same_head = row == col
    group_mean = jnp.where(same_head, 1.0 / HEAD_DIM, 0.0).astype(BF16)
    cos = cos_ref[...]
    sin = sin_ref[...]

    def rotary(x):
        half = HEAD_DIM // 2
        swapped = jnp.where(first_half, pltpu.roll(x, LANES - half, 1), pltpu.roll(x, half, 1))
        return x * cos + swapped * sin

    q = rotary(q_ref[0].astype(F32)) * (HEAD_DIM ** -0.5)
    k = rotary(k_ref[0].astype(F32))
    v = v_ref[0]
    qb = q.astype(BF16)
    kb = k.astype(BF16)

    intra = []
    for h in range(2):
        qh = jnp.where(head0 if h == 0 else jnp.logical_not(head0), qb, jnp.zeros_like(qb))
        inner = _dot_nt(qh, kb) * din_ref[h]
        intra.append(_dot(inner.astype(BF16), v))
    state = state_ref[...]
    cross = _dot(qb, state.astype(BF16)) * dq_ref[0]
    o = jnp.where(head0, intra[0], intra[1]) + cross

    kd = (k * dk_ref[0]).T.astype(BF16)
    state_ref[...] = state * dc_ref[0] + jnp.where(same_head, _dot(kd, v), 0.0)

    mu = _split_dot(o, group_mean)
    d = o - mu
    var = _split_dot(d * d, group_mean)
    y = d * lax.rsqrt(var + LN_EPS) * gn_ref[...]
    g = gate_ref[0].astype(F32)
    o_ref[0] = (g / (1.0 + jnp.exp(-g)) * y).astype(o_ref.dtype)


def _retention(h3, gn_g, n_heads, col0, chunk_target=256):
    bsz, s, _ = h3.shape
    c = _tile(s, chunk_target)
    pairs = n_heads // 2
    half = HEAD_DIM // 2

    pos = jnp.arange(s, dtype=F32)
    inv_freq = ROPE_BASE ** (-jnp.arange(half, dtype=F32) / half)
    ang = pos[:, None] * inv_freq[None, :]
    cos = jnp.tile(jnp.cos(ang), (1, LANES // half))
    sin = jnp.tile(jnp.concatenate([-jnp.sin(ang), jnp.sin(ang)], axis=1), (1, LANES // HEAD_DIM))

    gamma = 1.0 - 2.0 ** (-5.0 - jnp.arange(n_heads, dtype=F32))
    log_g = jnp.log(gamma)
    i = jnp.arange(c, dtype=F32)
    diff = i[:, None] - i[None, :]
    d_in = jnp.where(diff >= 0, jnp.exp(log_g[:, None, None] * jnp.maximum(diff, 0.0)), 0.0)
    per_lane = lambda t: jnp.repeat(t.reshape(pairs, 2, -1), HEAD_DIM, axis=1)
    d_q = jnp.swapaxes(per_lane(jnp.exp(log_g[:, None] * (i + 1.0))), 1, 2)
    d_k = jnp.swapaxes(per_lane(jnp.exp(log_g[:, None] * (c - 1.0 - i))), 1, 2)
    d_c = jnp.swapaxes(per_lane(jnp.exp(log_g * c)[:, None]), 1, 2)

    blk = lambda off: pl.BlockSpec((1, c, LANES), lambda b, p, t: (b, t, col0 + off * pairs + p))
    return pl.pallas_call(
        _ret_kernel,
        out_shape=jax.ShapeDtypeStruct((bsz, s, pairs * LANES), BF16),
        grid=(bsz, pairs, s // c),
        in_specs=[blk(0), blk(1), blk(2), blk(3),
                  pl.BlockSpec((c, LANES), lambda b, p, t: (t, 0)),
                  pl.BlockSpec((c, LANES), lambda b, p, t: (t, 0)),
                  pl.BlockSpec((2, c, c), lambda b, p, t: (p, 0, 0)),
                  pl.BlockSpec((1, c, LANES), lambda b, p, t: (p, 0, 0)),
                  pl.BlockSpec((1, c, LANES), lambda b, p, t: (p, 0, 0)),
                  pl.BlockSpec((1, 1, LANES), lambda b, p, t: (p, 0, 0)),
                  pl.BlockSpec((1, LANES), lambda b, p, t: (0, p))],
        out_specs=pl.BlockSpec((1, c, LANES), lambda b, p, t: (b, t, p)),
        scratch_shapes=[pltpu.VMEM((LANES, LANES), F32)],
        compiler_params=_params("parallel", "parallel", "arbitrary"),
        name="retention",
    )(h3, h3, h3, h3, cos, sin, d_in, d_q, d_k, d_c, gn_g.reshape(1, -1))


def _chunk_attn_kernel(q_ref, k_ref, v_ref, bias_ref, o_ref):
    tq = q_ref.shape[1]
    t0 = pl.program_id(2) * tq
    lane = lax.broadcasted_iota(jnp.int32, (1, LANES), 1)
    head0 = lane < HEAD_DIM
    q = q_ref[0] * jnp.asarray(HEAD_DIM ** -0.5, BF16)
    q_heads = (jnp.where(head0, q, jnp.zeros_like(q)), jnp.where(head0, jnp.zeros_like(q), q))

    width = bias_ref.shape[3]
    start = pl.multiple_of(jnp.maximum(t0 - LEFT_CHUNKS * CHUNK, 0), tq)
    k = k_ref[0, pl.ds(start, width), :]
    v = v_ref[0, pl.ds(start, width), :]
    logits = [_dot_nt(qh, k) for qh in q_heads]
    probs, denoms = [], []
    for h in range(2):
        s = logits[h] + bias_ref[0, 0, h * tq:(h + 1) * tq, :]
        p = jnp.exp(s - jnp.max(s, axis=-1, keepdims=True))
        denoms.append(jnp.sum(p, axis=-1, keepdims=True))
        probs.append(p.astype(BF16))
    outs = [_dot(probs[h], v) / denoms[h] for h in range(2)]
    o_ref[0] = jnp.where(head0, outs[0], outs[1]).astype(o_ref.dtype)


def _window_bias(rel_bias, tq, width):
    n_heads = rel_bias.shape[0]
    left = LEFT_CHUNKS * CHUNK
    rel_min, rel_max = left - (width - 1), left + tq - 1
    assert rel_min >= -MAX_REL and rel_max >= MAX_REL
    by_rel = jnp.concatenate(
        [rel_bias[:, rel_min + MAX_REL:], jnp.broadcast_to(rel_bias[:, -1:], (n_heads, rel_max - MAX_REL))], axis=1)
    m = rel_max - rel_min + 1
    rev = jnp.concatenate([by_rel[:, ::-1], jnp.zeros((n_heads, 1), by_rel.dtype)], axis=1)
    rows = jnp.tile(rev, (1, tq))[:, :tq * m].reshape(n_heads, tq, m)
    bias = rows[:, :, tq - 1:tq - 1 + width].astype(F32)
    q_chunk = jnp.arange(tq)[:, None] // CHUNK + LEFT_CHUNKS
    k_chunk = jnp.arange(width)[None, :] // CHUNK
    visible = (k_chunk >= q_chunk - LEFT_CHUNKS) & (k_chunk <= q_chunk)
    return jnp.where(visible[None], bias, NEG_BIG)


def _chunk_attention(qkv3, rel_bias, n_heads, tq=256):
    bsz, s, _ = qkv3.shape
    pairs = n_heads // 2
    n_shift = (LEFT_CHUNKS * CHUNK) // tq
    width = (n_shift + 1) * tq
    assert s >= width
    bias = _window_bias(rel_bias, tq, width)
    bias = jnp.stack([jnp.pad(bias[:, :, v * tq:], ((0, 0), (0, 0), (0, v * tq)), constant_values=NEG_BIG)
                      for v in range(n_shift + 1)], axis=1).reshape(pairs, 2, n_shift + 1, tq, width)
    bias = bias.transpose(0, 2, 1, 3, 4).reshape(pairs, n_shift + 1, 2 * tq, width)

    return pl.pallas_call(
        _chunk_attn_kernel,
        out_shape=jax.ShapeDtypeStruct((bsz, s, pairs * LANES), BF16),
        grid=(bsz, pairs, s // tq),
        in_specs=[pl.BlockSpec((1, tq, LANES), lambda b, p, t: (b, t, p)),
                  pl.BlockSpec((1, s, LANES), lambda b, p, t: (b, 0, pairs + p)),
                  pl.BlockSpec((1, s, LANES), lambda b, p, t: (b, 0, 2 * pairs + p)),
                  pl.BlockSpec((1, 1, 2 * tq, width), lambda b, p, t: (p, jnp.maximum(n_shift - t, 0), 0, 0))],
        out_specs=pl.BlockSpec((1, tq, LANES), lambda b, p, t: (b, t, p)),
        compiler_params=_params("parallel", "parallel", "arbitrary"),
        name="chunk_attention",
    )(qkv3, qkv3, qkv3, bias)


def _swiglu_ln_kernel(alpha, x_ref, wg_ref, wu_ref, wd_ref, g_ref, b_ref, o_ref, xb_ref, acc_ref):
    f = pl.program_id(1)

    @pl.when(f == 0)
    def _():
        xb_ref[...] = x_ref[...].astype(BF16)
        acc_ref[...] = jnp.zeros_like(acc_ref)

    xb = xb_ref[...]
    gate = _dot(xb, wg_ref[...])
    up = _dot(xb, wu_ref[...])
    hidden = (gate / (1.0 + jnp.exp(-gate)) * up).astype(BF16)
    acc_ref[...] += _dot(hidden, wd_ref[...])

    @pl.when(f == pl.num_programs(1) - 1)
    def _():
        o_ref[...] = _layer_norm(alpha * x_ref[...] + acc_ref[...], g_ref[...], b_ref[...])


def _swiglu_ln(x, wg, wu, wd, g, b, alpha, tm_target=512, tf_target=1408):
    t, d = x.shape
    ff = wg.shape[1]
    tm, tf = _tile(t, tm_target), _tile(ff, tf_target)
    return pl.pallas_call(
        functools.partial(_swiglu_ln_kernel, alpha),
        out_shape=jax.ShapeDtypeStruct((t, d), F32),
        grid=(t // tm, ff // tf),
        in_specs=[pl.BlockSpec((tm, d), lambda i, f: (i, 0)),
                  pl.BlockSpec((d, tf), lambda i, f: (0, f)),
                  pl.BlockSpec((d, tf), lambda i, f: (0, f)),
                  pl.BlockSpec((tf, d), lambda i, f: (f, 0)),
                  pl.BlockSpec((1, d), lambda i, f: (0, 0)),
                  pl.BlockSpec((1, d), lambda i, f: (0, 0))],
        out_specs=pl.BlockSpec((tm, d), lambda i, f: (i, 0)),
        scratch_shapes=[pltpu.VMEM((tm, d), BF16), pltpu.VMEM((tm, d), F32)],
        compiler_params=_params("parallel", "arbitrary"),
        name="swiglu_ln",
    )(x, wg, wu, wd, g.reshape(1, d), b.reshape(1, d))


ROUTE_E1, ROUTE_E2, ROUTE_RANK1, ROUTE_RANK2, ROUTE_G1, ROUTE_G2 = range(6)
ROUTE_FIELDS = 8
MOE_TILE = 512
PAIR_LIVE, PAIR_FIRST, PAIR_LAST = 1, 2, 4


def _route_kernel(n_experts, x_ref, w_ref, b_ref, col_ref, row_ref, cnt_ref, run_ref):
    @pl.when(pl.program_id(0) == 0)
    def _():
        run_ref[...] = jnp.zeros_like(run_ref)

    x = x_ref[...]
    w = w_ref[...]
    tm = x.shape[0]
    x_hi = x.astype(BF16)
    x_lo = (x - x_hi.astype(F32)).astype(BF16)
    w_hi = w.astype(BF16)
    w_lo = (w - w_hi.astype(F32)).astype(BF16)
    logits = _dot(x_hi, w_hi) + (_dot(x_hi, w_lo) + _dot(x_lo, w_hi)) + b_ref[...]
    lane = lax.broadcasted_iota(jnp.int32, logits.shape, 1).astype(F32)
    neg_inf = jnp.asarray(-jnp.inf, F32)
    logits = jnp.where(lane < n_experts, logits, neg_inf)
    top1 = jnp.max(logits, axis=-1, keepdims=True)
    idx1 = jnp.min(jnp.where(logits == top1, lane, float(LANES)), axis=-1, keepdims=True)
    rest = jnp.where(lane == idx1, neg_inf, logits)
    top2 = jnp.max(rest, axis=-1, keepdims=True)
    idx2 = jnp.min(jnp.where(rest == top2, lane, float(LANES)), axis=-1, keepdims=True)
    e2 = jnp.exp(top2 - top1)
    denom = 1.0 + e2

    chosen = jnp.where(jnp.logical_or(lane == idx1, lane == idx2), 1.0, 0.0)
    r = lax.broadcasted_iota(jnp.int32, (tm, tm), 0)
    c = lax.broadcasted_iota(jnp.int32, (tm, tm), 1)
    earlier = (c < r).astype(BF16)
    before = _dot(earlier, chosen.astype(BF16)) + run_ref[...]
    rank1 = jnp.sum(jnp.where(lane == idx1, before, 0.0), axis=-1, keepdims=True)
    rank2 = jnp.sum(jnp.where(lane == idx2, before, 0.0), axis=-1, keepdims=True)
    total = jnp.sum(chosen, axis=0, keepdims=True)
    cnt_ref[0] = total
    run_ref[...] += total

    fields = {ROUTE_E1: idx1, ROUTE_E2: idx2, ROUTE_RANK1: rank1, ROUTE_RANK2: rank2,
              ROUTE_G1: 1.0 / denom, ROUTE_G2: e2 / denom}
    meta = jnp.zeros_like(logits)
    for field, value in fields.items():
        meta = jnp.where(lane == field, value, meta)
    col_ref[...] = meta
    row_ref[0] = meta.T[0:ROUTE_FIELDS, :]


def _route(x, w_router, b_router, tm):
    t, d = x.shape
    n_experts = w_router.shape[1]
    nb = t // tm
    w = jnp.pad(w_router.astype(F32), ((0, 0), (0, LANES - n_experts)))
    b = jnp.pad(b_router.astype(F32), (0, LANES - n_experts)).reshape(1, LANES)
    return pl.pallas_call(
        functools.partial(_route_kernel, n_experts),
        out_shape=[jax.ShapeDtypeStruct((t, LANES), F32),
                   jax.ShapeDtypeStruct((nb, ROUTE_FIELDS, tm), F32),
                   jax.ShapeDtypeStruct((nb, 1, LANES), F32)],
        grid=(nb,),
        in_specs=[pl.BlockSpec((tm, d), lambda i: (i, 0)),
                  pl.BlockSpec((d, LANES), lambda i: (0, 0)),
                  pl.BlockSpec((1, LANES), lambda i: (0, 0))],
        out_specs=[pl.BlockSpec((tm, LANES), lambda i: (i, 0)),
                   pl.BlockSpec((1, ROUTE_FIELDS, tm), lambda i: (i, 0, 0)),
                   pl.BlockSpec((1, 1, LANES), lambda i: (i, 0, 0))],
        scratch_shapes=[pltpu.VMEM((1, LANES), F32)],
        compiler_params=_params("arbitrary"),
        name="route",
    )(x, w, b)


def _moe_plan(cnt, n_experts, tile, n_tiles):
    nb = cnt.shape[0]
    counts = cnt[:, 0, :n_experts].astype(jnp.int32)
    cum = jnp.cumsum(counts, axis=0) - counts
    total = jnp.sum(counts, axis=0)
    tiles_e = (total + tile - 1) // tile
    tile_end = jnp.cumsum(tiles_e)
    row_off = (tile_end - tiles_e) * tile
    start = row_off[None, :] + cum
    j0 = start // tile
    j1 = (start + counts - 1) // tile
    has = counts > 0
    tiles = jnp.stack([j0, j0 + 1], axis=-1)
    valid = jnp.stack([has, has & (j1 > j0)], axis=-1)

    n_pairs = nb * n_experts + n_tiles
    slot = jnp.arange(n_pairs)
    block_id = jnp.broadcast_to(jnp.arange(nb)[:, None, None], tiles.shape)
    expert_id = jnp.broadcast_to(jnp.arange(n_experts)[None, :, None], tiles.shape)

    def listed(order, group_key):
        flat = lambda a: a.transpose(order).reshape(-1)
        n_live = jnp.sum(valid)
        idx = jnp.nonzero(flat(valid), size=n_pairs, fill_value=0)[0]
        idx = jnp.where(slot < n_live, idx, idx[n_live - 1])
        fields = {name: flat(a)[idx].astype(jnp.int32)
                  for name, a in (("tile", tiles), ("block", block_id), ("expert", expert_id))}
        key = fields[group_key]
        live = slot < n_live
        first = live & (key != jnp.concatenate([jnp.full((1,), -1, key.dtype), key[:-1]]))
        last = live & ((slot == n_live - 1) | (key != jnp.concatenate([key[1:], key[-1:]])))
        fields["flag"] = (PAIR_LIVE * live + PAIR_FIRST * first + PAIR_LAST * last).astype(jnp.int32)
        return fields

    by_expert = listed((1, 0, 2), "tile")
    by_block = listed((0, 1, 2), "block")
    n_used = tile_end[-1:].astype(jnp.int32)
    tile_expert = jnp.minimum(jnp.searchsorted(tile_end, jnp.arange(n_tiles), side="right"),
                              n_experts - 1).astype(jnp.int32)
    return dict(by_expert=by_expert, by_block=by_block, row_off=row_off.astype(jnp.int32),
                n_used=n_used, tile_expert=tile_expert)


def _expert_field(meta, expert, first_field, absent):
    return jnp.where(meta[ROUTE_E1] == expert, meta[first_field],
                     jnp.where(meta[ROUTE_E2] == expert, meta[first_field + 1], absent))


def _expert_rows(meta, expert, offset):
    rank = _expert_field(meta, expert.astype(F32), ROUTE_RANK1, -1.0)
    return jnp.where(rank >= 0.0, rank + offset.astype(F32), -1.0).astype(jnp.int32)


def _gather_kernel(tile_ref, block_ref, expert_ref, flag_ref, off_ref, row_ref, col_ref, x_ref, xs_ref, gs_ref,
                   acc_ref):
    pair = pl.program_id(0)
    expert = expert_ref[pair]
    flag = flag_ref[pair]
    first = (flag & PAIR_FIRST) != 0
    last = (flag & PAIR_LAST) != 0
    tile, tm = xs_ref.shape[0], x_ref.shape[0]

    @pl.when((flag & PAIR_LIVE) != 0)
    def _():
        by_row = [row_ref[0, f:f + 1, :] for f in range(ROUTE_FIELDS)]
        by_col = [col_ref[:, f:f + 1] for f in range(ROUTE_FIELDS)]
        rows = tile_ref[pair] * tile + lax.broadcasted_iota(jnp.int32, (tile, tm), 0)
        onehot = jnp.where(rows == _expert_rows(by_row, expert, off_ref[expert]), 1.0, 0.0).astype(BF16)
        xg = _dot(onehot, x_ref[...])
        gate = _expert_field(by_col, expert.astype(F32), ROUTE_G1, 0.0)
        gate_hi = gate.astype(BF16).astype(F32)
        lane = lax.broadcasted_iota(jnp.int32, (tm, LANES), 1)
        gate_pair = jnp.where(lane == 0, gate_hi, jnp.where(lane == 1, gate - gate_hi, 0.0))
        gg = _dot(onehot, gate_pair.astype(BF16))

        @pl.when(jnp.logical_and(first, last))
        def _():
            xs_ref[...] = xg.astype(BF16)

        @pl.when(jnp.logical_and(first, jnp.logical_not(last)))
        def _():
            acc_ref[...] = xg

        @pl.when(jnp.logical_and(jnp.logical_not(first), jnp.logical_not(last)))
        def _():
            acc_ref[...] += xg

        @pl.when(jnp.logical_and(jnp.logical_not(first), last))
        def _():
            xs_ref[...] = (acc_ref[...] + xg).astype(BF16)

        @pl.when(first)
        def _():
            gs_ref[...] = gg

        @pl.when(jnp.logical_not(first))
        def _():
            gs_ref[...] += gg


def _moe_gather(plan, meta_row, meta_col, xb, tile, n_tiles):
    t, d = xb.shape
    _, _, tm = meta_row.shape
    pairs = plan["by_expert"]
    return pl.pallas_call(
        _gather_kernel,
        out_shape=[jax.ShapeDtypeStruct((n_tiles * tile, d), BF16),
                   jax.ShapeDtypeStruct((n_tiles * tile, LANES), F32)],
        grid_spec=pltpu.PrefetchScalarGridSpec(
            num_scalar_prefetch=5,
            grid=(pairs["tile"].shape[0],),
            in_specs=[pl.BlockSpec((1, ROUTE_FIELDS, tm), lambda p, tiles, blocks, *_: (blocks[p], 0, 0)),
                      pl.BlockSpec((tm, LANES), lambda p, tiles, blocks, *_: (blocks[p], 0)),
                      pl.BlockSpec((tm, d), lambda p, tiles, blocks, *_: (blocks[p], 0))],
            out_specs=[pl.BlockSpec((tile, d), lambda p, tiles, *_: (tiles[p], 0)),
                       pl.BlockSpec((tile, LANES), lambda p, tiles, *_: (tiles[p], 0))],
            scratch_shapes=[pltpu.VMEM((tile, d), F32)]),
        compiler_params=_params("arbitrary"),
        name="moe_gather",
    )(pairs["tile"], pairs["block"], pairs["expert"], pairs["flag"], plan["row_off"], meta_row, meta_col, xb)


def _moe_ffn_kernel(te_ref, used_ref, xs_ref, gs_ref, wg_ref, wu_ref, wd_ref, o_ref, acc_ref):
    f = pl.program_id(1)

    @pl.when(pl.program_id(0) < used_ref[0])
    def _():
        @pl.when(f == 0)
        def _():
            acc_ref[...] = jnp.zeros_like(acc_ref)

        xs = xs_ref[...]
        gate = _dot(xs, wg_ref[0])
        up = _dot(xs, wu_ref[0])
        hidden = (gate / (1.0 + jnp.exp(-gate)) * up).astype(BF16)
        acc_ref[...] += _dot(hidden, wd_ref[0])

        @pl.when(f == pl.num_programs(1) - 1)
        def _():
            o_ref[...] = (acc_ref[...] * (gs_ref[:, 0:1] + gs_ref[:, 1:2])).astype(o_ref.dtype)


def _moe_ffn(plan, xs, gs, wg, wu, wd, tile, tf_target=1792):
    rows, d = xs.shape
    ff = wg.shape[2]
    tf = _tile(ff, tf_target)
    nf = ff // tf

    def tile_of(i, used):
        return jnp.minimum(i, used[0] - 1)

    def chunk_of(i, f, used):
        return jnp.where(i < used[0], f, nf - 1)

    return pl.pallas_call(
        _moe_ffn_kernel,
        out_shape=jax.ShapeDtypeStruct((rows, d), BF16),
        grid_spec=pltpu.PrefetchScalarGridSpec(
            num_scalar_prefetch=2,
            grid=(rows // tile, nf),
            in_specs=[pl.BlockSpec((tile, d), lambda i, f, te, used: (tile_of(i, used), 0)),
                      pl.BlockSpec((tile, LANES), lambda i, f, te, used: (tile_of(i, used), 0)),
                      pl.BlockSpec((1, d, tf), lambda i, f, te, used: (te[tile_of(i, used)], 0, chunk_of(i, f, used))),
                      pl.BlockSpec((1, d, tf), lambda i, f, te, used: (te[tile_of(i, used)], 0, chunk_of(i, f, used))),
                      pl.BlockSpec((1, tf, d), lambda i, f, te, used: (te[tile_of(i, used)], chunk_of(i, f, used), 0))],
            out_specs=pl.BlockSpec((tile, d), lambda i, f, te, used: (tile_of(i, used), 0)),
            scratch_shapes=[pltpu.VMEM((tile, d), F32)]),
        compiler_params=_params("arbitrary", "arbitrary"),
        name="moe_ffn",
    )(plan["tile_expert"], plan["n_used"], xs, gs, wg, wu, wd)


def _combine_ln_kernel(alpha, tile_ref, block_ref, expert_ref, flag_ref, off_ref, col_ref, os_ref, x_ref, g_ref,
                       b_ref, o_ref, acc_ref):
    pair = pl.program_id(0)
    expert = expert_ref[pair]
    flag = flag_ref[pair]
    tm, tile = x_ref.shape[0], os_ref.shape[0]

    @pl.when((flag & PAIR_LIVE) != 0)
    def _():
        by_col = [col_ref[:, f:f + 1] for f in range(ROUTE_FIELDS)]
        cols = tile_ref[pair] * tile + lax.broadcasted_iota(jnp.int32, (tm, tile), 1)
        onehot = jnp.where(cols == _expert_rows(by_col, expert, off_ref[expert]), 1.0, 0.0).astype(BF16)
        part = _dot(onehot, os_ref[...])

        @pl.when((flag & PAIR_FIRST) != 0)
        def _():
            acc_ref[...] = part

        @pl.when((flag & PAIR_FIRST) == 0)
        def _():
            acc_ref[...] += part

    @pl.when((flag & PAIR_LAST) != 0)
    def _():
        o_ref[...] = _layer_norm(alpha * x_ref[...] + acc_ref[...], g_ref[...], b_ref[...])


def _moe_combine_ln(plan, meta_col, o_sorted, x, g, b, alpha, tile, tm):
    t, d = x.shape
    pairs = plan["by_block"]
    return pl.pallas_call(
        functools.partial(_combine_ln_kernel, alpha),
        out_shape=jax.ShapeDtypeStruct((t, d), F32),
        grid_spec=pltpu.PrefetchScalarGridSpec(
            num_scalar_prefetch=5,
            grid=(pairs["tile"].shape[0],),
            in_specs=[pl.BlockSpec((tm, LANES), lambda p, tiles, blocks, *_: (blocks[p], 0)),
                      pl.BlockSpec((tile, d), lambda p, tiles, *_: (tiles[p], 0)),
                      pl.BlockSpec((tm, d), lambda p, tiles, blocks, *_: (blocks[p], 0)),
                      pl.BlockSpec((1, d), lambda p, *_: (0, 0)),
                      pl.BlockSpec((1, d), lambda p, *_: (0, 0))],
            out_specs=pl.BlockSpec((tm, d), lambda p, tiles, blocks, *_: (blocks[p], 0)),
            scratch_shapes=[pltpu.VMEM((tm, d), F32)]),
        compiler_params=_params("arbitrary"),
        name="moe_combine_ln",
    )(pairs["tile"], pairs["block"], pairs["expert"], pairs["flag"], plan["row_off"], meta_col, o_sorted, x,
      g.reshape(1, d), b.reshape(1, d))


def _moe_ln(x, xb, w_router, b_router, wg, wu, wd, g, b, alpha):
    t, _ = x.shape
    n_experts = wg.shape[0]
    tile = _tile(t, MOE_TILE)
    n_tiles = TOP_K * t // tile + n_experts
    meta_col, meta_row, cnt = _route(x, w_router, b_router, tile)
    plan = _moe_plan(cnt, n_experts, tile, n_tiles)
    xs, gs = _moe_gather(plan, meta_row, meta_col, xb, tile, n_tiles)
    o_sorted = _moe_ffn(plan, xs, gs, wg, wu, wd, tile)
    return _moe_combine_ln(plan, meta_col, o_sorted, x, g, b, alpha, tile, tile)


def _even_layer(x, bsz, s, alpha, ln1_g, ln1_b, w_in, ret_gn_g, w_out, ln2_g, ln2_b, w_gate, w_up, w_down):
    d = x.shape[1]
    n_heads = d // HEAD_DIM
    n_sb = n_heads // 2
    n_ret = n_heads - n_sb
    h = _proj(x, w_in.astype(BF16))
    h3 = h.reshape(bsz, s, h.shape[1])
    o_sb = _stick_breaking(h3, n_sb, 0)
    o_ret = _retention(h3, ret_gn_g, n_ret, 3 * n_sb * HEAD_DIM // LANES)
    x = _out_ln([o_sb.reshape(bsz * s, -1), o_ret.reshape(bsz * s, -1)], w_out.astype(BF16),
                x, ln1_g, ln1_b, alpha)
    return _swiglu_ln(x, w_gate.astype(BF16), w_up.astype(BF16), w_down.astype(BF16), ln2_g, ln2_b, alpha)


def _odd_layer(x, bsz, s, alpha, ln1_g, ln1_b, w_qkv, rel_bias, w_out, ln2_g, ln2_b, w_router, b_router,
               w_gate, w_up, w_down):
    d = x.shape[1]
    qkv = _proj(x, w_qkv.astype(BF16))
    o = _chunk_attention(qkv.reshape(bsz, s, 3 * d), rel_bias, d // HEAD_DIM)
    x, xb = _out_ln([o.reshape(bsz * s, d)], w_out.astype(BF16), x, ln1_g, ln1_b, alpha, also_bf16=True)
    return _moe_ln(x, xb, w_router, b_router, w_gate.astype(BF16), w_up.astype(BF16), w_down.astype(BF16),
                   ln2_g, ln2_b, alpha)


def kernel(x, even_ln1_g, even_ln1_b, even_w_in, even_ret_gn_g, even_w_out, even_ln2_g, even_ln2_b,
           even_w_gate, even_w_up, even_w_down, odd_ln1_g, odd_ln1_b, odd_w_qkv, odd_rel_bias, odd_w_out,
           odd_ln2_g, odd_ln2_b, odd_w_router, odd_b_router, odd_w_gate, odd_w_up, odd_w_down):
    bsz, s, d = x.shape
    depth = even_w_in.shape[0] + odd_w_qkv.shape[0]
    alpha = (2 * depth) ** 0.25
    xt = x.reshape(bsz * s, d)
    for layer in range(depth):
        i = layer // 2
        if layer % 2 == 0:
            xt = _even_layer(xt, bsz, s, alpha, even_ln1_g[i], even_ln1_b[i], even_w_in[i], even_ret_gn_g[i],
                             even_w_out[i], even_ln2_g[i], even_ln2_b[i], even_w_gate[i], even_w_up[i],
                             even_w_down[i])
        else:
            xt = _odd_layer(xt, bsz, s, alpha, odd_ln1_g[i], odd_ln1_b[i], odd_w_qkv[i], odd_rel_bias[i],
                            odd_w_out[i], odd_ln2_g[i], odd_ln2_b[i], odd_w_router[i], odd_b_router[i],
                            odd_w_gate[i], odd_w_up[i], odd_w_down[i])
    return xt.reshape(bsz, s, d)
```

```python
import functools

import jax
import jax.numpy as jnp
from jax import lax
from jax.experimental import pallas as pl
from jax.experimental.pallas import tpu as pltpu

HEAD_DIM = 64
LANES = 128
CHUNK = 64
LEFT_CHUNKS = 8
MAX_REL = 4 * CHUNK
TOP_K = 2
ROPE_BASE = 10000.0
LN_EPS = 1e-5
NEG_BIG = -1e30
VMEM_LIMIT = 56 * 1024 * 1024

BF16 = jnp.bfloat16
F32 = jnp.float32


def _params(*sem):
    return pltpu.CompilerParams(dimension_semantics=sem, vmem_limit_bytes=VMEM_LIMIT)


def _tile(n, target):
    if n <= target:
        return n
    t = target - target % LANES
    while t >= LANES:
        if n % t == 0:
            return t
        t -= LANES
    return n


def _dot(a, b):
    return jnp.dot(a, b, preferred_element_type=F32)


def _dot_nt(a, b):
    return lax.dot_general(a, b, (((1,), (1,)), ((), ())), preferred_element_type=F32)


def _split_dot(a, b_bf16):
    hi = a.astype(BF16)
    lo = (a - hi.astype(F32)).astype(BF16)
    return _dot(hi, b_bf16) + _dot(lo, b_bf16)


def _layer_norm(r, g, b):
    mu = jnp.mean(r, axis=-1, keepdims=True)
    d = r - mu
    var = jnp.mean(d * d, axis=-1, keepdims=True)
    return d * lax.rsqrt(var + LN_EPS) * g + b


def _proj_kernel(x_ref, w_ref, o_ref, xb_ref):
    @pl.when(pl.program_id(1) == 0)
    def _():
        xb_ref[...] = x_ref[...].astype(BF16)

    o_ref[...] = _dot(xb_ref[...], w_ref[...]).astype(o_ref.dtype)


def _proj(x, w, tm_target=1024, tn_target=1792):
    t, k = x.shape
    n = w.shape[1]
    tm, tn = _tile(t, tm_target), _tile(n, tn_target)
    return pl.pallas_call(
        _proj_kernel,
        out_shape=jax.ShapeDtypeStruct((t, n), BF16),
        grid=(t // tm, n // tn),
        in_specs=[pl.BlockSpec((tm, k), lambda i, j: (i, 0)),
                  pl.BlockSpec((k, tn), lambda i, j: (0, j))],
        out_specs=pl.BlockSpec((tm, tn), lambda i, j: (i, j)),
        scratch_shapes=[pltpu.VMEM((tm, k), BF16)],
        compiler_params=_params("parallel", "arbitrary"),
        name="proj",
    )(x, w)


def _out_ln_kernel(alpha, n_a, *refs):
    a_refs = refs[:n_a]
    w_ref, x_ref, g_ref, b_ref = refs[n_a:n_a + 4]
    o_refs = refs[n_a + 4:]
    ka = a_refs[0].shape[1]
    y = _dot(a_refs[0][...], w_ref[0:ka, :])
    for idx in range(1, n_a):
        y = y + _dot(a_refs[idx][...], w_ref[idx * ka:(idx + 1) * ka, :])
    out = _layer_norm(alpha * x_ref[...] + y, g_ref[...], b_ref[...])
    for o_ref in o_refs:
        o_ref[...] = out.astype(o_ref.dtype)


def _out_ln(a_list, w, x, g, b, alpha, also_bf16=False, tm_target=512):
    t, d = x.shape
    tm = _tile(t, tm_target)
    ka = a_list[0].shape[1]
    row = lambda i: (i, 0)
    fixed = lambda i: (0, 0)
    dtypes = [F32, BF16] if also_bf16 else [F32]
    outs = pl.pallas_call(
        functools.partial(_out_ln_kernel, alpha, len(a_list)),
        out_shape=[jax.ShapeDtypeStruct((t, d), dt) for dt in dtypes],
        grid=(t // tm,),
        in_specs=[pl.BlockSpec((tm, ka), row) for _ in a_list]
        + [pl.BlockSpec(w.shape, fixed), pl.BlockSpec((tm, d), row),
           pl.BlockSpec((1, d), fixed), pl.BlockSpec((1, d), fixed)],
        out_specs=[pl.BlockSpec((tm, d), row) for _ in dtypes],
        compiler_params=_params("parallel"),
        name="out_ln",
    )(*a_list, w, x, g.reshape(1, d), b.reshape(1, d))
    return outs if also_bf16 else outs[0]


def _sb_kernel(q_ref, k_ref, v_ref, o_ref, acc_ref, z_ref, w_ref):
    tq = q_ref.shape[1]
    n_heads = 2 * (q_ref.shape[2] // LANES)
    qi = pl.program_id(2)
    lane = lax.broadcasted_iota(jnp.int32, (1, LANES), 1)
    head0 = lane < HEAD_DIM
    row = lax.broadcasted_iota(jnp.int32, (tq, tq), 0)
    col = lax.broadcasted_iota(jnp.int32, (tq, tq), 1)
    after = (row > col).astype(BF16)
    strictly_causal = col < row

    def pair_lanes(h):
        return slice((h // 2) * LANES, (h // 2 + 1) * LANES)

    def masked_q(h):
        q = q_ref[0, :, pair_lanes(h)] * jnp.asarray(HEAD_DIM ** -0.5, BF16)
        return jnp.where(head0, q, jnp.zeros_like(q)) if h % 2 == 0 else jnp.where(head0, jnp.zeros_like(q), q)

    q_heads = [masked_q(h) for h in range(n_heads)]

    def key_block(ref, kb, h):
        return ref[0, pl.ds(pl.multiple_of(kb * tq, tq), tq), pair_lanes(h)]

    def step(kb, carries, diagonal=False):
        log_betas, log_1ms, suffixes = [], [], []
        for h in range(n_heads):
            z = z_ref[h]
            neg_z = -z
            log_1m = jnp.minimum(neg_z, 0.0) - jnp.log(1.0 + jnp.exp(jnp.minimum(z, neg_z)))
            log_betas.append(z + log_1m)
            if diagonal:
                log_1m = jnp.where(strictly_causal, log_1m, 0.0)
            log_1ms.append(log_1m[:, 0:1])
            suffixes.append(_dot(log_1m.astype(BF16), after))
        kb_next = jnp.maximum(kb - 1, 0)
        for h in range(n_heads):
            z_ref[h] = _dot_nt(q_heads[h], key_block(k_ref, kb_next, h))
        new_carries = []
        for h in range(n_heads):
            w = jnp.exp(log_betas[h] + suffixes[h] + carries[h])
            if diagonal:
                w = jnp.where(strictly_causal, w, 0.0)
            w_ref[h] = w.astype(BF16)
            new_carries.append(carries[h] + suffixes[h][:, 0:1] + log_1ms[h])
        return tuple(new_carries)

    def add_values(kb):
        for h in range(n_heads):
            acc_ref[h] += _dot(w_ref[h], key_block(v_ref, kb, h))

    def later_step(it, carries):
        kb = qi - 1 - it
        add_values(kb + 1)
        return step(kb, carries)

    for h in range(n_heads):
        z_ref[h] = _dot_nt(q_heads[h], key_block(k_ref, qi, h))
    acc_ref[...] = jnp.zeros_like(acc_ref)
    zero = jnp.zeros((tq, 1), F32)
    carries = step(qi, (zero,) * n_heads, True)
    lax.fori_loop(0, qi, later_step, carries)
    add_values(0)
    for h in range(0, n_heads, 2):
        o_ref[0, :, pair_lanes(h)] = jnp.where(head0, acc_ref[h], acc_ref[h + 1]).astype(o_ref.dtype)


def _stick_breaking(h3, n_heads, col0, tq_target=256):
    bsz, s, _ = h3.shape
    tq = _tile(s, tq_target)
    pairs = n_heads // 2
    per_step = 2 if pairs % 2 == 0 and col0 % 2 == 0 else 1
    groups, lanes, first = pairs // per_step, per_step * LANES, col0 // per_step
    return pl.pallas_call(
        _sb_kernel,
        out_shape=jax.ShapeDtypeStruct((bsz, s, pairs * LANES), BF16),
        grid=(bsz, groups, s // tq),
        in_specs=[pl.BlockSpec((1, tq, lanes), lambda b, p, i: (b, i, first + p)),
                  pl.BlockSpec((1, s, lanes), lambda b, p, i: (b, 0, first + groups + p)),
                  pl.BlockSpec((1, s, lanes), lambda b, p, i: (b, 0, first + 2 * groups + p))],
        out_specs=pl.BlockSpec((1, tq, lanes), lambda b, p, i: (b, i, p)),
        scratch_shapes=[pltpu.VMEM((2 * per_step, tq, LANES), F32), pltpu.VMEM((2 * per_step, tq, tq), F32),
                        pltpu.VMEM((2 * per_step, tq, tq), BF16)],
        compiler_params=_params("parallel", "parallel", "arbitrary"),
        name="stick_breaking",
    )(h3, h3, h3)


def _ret_kernel(q_ref, k_ref, v_ref, gate_ref, cos_ref, sin_ref, din_ref, dq_ref, dk_ref,
                dc_ref, gn_ref, o_ref, state_ref):
    @pl.when(pl.program_id(2) == 0)
    def _():
        state_ref[...] = jnp.zeros_like(state_ref)

    lane = lax.broadcasted_iota(jnp.int32, (1, LANES), 1)
    head0 = lane < HEAD_DIM
    first_half = (lane & (HEAD_DIM // 2)) == 0
    row = lax.broadcasted_iota(jnp.int32, (LANES, LANES), 0) < HEAD_DIM
    col = lax.broadcasted_iota(jnp.int32, (LANES, LANES), 1) < HEAD_DIM
    same_head = row == col
    group_mean = jnp.where(same_head, 1.0 / HEAD_DIM, 0.0).astype(BF16)
    cos = cos_ref[...]
    sin = sin_ref[...]

    def rotary(x):
        half = HEAD_DIM // 2
        swapped = jnp.where(first_half, pltpu.roll(x, LANES - half, 1), pltpu.roll(x, half, 1))
        return x * cos + swapped * sin

    q = rotary(q_ref[0].astype(F32)) * (HEAD_DIM ** -0.5)
    k = rotary(k_ref[0].astype(F32))
    v = v_ref[0]
    qb = q.astype(BF16)
    kb = k.astype(BF16)

    intra = []
    for h in range(2):
        qh = jnp.where(head0 if h == 0 else jnp.logical_not(head0), qb, jnp.zeros_like(qb))
        inner = _dot_nt(qh, kb) * din_ref[h]
        intra.append(_dot(inner.astype(BF16), v))
    state = state_ref[...]
    cross = _dot(qb, state.astype(BF16)) * dq_ref[0]
    o = jnp.where(head0, intra[0], intra[1]) + cross

    kd = (k * dk_ref[0]).T.astype(BF16)
    state_ref[...] = state * dc_ref[0] + jnp.where(same_head, _dot(kd, v), 0.0)

    mu = _split_dot(o, group_mean)
    d = o - mu
    var = _split_dot(d * d, group_mean)
    y = d * lax.rsqrt(var + LN_EPS) * gn_ref[...]
    g = gate_ref[0].astype(F32)
    o_ref[0] = (g / (1.0 + jnp.exp(-g)) * y).astype(o_ref.dtype)


def _retention(h3, gn_g, n_heads, col0, chunk_target=256):
    bsz, s, _ = h3.shape
    c = _tile(s, chunk_target)
    pairs = n_heads // 2
    half = HEAD_DIM // 2

    pos = jnp.arange(s, dtype=F32)
    inv_freq = ROPE_BASE ** (-jnp.arange(half, dtype=F32) / half)
    ang = pos[:, None] * inv_freq[None, :]
    cos = jnp.tile(jnp.cos(ang), (1, LANES // half))
    sin = jnp.tile(jnp.concatenate([-jnp.sin(ang), jnp.sin(ang)], axis=1), (1, LANES // HEAD_DIM))

    gamma = 1.0 - 2.0 ** (-5.0 - jnp.arange(n_heads, dtype=F32))
    log_g = jnp.log(gamma)
    i = jnp.arange(c, dtype=F32)
    diff = i[:, None] - i[None, :]
    d_in = jnp.where(diff >= 0, jnp.exp(log_g[:, None, None] * jnp.maximum(diff, 0.0)), 0.0)
    per_lane = lambda t: jnp.repeat(t.reshape(pairs, 2, -1), HEAD_DIM, axis=1)
    d_q = jnp.swapaxes(per_lane(jnp.exp(log_g[:, None] * (i + 1.0))), 1, 2)
    d_k = jnp.swapaxes(per_lane(jnp.exp(log_g[:, None] * (c - 1.0 - i))), 1, 2)
    d_c = jnp.swapaxes(per_lane(jnp.exp(log_g * c)[:, None]), 1, 2)

    blk = lambda off: pl.BlockSpec((1, c, LANES), lambda b, p, t: (b, t, col0 + off * pairs + p))
    return pl.pallas_call(
        _ret_kernel,
        out_shape=jax.ShapeDtypeStruct((bsz, s, pairs * LANES), BF16),
        grid=(bsz, pairs, s // c),
        in_specs=[blk(0), blk(1), blk(2), blk(3),
                  pl.BlockSpec((c, LANES), lambda b, p, t: (t, 0)),
                  pl.BlockSpec((c, LANES), lambda b, p, t: (t, 0)),
                  pl.BlockSpec((2, c, c), lambda b, p, t: (p, 0, 0)),
                  pl.BlockSpec((1, c, LANES), lambda b, p, t: (p, 0, 0)),
                  pl.BlockSpec((1, c, LANES), lambda b, p, t: (p, 0, 0)),
                  pl.BlockSpec((1, 1, LANES), lambda b, p, t: (p, 0, 0)),
                  pl.BlockSpec((1, LANES), lambda b, p, t: (0, p))],
        out_specs=pl.BlockSpec((1, c, LANES), lambda b, p, t: (b, t, p)),
        scratch_shapes=[pltpu.VMEM((LANES, LANES), F32)],
        compiler_params=_params("parallel", "parallel", "arbitrary"),
        name="retention",
    )(h3, h3, h3, h3, cos, sin, d_in, d_q, d_k, d_c, gn_g.reshape(1, -1))


def _chunk_attn_kernel(q_ref, k_ref, v_ref, bias_ref, o_ref):
    tq = q_ref.shape[1]
    t0 = pl.program_id(2) * tq
    n_pairs = q_ref.shape[2] // LANES
    lane = lax.broadcasted_iota(jnp.int32, (1, LANES), 1)
    head0 = lane < HEAD_DIM

    width = bias_ref.shape[3]
    start = pl.multiple_of(jnp.maximum(t0 - LEFT_CHUNKS * CHUNK, 0), tq)

    def pair_slice(ref, pair, rows):
        return ref[0, rows, pair * LANES:(pair + 1) * LANES]

    logits = []
    for pair in range(n_pairs):
        q = pair_slice(q_ref, pair, slice(None)) * jnp.asarray(HEAD_DIM ** -0.5, BF16)
        k = pair_slice(k_ref, pair, pl.ds(start, width))
        logits.append(_dot_nt(jnp.where(head0, q, jnp.zeros_like(q)), k))
        logits.append(_dot_nt(jnp.where(head0, jnp.zeros_like(q), q), k))
    for pair in range(n_pairs):
        probs, denoms = [], []
        for h in range(2):
            head = 2 * pair + h
            s = logits[head] + bias_ref[0, 0, head * tq:(head + 1) * tq, :]
            p = jnp.exp(s - jnp.max(s, axis=-1, keepdims=True))
            denoms.append(jnp.sum(p, axis=-1, keepdims=True))
            probs.append(p.astype(BF16))
        v = pair_slice(v_ref, pair, pl.ds(start, width))
        outs = [_dot(probs[h], v) / denoms[h] for h in range(2)]
        o_ref[0, :, pair * LANES:(pair + 1) * LANES] = jnp.where(head0, outs[0], outs[1]).astype(o_ref.dtype)


def _window_bias(rel_bias, tq, width):
    n_heads = rel_bias.shape[0]
    left = LEFT_CHUNKS * CHUNK
    rel_min, rel_max = left - (width - 1), left + tq - 1
    assert rel_min >= -MAX_REL and rel_max >= MAX_REL
    by_rel = jnp.concatenate(
        [rel_bias[:, rel_min + MAX_REL:], jnp.broadcast_to(rel_bias[:, -1:], (n_heads, rel_max - MAX_REL))], axis=1)
    m = rel_max - rel_min + 1
    rev = jnp.concatenate([by_rel[:, ::-1], jnp.zeros((n_heads, 1), by_rel.dtype)], axis=1)
    rows = jnp.tile(rev, (1, tq))[:, :tq * m].reshape(n_heads, tq, m)
    bias = rows[:, :, tq - 1:tq - 1 + width].astype(F32)
    q_chunk = jnp.arange(tq)[:, None] // CHUNK + LEFT_CHUNKS
    k_chunk = jnp.arange(width)[None, :] // CHUNK
    visible = (k_chunk >= q_chunk - LEFT_CHUNKS) & (k_chunk <= q_chunk)
    return jnp.where(visible[None], bias, NEG_BIG)


def _chunk_attention(qkv3, rel_bias, n_heads, tq=256):
    bsz, s, _ = qkv3.shape
    pairs = n_heads // 2
    n_shift = (LEFT_CHUNKS * CHUNK) // tq
    width = (n_shift + 1) * tq
    assert s >= width
    bias = _window_bias(rel_bias, tq, width)
    per_step = 2 if pairs % 2 == 0 else 1
    groups, heads = pairs // per_step, 2 * per_step
    lanes = per_step * LANES
    bias = jnp.stack([jnp.pad(bias[:, :, v * tq:], ((0, 0), (0, 0), (0, v * tq)), constant_values=NEG_BIG)
                      for v in range(n_shift + 1)], axis=1).reshape(groups, heads, n_shift + 1, tq, width)
    bias = bias.transpose(0, 2, 1, 3, 4).reshape(groups, n_shift + 1, heads * tq, width)

    return pl.pallas_call(
        _chunk_attn_kernel,
        out_shape=jax.ShapeDtypeStruct((bsz, s, pairs * LANES), BF16),
        grid=(bsz, groups, s // tq),
        in_specs=[pl.BlockSpec((1, tq, lanes), lambda b, p, t: (b, t, p)),
                  pl.BlockSpec((1, s, lanes), lambda b, p, t: (b, 0, groups + p)),
                  pl.BlockSpec((1, s, lanes), lambda b, p, t: (b, 0, 2 * groups + p)),
                  pl.BlockSpec((1, 1, heads * tq, width), lambda b, p, t: (p, jnp.maximum(n_shift - t, 0), 0, 0))],
        out_specs=pl.BlockSpec((1, tq, lanes), lambda b, p, t: (b, t, p)),
        compiler_params=_params("parallel", "parallel", "arbitrary"),
        name="chunk_attention",
    )(qkv3, qkv3, qkv3, bias)


def _swiglu_ln_kernel(alpha, x_ref, wg_ref, wu_ref, wd_ref, g_ref, b_ref, o_ref, xb_ref, acc_ref):
    f = pl.program_id(1)

    @pl.when(f == 0)
    def _():
        xb_ref[...] = x_ref[...].astype(BF16)
        acc_ref[...] = jnp.zeros_like(acc_ref)

    xb = xb_ref[...]
    gate = _dot(xb, wg_ref[...])
    up = _dot(xb, wu_ref[...])
    hidden = (gate / (1.0 + jnp.exp(-gate)) * up).astype(BF16)
    acc_ref[...] += _dot(hidden, wd_ref[...])

    @pl.when(f == pl.num_programs(1) - 1)
    def _():
        o_ref[...] = _layer_norm(alpha * x_ref[...] + acc_ref[...], g_ref[...], b_ref[...])


def _swiglu_ln(x, wg, wu, wd, g, b, alpha, tm_target=512, tf_target=1408):
    t, d = x.shape
    ff = wg.shape[1]
    tm, tf = _tile(t, tm_target), _tile(ff, tf_target)
    return pl.pallas_call(
        functools.partial(_swiglu_ln_kernel, alpha),
        out_shape=jax.ShapeDtypeStruct((t, d), F32),
        grid=(t // tm, ff // tf),
        in_specs=[pl.BlockSpec((tm, d), lambda i, f: (i, 0)),
                  pl.BlockSpec((d, tf), lambda i, f: (0, f)),
                  pl.BlockSpec((d, tf), lambda i, f: (0, f)),
                  pl.BlockSpec((tf, d), lambda i, f: (f, 0)),
                  pl.BlockSpec((1, d), lambda i, f: (0, 0)),
                  pl.BlockSpec((1, d), lambda i, f: (0, 0))],
        out_specs=pl.BlockSpec((tm, d), lambda i, f: (i, 0)),
        scratch_shapes=[pltpu.VMEM((tm, d), BF16), pltpu.VMEM((tm, d), F32)],
        compiler_params=_params("parallel", "arbitrary"),
        name="swiglu_ln",
    )(x, wg, wu, wd, g.reshape(1, d), b.reshape(1, d))


ROUTE_E1, ROUTE_E2, ROUTE_RANK1, ROUTE_RANK2, ROUTE_G1, ROUTE_G2 = range(6)
ROUTE_FIELDS = 8
MOE_TILE = 512
MOE_WINDOW = 256
ROW_ALIGN = 16
PAIR_LIVE, PAIR_FIRST, PAIR_LAST, PAIR_NARROW = 1, 2, 4, 8


def _route_kernel(n_experts, x_ref, w_ref, b_ref, col_ref, row_ref, cnt_ref, run_ref):
    @pl.when(pl.program_id(0) == 0)
    def _():
        run_ref[...] = jnp.zeros_like(run_ref)

    x = x_ref[...]
    w = w_ref[...]
    tm = x.shape[0]
    x_hi = x.astype(BF16)
    x_lo = (x - x_hi.astype(F32)).astype(BF16)
    w_hi = w.astype(BF16)
    w_lo = (w - w_hi.astype(F32)).astype(BF16)
    logits = _dot(x_hi, w_hi) + (_dot(x_hi, w_lo) + _dot(x_lo, w_hi)) + b_ref[...]
    lane = lax.broadcasted_iota(jnp.int32, logits.shape, 1).astype(F32)
    neg_inf = jnp.asarray(-jnp.inf, F32)
    logits = jnp.where(lane < n_experts, logits, neg_inf)
    top1 = jnp.max(logits, axis=-1, keepdims=True)
    idx1 = jnp.min(jnp.where(logits == top1, lane, float(LANES)), axis=-1, keepdims=True)
    rest = jnp.where(lane == idx1, neg_inf, logits)
    top2 = jnp.max(rest, axis=-1, keepdims=True)
    idx2 = jnp.min(jnp.where(rest == top2, lane, float(LANES)), axis=-1, keepdims=True)
    e2 = jnp.exp(top2 - top1)
    denom = 1.0 + e2

    chosen = jnp.where(jnp.logical_or(lane == idx1, lane == idx2), 1.0, 0.0)
    r = lax.broadcasted_iota(jnp.int32, (tm, tm), 0)
    c = lax.broadcasted_iota(jnp.int32, (tm, tm), 1)
    earlier = (c < r).astype(BF16)
    before = _dot(earlier, chosen.astype(BF16)) + run_ref[...]
    rank1 = jnp.sum(jnp.where(lane == idx1, before, 0.0), axis=-1, keepdims=True)
    rank2 = jnp.sum(jnp.where(lane == idx2, before, 0.0), axis=-1, keepdims=True)
    total = jnp.sum(chosen, axis=0, keepdims=True)
    cnt_ref[0] = total
    run_ref[...] += total

    fields = {ROUTE_E1: idx1, ROUTE_E2: idx2, ROUTE_RANK1: rank1, ROUTE_RANK2: rank2,
              ROUTE_G1: 1.0 / denom, ROUTE_G2: e2 / denom}
    meta = jnp.zeros_like(logits)
    for field, value in fields.items():
        meta = jnp.where(lane == field, value, meta)
    col_ref[...] = meta
    row_ref[0] = meta.T[0:ROUTE_FIELDS, :]


def _route(x, w_router, b_router, tm):
    t, d = x.shape
    n_experts = w_router.shape[1]
    nb = t // tm
    w = jnp.pad(w_router.astype(F32), ((0, 0), (0, LANES - n_experts)))
    b = jnp.pad(b_router.astype(F32), (0, LANES - n_experts)).reshape(1, LANES)
    return pl.pallas_call(
        functools.partial(_route_kernel, n_experts),
        out_shape=[jax.ShapeDtypeStruct((t, LANES), F32),
                   jax.ShapeDtypeStruct((nb, ROUTE_FIELDS, tm), F32),
                   jax.ShapeDtypeStruct((nb, 1, LANES), F32)],
        grid=(nb,),
        in_specs=[pl.BlockSpec((tm, d), lambda i: (i, 0)),
                  pl.BlockSpec((d, LANES), lambda i: (0, 0)),
                  pl.BlockSpec((1, LANES), lambda i: (0, 0))],
        out_specs=[pl.BlockSpec((tm, LANES), lambda i: (i, 0)),
                   pl.BlockSpec((1, ROUTE_FIELDS, tm), lambda i: (i, 0, 0)),
                   pl.BlockSpec((1, 1, LANES), lambda i: (i, 0, 0))],
        scratch_shapes=[pltpu.VMEM((1, LANES), F32)],
        compiler_params=_params("arbitrary"),
        name="route",
    )(x, w, b)


def _moe_plan(cnt, n_experts, tile, n_tiles):
    nb = cnt.shape[0]
    counts = cnt[:, 0, :n_experts].astype(jnp.int32)
    cum = jnp.cumsum(counts, axis=0) - counts
    total = jnp.sum(counts, axis=0)
    tiles_e = (total + tile - 1) // tile
    tile_end = jnp.cumsum(tiles_e)
    row_off = (tile_end - tiles_e) * tile
    start = row_off[None, :] + cum
    j0 = start // tile
    j1 = (start + counts - 1) // tile
    has = counts > 0
    tiles = jnp.stack([j0, j0 + 1], axis=-1)
    valid = jnp.stack([has, has & (j1 > j0)], axis=-1)
    lo = jnp.clip(start[..., None] - tiles * tile, 0, tile)
    hi = jnp.clip((start + counts)[..., None] - tiles * tile, 0, tile)
    win = min(MOE_WINDOW, tile)
    window = jnp.minimum(lo // ROW_ALIGN * ROW_ALIGN, tile - win)
    narrow = hi <= window + win

    n_pairs = nb * n_experts + n_tiles
    slot = jnp.arange(n_pairs)
    block_id = jnp.broadcast_to(jnp.arange(nb)[:, None, None], tiles.shape)
    expert_id = jnp.broadcast_to(jnp.arange(n_experts)[None, :, None], tiles.shape)

    def listed(order, group_key):
        flat = lambda a: a.transpose(order).reshape(-1)
        n_live = jnp.sum(valid)
        idx = jnp.nonzero(flat(valid), size=n_pairs, fill_value=0)[0]
        idx = jnp.where(slot < n_live, idx, idx[n_live - 1])
        fields = {name: flat(a)[idx].astype(jnp.int32)
                  for name, a in (("tile", tiles), ("block", block_id), ("expert", expert_id),
                                  ("window", window), ("narrow", narrow))}
        key = fields[group_key]
        live = slot < n_live
        first = live & (key != jnp.concatenate([jnp.full((1,), -1, key.dtype), key[:-1]]))
        last = live & ((slot == n_live - 1) | (key != jnp.concatenate([key[1:], key[-1:]])))
        fields["flag"] = (PAIR_LIVE * live + PAIR_FIRST * first + PAIR_LAST * last
                          + PAIR_NARROW * fields.pop("narrow")).astype(jnp.int32)
        return fields

    by_expert = listed((1, 0, 2), "tile")
    by_block = listed((0, 1, 2), "block")
    n_used = tile_end[-1:].astype(jnp.int32)
    tile_expert = jnp.minimum(jnp.searchsorted(tile_end, jnp.arange(n_tiles), side="right"),
                              n_experts - 1).astype(jnp.int32)
    return dict(by_expert=by_expert, by_block=by_block, row_off=row_off.astype(jnp.int32),
                n_used=n_used, tile_expert=tile_expert)


def _expert_field(meta, expert, first_field, absent):
    return jnp.where(meta[ROUTE_E1] == expert, meta[first_field],
                     jnp.where(meta[ROUTE_E2] == expert, meta[first_field + 1], absent))


def _expert_rows(meta, expert, offset):
    rank = _expert_field(meta, expert.astype(F32), ROUTE_RANK1, -1.0)
    return jnp.where(rank >= 0.0, rank + offset.astype(F32), -1.0).astype(jnp.int32)


def _gather_kernel(tile_ref, block_ref, expert_ref, window_ref, flag_ref, off_ref, row_ref, col_ref, x_ref, xs_ref,
                   gs_ref, acc_ref):
    pair = pl.program_id(0)
    expert = expert_ref[pair]
    flag = flag_ref[pair]
    narrow = (flag & PAIR_NARROW) != 0
    tile, tm = xs_ref.shape[0], x_ref.shape[0]

    @pl.when((flag & PAIR_LIVE) != 0)
    def _():
        @pl.when((flag & PAIR_FIRST) != 0)
        def _():
            acc_ref[...] = jnp.zeros_like(acc_ref)
            gs_ref[...] = jnp.zeros_like(gs_ref)

        by_row = [row_ref[0, f:f + 1, :] for f in range(ROUTE_FIELDS)]
        by_col = [col_ref[:, f:f + 1] for f in range(ROUTE_FIELDS)]
        sorted_row = _expert_rows(by_row, expert, off_ref[expert])
        gate = _expert_field(by_col, expert.astype(F32), ROUTE_G1, 0.0)
        gate_hi = gate.astype(BF16).astype(F32)
        lane = lax.broadcasted_iota(jnp.int32, (tm, LANES), 1)
        gate_pair = jnp.where(lane == 0, gate_hi, jnp.where(lane == 1, gate - gate_hi, 0.0)).astype(BF16)

        def place(row0, n_rows):
            rows = tile_ref[pair] * tile + row0 + lax.broadcasted_iota(jnp.int32, (n_rows, tm), 0)
            onehot = jnp.where(rows == sorted_row, 1.0, 0.0).astype(BF16)
            acc_ref[pl.ds(row0, n_rows), :] += _dot(onehot, x_ref[...])
            gs_ref[pl.ds(row0, n_rows), :] += _dot(onehot, gate_pair)

        @pl.when(narrow)
        def _():
            place(pl.multiple_of(window_ref[pair], ROW_ALIGN), min(MOE_WINDOW, tile))

        @pl.when(jnp.logical_not(narrow))
        def _():
            place(0, tile)

        @pl.when((flag & PAIR_LAST) != 0)
        def _():
            xs_ref[...] = acc_ref[...].astype(BF16)


def _moe_gather(plan, meta_row, meta_col, xb, tile, n_tiles):
    t, d = xb.shape
    _, _, tm = meta_row.shape
    pairs = plan["by_expert"]
    return pl.pallas_call(
        _gather_kernel,
        out_shape=[jax.ShapeDtypeStruct((n_tiles * tile, d), BF16),
                   jax.ShapeDtypeStruct((n_tiles * tile, LANES), F32)],
        grid_spec=pltpu.PrefetchScalarGridSpec(
            num_scalar_prefetch=6,
            grid=(pairs["tile"].shape[0],),
            in_specs=[pl.BlockSpec((1, ROUTE_FIELDS, tm), lambda p, tiles, blocks, *_: (blocks[p], 0, 0)),
                      pl.BlockSpec((tm, LANES), lambda p, tiles, blocks, *_: (blocks[p], 0)),
                      pl.BlockSpec((tm, d), lambda p, tiles, blocks, *_: (blocks[p], 0))],
            out_specs=[pl.BlockSpec((tile, d), lambda p, tiles, *_: (tiles[p], 0)),
                       pl.BlockSpec((tile, LANES), lambda p, tiles, *_: (tiles[p], 0))],
            scratch_shapes=[pltpu.VMEM((tile, d), F32)]),
        compiler_params=_params("arbitrary"),
        name="moe_gather",
    )(pairs["tile"], pairs["block"], pairs["expert"], pairs["window"], pairs["flag"], plan["row_off"],
      meta_row, meta_col, xb)


def _moe_ffn_kernel(te_ref, used_ref, xs_ref, gs_ref, wg_ref, wu_ref, wd_ref, o_ref, acc_ref):
    f = pl.program_id(1)

    @pl.when(pl.program_id(0) < used_ref[0])
    def _():
        @pl.when(f == 0)
        def _():
            acc_ref[...] = jnp.zeros_like(acc_ref)

        xs = xs_ref[...]
        gate = _dot(xs, wg_ref[0])
        up = _dot(xs, wu_ref[0])
        hidden = (gate / (1.0 + jnp.exp(-gate)) * up).astype(BF16)
        acc_ref[...] += _dot(hidden, wd_ref[0])

        @pl.when(f == pl.num_programs(1) - 1)
        def _():
            o_ref[...] = (acc_ref[...] * (gs_ref[:, 0:1] + gs_ref[:, 1:2])).astype(o_ref.dtype)


def _moe_ffn(plan, xs, gs, wg, wu, wd, tile, tf_target=1792):
    rows, d = xs.shape
    ff = wg.shape[2]
    tf = _tile(ff, tf_target)
    nf = ff // tf

    def tile_of(i, used):
        return jnp.minimum(i, used[0] - 1)

    def chunk_of(i, f, used):
        return jnp.where(i < used[0], f, nf - 1)

    return pl.pallas_call(
        _moe_ffn_kernel,
        out_shape=jax.ShapeDtypeStruct((rows, d), BF16),
        grid_spec=pltpu.PrefetchScalarGridSpec(
            num_scalar_prefetch=2,
            grid=(rows // tile, nf),
            in_specs=[pl.BlockSpec((tile, d), lambda i, f, te, used: (tile_of(i, used), 0)),
                      pl.BlockSpec((tile, LANES), lambda i, f, te, used: (tile_of(i, used), 0)),
                      pl.BlockSpec((1, d, tf), lambda i, f, te, used: (te[tile_of(i, used)], 0, chunk_of(i, f, used))),
                      pl.BlockSpec((1, d, tf), lambda i, f, te, used: (te[tile_of(i, used)], 0, chunk_of(i, f, used))),
                      pl.BlockSpec((1, tf, d), lambda i, f, te, used: (te[tile_of(i, used)], chunk_of(i, f, used), 0))],
            out_specs=pl.BlockSpec((tile, d), lambda i, f, te, used: (tile_of(i, used), 0)),
            scratch_shapes=[pltpu.VMEM((tile, d), F32)]),
        compiler_params=_params("arbitrary", "arbitrary"),
        name="moe_ffn",
    )(plan["tile_expert"], plan["n_used"], xs, gs, wg, wu, wd)


def _combine_ln_kernel(alpha, tile_ref, block_ref, expert_ref, window_ref, flag_ref, off_ref, col_ref, os_ref, x_ref,
                       g_ref, b_ref, o_ref, acc_ref):
    pair = pl.program_id(0)
    expert = expert_ref[pair]
    flag = flag_ref[pair]
    narrow = (flag & PAIR_NARROW) != 0
    tm, tile = x_ref.shape[0], os_ref.shape[0]

    @pl.when((flag & PAIR_LIVE) != 0)
    def _():
        @pl.when((flag & PAIR_FIRST) != 0)
        def _():
            acc_ref[...] = jnp.zeros_like(acc_ref)

        by_col = [col_ref[:, f:f + 1] for f in range(ROUTE_FIELDS)]
        sorted_row = _expert_rows(by_col, expert, off_ref[expert])

        def collect(row0, n_rows):
            cols = tile_ref[pair] * tile + row0 + lax.broadcasted_iota(jnp.int32, (tm, n_rows), 1)
            onehot = jnp.where(cols == sorted_row, 1.0, 0.0).astype(BF16)
            acc_ref[...] += _dot(onehot, os_ref[pl.ds(row0, n_rows), :])

        @pl.when(narrow)
        def _():
            collect(pl.multiple_of(window_ref[pair], ROW_ALIGN), min(MOE_WINDOW, tile))

        @pl.when(jnp.logical_not(narrow))
        def _():
            collect(0, tile)

    @pl.when((flag & PAIR_LAST) != 0)
    def _():
        o_ref[...] = _layer_norm(alpha * x_ref[...] + acc_ref[...], g_ref[...], b_ref[...])


def _moe_combine_ln(plan, meta_col, o_sorted, x, g, b, alpha, tile, tm):
    t, d = x.shape
    pairs = plan["by_block"]
    return pl.pallas_call(
        functools.partial(_combine_ln_kernel, alpha),
        out_shape=jax.ShapeDtypeStruct((t, d), F32),
        grid_spec=pltpu.PrefetchScalarGridSpec(
            num_scalar_prefetch=6,
            grid=(pairs["tile"].shape[0],),
            in_specs=[pl.BlockSpec((tm, LANES), lambda p, tiles, blocks, *_: (blocks[p], 0)),
                      pl.BlockSpec((tile, d), lambda p, tiles, *_: (tiles[p], 0)),
                      pl.BlockSpec((tm, d), lambda p, tiles, blocks, *_: (blocks[p], 0)),
                      pl.BlockSpec((1, d), lambda p, *_: (0, 0)),
                      pl.BlockSpec((1, d), lambda p, *_: (0, 0))],
            out_specs=pl.BlockSpec((tm, d), lambda p, tiles, blocks, *_: (blocks[p], 0)),
            scratch_shapes=[pltpu.VMEM((tm, d), F32)]),
        compiler_params=_params("arbitrary"),
        name="moe_combine_ln",
    )(pairs["tile"], pairs["block"], pairs["expert"], pairs["window"], pairs["flag"], plan["row_off"],
      meta_col, o_sorted, x, g.reshape(1, d), b.reshape(1, d))


def _moe_ln(x, xb, w_router, b_router, wg, wu, wd, g, b, alpha):
    t, _ = x.shape
    n_experts = wg.shape[0]
    tile = _tile(t, MOE_TILE)
    n_tiles = TOP_K * t // tile + n_experts
    meta_col, meta_row, cnt = _route(x, w_router, b_router, tile)
    plan = _moe_plan(cnt, n_experts, tile, n_tiles)
    xs, gs = _moe_gather(plan, meta_row, meta_col, xb, tile, n_tiles)
    o_sorted = _moe_ffn(plan, xs, gs, wg, wu, wd, tile)
    return _moe_combine_ln(plan, meta_col, o_sorted, x, g, b, alpha, tile, tile)


def _even_layer(x, bsz, s, alpha, ln1_g, ln1_b, w_in, ret_gn_g, w_out, ln2_g, ln2_b, w_gate, w_up, w_down):
    d = x.shape[1]
    n_heads = d // HEAD_DIM
    n_sb = n_heads // 2
    n_ret = n_heads - n_sb
    h = _proj(x, w_in.astype(BF16))
    h3 = h.reshape(bsz, s, h.shape[1])
    o_sb = _stick_breaking(h3, n_sb, 0)
    o_ret = _retention(h3, ret_gn_g, n_ret, 3 * n_sb * HEAD_DIM // LANES)
    x = _out_ln([o_sb.reshape(bsz * s, -1), o_ret.reshape(bsz * s, -1)], w_out.astype(BF16),
                x, ln1_g, ln1_b, alpha)
    return _swiglu_ln(x, w_gate.astype(BF16), w_up.astype(BF16), w_down.astype(BF16), ln2_g, ln2_b, alpha)


def _odd_layer(x, bsz, s, alpha, ln1_g, ln1_b, w_qkv, rel_bias, w_out, ln2_g, ln2_b, w_router, b_router,
               w_gate, w_up, w_down):
    d = x.shape[1]
    qkv = _proj(x, w_qkv.astype(BF16))
    o = _chunk_attention(qkv.reshape(bsz, s, 3 * d), rel_bias, d // HEAD_DIM)
    x, xb = _out_ln([o.reshape(bsz * s, d)], w_out.astype(BF16), x, ln1_g, ln1_b, alpha, also_bf16=True)
    return _moe_ln(x, xb, w_router, b_router, w_gate.astype(BF16), w_up.astype(BF16), w_down.astype(BF16),
                   ln2_g, ln2_b, alpha)


def kernel(x, even_ln1_g, even_ln1_b, even_w_in, even_ret_gn_g, even_w_out, even_ln2_g, even_ln2_b,
           even_w_gate, even_w_up, even_w_down, odd_ln1_g, odd_ln1_b, odd_w_qkv, odd_rel_bias, odd_w_out,
           odd_ln2_g, odd_ln2_b, odd_w_router, odd_b_router, odd_w_gate, odd_w_up, odd_w_down):
    bsz, s, d = x.shape
    depth = even_w_in.shape[0] + odd_w_qkv.shape[0]
    alpha = (2 * depth) ** 0.25
    xt = x.reshape(bsz * s, d)
    for layer in range(depth):
        i = layer // 2
        if layer % 2 == 0:
            xt = _even_layer(xt, bsz, s, alpha, even_ln1_g[i], even_ln1_b[i], even_w_in[i], even_ret_gn_g[i],
                             even_w_out[i], even_ln2_g[i], even_ln2_b[i], even_w_gate[i], even_w_up[i],
                             even_w_down[i])
        else:
            xt = _odd_layer(xt, bsz, s, alpha, odd_ln1_g[i], odd_ln1_b[i], odd_w_qkv[i], odd_rel_bias[i],
                            odd_w_out[i], odd_ln2_g[i], odd_ln2_b[i], odd_w_router[i], odd_b_router[i],
                            odd_w_gate[i], odd_w_up[i], odd_w_down[i])
    return xt.reshape(bsz, s, d)
```

```python
import functools

import jax
import jax.numpy as jnp
from jax import lax
from jax.experimental import pallas as pl
from jax.experimental.pallas import tpu as pltpu

HEAD_DIM = 64
LANES = 128
CHUNK = 64
LEFT_CHUNKS = 8
MAX_REL = 4 * CHUNK
TOP_K = 2
ROPE_BASE = 10000.0
LN_EPS = 1e-5
NEG_BIG = -1e30
VMEM_LIMIT = 56 * 1024 * 1024

BF16 = jnp.bfloat16
F32 = jnp.float32


def _params(*sem):
    return pltpu.CompilerParams(dimension_semantics=sem, vmem_limit_bytes=VMEM_LIMIT)


def _tile(n, target):
    if n <= target:
        return n
    t = target - target % LANES
    while t >= LANES:
        if n % t == 0:
            return t
        t -= LANES
    return n


def _dot(a, b):
    return jnp.dot(a, b, preferred_element_type=F32)


def _dot_nt(a, b):
    return lax.dot_general(a, b, (((1,), (1,)), ((), ())), preferred_element_type=F32)


def _split_dot(a, b_bf16):
    hi = a.astype(BF16)
    lo = (a - hi.astype(F32)).astype(BF16)
    return _dot(hi, b_bf16) + _dot(lo, b_bf16)


def _layer_norm(r, g, b):
    mu = jnp.mean(r, axis=-1, keepdims=True)
    d = r - mu
    var = jnp.mean(d * d, axis=-1, keepdims=True)
    return d * lax.rsqrt(var + LN_EPS) * g + b


def _proj_kernel(x_ref, w_ref, o_ref, xb_ref):
    @pl.when(pl.program_id(1) == 0)
    def _():
        xb_ref[...] = x_ref[...].astype(BF16)

    o_ref[...] = _dot(xb_ref[...], w_ref[...]).astype(o_ref.dtype)


def _proj(x, w, tm_target=1024, tn_target=1792):
    t, k = x.shape
    n = w.shape[1]
    tm, tn = _tile(t, tm_target), _tile(n, tn_target)
    return pl.pallas_call(
        _proj_kernel,
        out_shape=jax.ShapeDtypeStruct((t, n), BF16),
        grid=(t // tm, n // tn),
        in_specs=[pl.BlockSpec((tm, k), lambda i, j: (i, 0)),
                  pl.BlockSpec((k, tn), lambda i, j: (0, j))],
        out_specs=pl.BlockSpec((tm, tn), lambda i, j: (i, j)),
        scratch_shapes=[pltpu.VMEM((tm, k), BF16)],
        compiler_params=_params("parallel", "arbitrary"),
        name="proj",
    )(x, w)


def _out_ln_kernel(alpha, n_a, *refs):
    a_refs = refs[:n_a]
    w_ref, x_ref, g_ref, b_ref = refs[n_a:n_a + 4]
    o_refs = refs[n_a + 4:]
    ka = a_refs[0].shape[1]
    y = _dot(a_refs[0][...], w_ref[0:ka, :])
    for idx in range(1, n_a):
        y = y + _dot(a_refs[idx][...], w_ref[idx * ka:(idx + 1) * ka, :])
    out = _layer_norm(alpha * x_ref[...] + y, g_ref[...], b_ref[...])
    for o_ref in o_refs:
        o_ref[...] = out.astype(o_ref.dtype)


def _out_ln(a_list, w, x, g, b, alpha, also_bf16=False, tm_target=512):
    t, d = x.shape
    tm = _tile(t, tm_target)
    ka = a_list[0].shape[1]
    row = lambda i: (i, 0)
    fixed = lambda i: (0, 0)
    dtypes = [F32, BF16] if also_bf16 else [F32]
    outs = pl.pallas_call(
        functools.partial(_out_ln_kernel, alpha, len(a_list)),
        out_shape=[jax.ShapeDtypeStruct((t, d), dt) for dt in dtypes],
        grid=(t // tm,),
        in_specs=[pl.BlockSpec((tm, ka), row) for _ in a_list]
        + [pl.BlockSpec(w.shape, fixed), pl.BlockSpec((tm, d), row),
           pl.BlockSpec((1, d), fixed), pl.BlockSpec((1, d), fixed)],
        out_specs=[pl.BlockSpec((tm, d), row) for _ in dtypes],
        compiler_params=_params("parallel"),
        name="out_ln",
    )(*a_list, w, x, g.reshape(1, d), b.reshape(1, d))
    return outs if also_bf16 else outs[0]


def _sb_kernel(q_ref, k_ref, v_ref, o_ref, acc_ref, z_ref, w_ref):
    tq = q_ref.shape[1]
    n_heads = 2 * (q_ref.shape[2] // LANES)
    qi = pl.program_id(2)
    lane = lax.broadcasted_iota(jnp.int32, (1, LANES), 1)
    head0 = lane < HEAD_DIM
    row = lax.broadcasted_iota(jnp.int32, (tq, tq), 0)
    col = lax.broadcasted_iota(jnp.int32, (tq, tq), 1)
    after = (row > col).astype(BF16)
    strictly_causal = col < row

    def pair_lanes(h):
        return slice((h // 2) * LANES, (h // 2 + 1) * LANES)

    def masked_q(h):
        q = q_ref[0, :, pair_lanes(h)] * jnp.asarray(HEAD_DIM ** -0.5, BF16)
        return jnp.where(head0, q, jnp.zeros_like(q)) if h % 2 == 0 else jnp.where(head0, jnp.zeros_like(q), q)

    q_heads = [masked_q(h) for h in range(n_heads)]

    def key_block(ref, kb, h):
        return ref[0, pl.ds(pl.multiple_of(kb * tq, tq), tq), pair_lanes(h)]

    def step(kb, carries, diagonal=False):
        log_betas, log_1ms, suffixes = [], [], []
        for h in range(n_heads):
            z = z_ref[h]
            neg_z = -z
            log_1m = jnp.minimum(neg_z, 0.0) - jnp.log(1.0 + jnp.exp(jnp.minimum(z, neg_z)))
            log_betas.append(z + log_1m)
            if diagonal:
                log_1m = jnp.where(strictly_causal, log_1m, 0.0)
            log_1ms.append(log_1m[:, 0:1])
            suffixes.append(_dot(log_1m.astype(BF16), after))
        kb_next = jnp.maximum(kb - 1, 0)
        for h in range(n_heads):
            z_ref[h] = _dot_nt(q_heads[h], key_block(k_ref, kb_next, h))
        new_carries = []
        for h in range(n_heads):
            w = jnp.exp(log_betas[h] + suffixes[h] + carries[h])
            if diagonal:
                w = jnp.where(strictly_causal, w, 0.0)
            w_ref[h] = w.astype(BF16)
            new_carries.append(carries[h] + suffixes[h][:, 0:1] + log_1ms[h])
        return tuple(new_carries)

    def add_values(kb):
        for h in range(n_heads):
            acc_ref[h] += _dot(w_ref[h], key_block(v_ref, kb, h))

    def later_step(it, carries):
        kb = qi - 1 - it
        add_values(kb + 1)
        return step(kb, carries)

    for h in range(n_heads):
        z_ref[h] = _dot_nt(q_heads[h], key_block(k_ref, qi, h))
    acc_ref[...] = jnp.zeros_like(acc_ref)
    zero = jnp.zeros((tq, 1), F32)
    carries = step(qi, (zero,) * n_heads, True)
    lax.fori_loop(0, qi, later_step, carries)
    add_values(0)
    for h in range(0, n_heads, 2):
        o_ref[0, :, pair_lanes(h)] = jnp.where(head0, acc_ref[h], acc_ref[h + 1]).astype(o_ref.dtype)


def _stick_breaking(h3, n_heads, col0, tq_target=256):
    bsz, s, _ = h3.shape
    tq = _tile(s, tq_target)
    pairs = n_heads // 2
    per_step = 2 if pairs % 2 == 0 and col0 % 2 == 0 else 1
    groups, lanes, first = pairs // per_step, per_step * LANES, col0 // per_step
    return pl.pallas_call(
        _sb_kernel,
        out_shape=jax.ShapeDtypeStruct((bsz, s, pairs * LANES), BF16),
        grid=(bsz, groups, s // tq),
        in_specs=[pl.BlockSpec((1, tq, lanes), lambda b, p, i: (b, i, first + p)),
                  pl.BlockSpec((1, s, lanes), lambda b, p, i: (b, 0, first + groups + p)),
                  pl.BlockSpec((1, s, lanes), lambda b, p, i: (b, 0, first + 2 * groups + p))],
        out_specs=pl.BlockSpec((1, tq, lanes), lambda b, p, i: (b, i, p)),
        scratch_shapes=[pltpu.VMEM((2 * per_step, tq, LANES), F32), pltpu.VMEM((2 * per_step, tq, tq), F32),
                        pltpu.VMEM((2 * per_step, tq, tq), BF16)],
        compiler_params=_params("parallel", "parallel", "arbitrary"),
        name="stick_breaking",
    )(h3, h3, h3)


def _ret_kernel(q_ref, k_ref, v_ref, gate_ref, cos_ref, sin_ref, din_ref, dq_ref, dk_ref,
                dc_ref, gn_ref, o_ref, state_ref):
    @pl.when(pl.program_id(2) == 0)
    def _():
        state_ref[...] = jnp.zeros_like(state_ref)

    lane = lax.broadcasted_iota(jnp.int32, (1, LANES), 1)
    head0 = lane < HEAD_DIM
    first_half = (lane & (HEAD_DIM // 2)) == 0
    row = lax.broadcasted_iota(jnp.int32, (LANES, LANES), 0) < HEAD_DIM
    col = lax.broadcasted_iota(jnp.int32, (LANES, LANES), 1) < HEAD_DIM
    same_head = row == col
    group_mean = jnp.where(same_head, 1.0 / HEAD_DIM, 0.0).astype(BF16)
    cos = cos_ref[...]
    sin = sin_ref[...]

    def rotary(x):
        half = HEAD_DIM // 2
        swapped = jnp.where(first_half, pltpu.roll(x, LANES - half, 1), pltpu.roll(x, half, 1))
        return x * cos + swapped * sin

    pairs = range(q_ref.shape[2] // LANES)
    lanes = lambda p: slice(p * LANES, (p + 1) * LANES)
    qb, kb, v, scores, cross, outer = [], [], [], [], [], []
    for p in pairs:
        q = rotary(q_ref[0, :, lanes(p)].astype(F32)) * (HEAD_DIM ** -0.5)
        k = rotary(k_ref[0, :, lanes(p)].astype(F32))
        v.append(v_ref[0, :, lanes(p)])
        qb.append(q.astype(BF16))
        kb.append(k.astype(BF16))
        scores.append([_dot_nt(jnp.where(mask, qb[p], jnp.zeros_like(qb[p])), kb[p])
                       for mask in (head0, jnp.logical_not(head0))])
        cross.append(_dot(qb[p], state_ref[p].astype(BF16)))
        outer.append(_dot((k * dk_ref[p]).T.astype(BF16), v[p]))
    intra = [[_dot((scores[p][h] * din_ref[2 * p + h]).astype(BF16), v[p]) for h in range(2)] for p in pairs]
    for p in pairs:
        state_ref[p] = state_ref[p] * dc_ref[p] + jnp.where(same_head, outer[p], 0.0)
        o = jnp.where(head0, intra[p][0], intra[p][1]) + cross[p] * dq_ref[p]
        mu = _split_dot(o, group_mean)
        d = o - mu
        var = _split_dot(d * d, group_mean)
        y = d * lax.rsqrt(var + LN_EPS) * gn_ref[:, lanes(p)]
        g = gate_ref[0, :, lanes(p)].astype(F32)
        o_ref[0, :, lanes(p)] = (g / (1.0 + jnp.exp(-g)) * y).astype(o_ref.dtype)


def _retention(h3, gn_g, n_heads, col0, chunk_target=256):
    bsz, s, _ = h3.shape
    c = _tile(s, chunk_target)
    pairs = n_heads // 2
    half = HEAD_DIM // 2

    pos = jnp.arange(s, dtype=F32)
    inv_freq = ROPE_BASE ** (-jnp.arange(half, dtype=F32) / half)
    ang = pos[:, None] * inv_freq[None, :]
    cos = jnp.tile(jnp.cos(ang), (1, LANES // half))
    sin = jnp.tile(jnp.concatenate([-jnp.sin(ang), jnp.sin(ang)], axis=1), (1, LANES // HEAD_DIM))

    gamma = 1.0 - 2.0 ** (-5.0 - jnp.arange(n_heads, dtype=F32))
    log_g = jnp.log(gamma)
    i = jnp.arange(c, dtype=F32)
    diff = i[:, None] - i[None, :]
    d_in = jnp.where(diff >= 0, jnp.exp(log_g[:, None, None] * jnp.maximum(diff, 0.0)), 0.0)
    per_lane = lambda t: jnp.repeat(t.reshape(pairs, 2, -1), HEAD_DIM, axis=1)
    d_q = jnp.swapaxes(per_lane(jnp.exp(log_g[:, None] * (i + 1.0))), 1, 2)
    d_k = jnp.swapaxes(per_lane(jnp.exp(log_g[:, None] * (c - 1.0 - i))), 1, 2)
    d_c = jnp.swapaxes(per_lane(jnp.exp(log_g * c)[:, None]), 1, 2)

    per_step = 2 if pairs % 2 == 0 and col0 % 2 == 0 else 1
    groups, lanes, first = pairs // per_step, per_step * LANES, col0 // per_step
    blk = lambda off: pl.BlockSpec((1, c, lanes), lambda b, p, t: (b, t, first + off * groups + p))
    return pl.pallas_call(
        _ret_kernel,
        out_shape=jax.ShapeDtypeStruct((bsz, s, pairs * LANES), BF16),
        grid=(bsz, groups, s // c),
        in_specs=[blk(0), blk(1), blk(2), blk(3),
                  pl.BlockSpec((c, LANES), lambda b, p, t: (t, 0)),
                  pl.BlockSpec((c, LANES), lambda b, p, t: (t, 0)),
                  pl.BlockSpec((2 * per_step, c, c), lambda b, p, t: (p, 0, 0)),
                  pl.BlockSpec((per_step, c, LANES), lambda b, p, t: (p, 0, 0)),
                  pl.BlockSpec((per_step, c, LANES), lambda b, p, t: (p, 0, 0)),
                  pl.BlockSpec((per_step, 1, LANES), lambda b, p, t: (p, 0, 0)),
                  pl.BlockSpec((1, lanes), lambda b, p, t: (0, p))],
        out_specs=pl.BlockSpec((1, c, lanes), lambda b, p, t: (b, t, p)),
        scratch_shapes=[pltpu.VMEM((per_step, LANES, LANES), F32)],
        compiler_params=_params("parallel", "parallel", "arbitrary"),
        name="retention",
    )(h3, h3, h3, h3, cos, sin, d_in, d_q, d_k, d_c, gn_g.reshape(1, -1))


def _chunk_attn_kernel(q_ref, k_ref, v_ref, bias_ref, o_ref):
    tq = q_ref.shape[1]
    t0 = pl.program_id(2) * tq
    n_pairs = q_ref.shape[2] // LANES
    lane = lax.broadcasted_iota(jnp.int32, (1, LANES), 1)
    head0 = lane < HEAD_DIM

    width = bias_ref.shape[3]
    start = pl.multiple_of(jnp.maximum(t0 - LEFT_CHUNKS * CHUNK, 0), tq)

    def pair_slice(ref, pair, rows):
        return ref[0, rows, pair * LANES:(pair + 1) * LANES]

    logits = []
    for pair in range(n_pairs):
        q = pair_slice(q_ref, pair, slice(None)) * jnp.asarray(HEAD_DIM ** -0.5, BF16)
        k = pair_slice(k_ref, pair, pl.ds(start, width))
        logits.append(_dot_nt(jnp.where(head0, q, jnp.zeros_like(q)), k))
        logits.append(_dot_nt(jnp.where(head0, jnp.zeros_like(q), q), k))
    for pair in range(n_pairs):
        probs, denoms = [], []
        for h in range(2):
            head = 2 * pair + h
            s = logits[head] + bias_ref[0, 0, head * tq:(head + 1) * tq, :]
            p = jnp.exp(s - jnp.max(s, axis=-1, keepdims=True))
            denoms.append(jnp.sum(p, axis=-1, keepdims=True))
            probs.append(p.astype(BF16))
        v = pair_slice(v_ref, pair, pl.ds(start, width))
        outs = [_dot(probs[h], v) / denoms[h] for h in range(2)]
        o_ref[0, :, pair * LANES:(pair + 1) * LANES] = jnp.where(head0, outs[0], outs[1]).astype(o_ref.dtype)


def _window_bias(rel_bias, tq, width):
    n_heads = rel_bias.shape[0]
    left = LEFT_CHUNKS * CHUNK
    rel_min, rel_max = left - (width - 1), left + tq - 1
    assert rel_min >= -MAX_REL and rel_max >= MAX_REL
    by_rel = jnp.concatenate(
        [rel_bias[:, rel_min + MAX_REL:], jnp.broadcast_to(rel_bias[:, -1:], (n_heads, rel_max - MAX_REL))], axis=1)
    m = rel_max - rel_min + 1
    rev = jnp.concatenate([by_rel[:, ::-1], jnp.zeros((n_heads, 1), by_rel.dtype)], axis=1)
    rows = jnp.tile(rev, (1, tq))[:, :tq * m].reshape(n_heads, tq, m)
    bias = rows[:, :, tq - 1:tq - 1 + width].astype(F32)
    q_chunk = jnp.arange(tq)[:, None] // CHUNK + LEFT_CHUNKS
    k_chunk = jnp.arange(width)[None, :] // CHUNK
    visible = (k_chunk >= q_chunk - LEFT_CHUNKS) & (k_chunk <= q_chunk)
    return jnp.where(visible[None], bias, NEG_BIG)


def _chunk_attention(qkv3, rel_bias, n_heads, tq=256):
    bsz, s, _ = qkv3.shape
    pairs = n_heads // 2
    n_shift = (LEFT_CHUNKS * CHUNK) // tq
    width = (n_shift + 1) * tq
    assert s >= width
    bias = _window_bias(rel_bias, tq, width)
    per_step = 2 if pairs % 2 == 0 else 1
    groups, heads = pairs // per_step, 2 * per_step
    lanes = per_step * LANES
    bias = jnp.stack([jnp.pad(bias[:, :, v * tq:], ((0, 0), (0, 0), (0, v * tq)), constant_values=NEG_BIG)
                      for v in range(n_shift + 1)], axis=1).reshape(groups, heads, n_shift + 1, tq, width)
    bias = bias.transpose(0, 2, 1, 3, 4).reshape(groups, n_shift + 1, heads * tq, width)

    return pl.pallas_call(
        _chunk_attn_kernel,
        out_shape=jax.ShapeDtypeStruct((bsz, s, pairs * LANES), BF16),
        grid=(bsz, groups, s // tq),
        in_specs=[pl.BlockSpec((1, tq, lanes), lambda b, p, t: (b, t, p)),
                  pl.BlockSpec((1, s, lanes), lambda b, p, t: (b, 0, groups + p)),
                  pl.BlockSpec((1, s, lanes), lambda b, p, t: (b, 0, 2 * groups + p)),
                  pl.BlockSpec((1, 1, heads * tq, width), lambda b, p, t: (p, jnp.maximum(n_shift - t, 0), 0, 0))],
        out_specs=pl.BlockSpec((1, tq, lanes), lambda b, p, t: (b, t, p)),
        compiler_params=_params("parallel", "parallel", "arbitrary"),
        name="chunk_attention",
    )(qkv3, qkv3, qkv3, bias)


def _swiglu_ln_kernel(alpha, x_ref, wg_ref, wu_ref, wd_ref, g_ref, b_ref, o_ref, xb_ref, acc_ref):
    f = pl.program_id(1)

    @pl.when(f == 0)
    def _():
        xb_ref[...] = x_ref[...].astype(BF16)
        acc_ref[...] = jnp.zeros_like(acc_ref)

    xb = xb_ref[...]
    gate = _dot(xb, wg_ref[...])
    up = _dot(xb, wu_ref[...])
    hidden = (gate / (1.0 + jnp.exp(-gate)) * up).astype(BF16)
    acc_ref[...] += _dot(hidden, wd_ref[...])

    @pl.when(f == pl.num_programs(1) - 1)
    def _():
        o_ref[...] = _layer_norm(alpha * x_ref[...] + acc_ref[...], g_ref[...], b_ref[...])


def _swiglu_ln(x, wg, wu, wd, g, b, alpha, tm_target=512, tf_target=2816):
    t, d = x.shape
    ff = wg.shape[1]
    tm, tf = _tile(t, tm_target), _tile(ff, tf_target)
    once = dict(pipeline_mode=pl.Buffered(1)) if tf == ff else {}
    return pl.pallas_call(
        functools.partial(_swiglu_ln_kernel, alpha),
        out_shape=jax.ShapeDtypeStruct((t, d), F32),
        grid=(t // tm, ff // tf),
        in_specs=[pl.BlockSpec((tm, d), lambda i, f: (i, 0)),
                  pl.BlockSpec((d, tf), lambda i, f: (0, f), **once),
                  pl.BlockSpec((d, tf), lambda i, f: (0, f), **once),
                  pl.BlockSpec((tf, d), lambda i, f: (f, 0), **once),
                  pl.BlockSpec((1, d), lambda i, f: (0, 0)),
                  pl.BlockSpec((1, d), lambda i, f: (0, 0))],
        out_specs=pl.BlockSpec((tm, d), lambda i, f: (i, 0)),
        scratch_shapes=[pltpu.VMEM((tm, d), BF16), pltpu.VMEM((tm, d), F32)],
        compiler_params=_params("parallel", "arbitrary"),
        name="swiglu_ln",
    )(x, wg, wu, wd, g.reshape(1, d), b.reshape(1, d))


ROUTE_E1, ROUTE_E2, ROUTE_RANK1, ROUTE_RANK2 = range(4)
ROUTE_FIELDS = 8
MOE_TILE = 512
MOE_WINDOW = 256
ROW_ALIGN = 16
PAIR_LIVE, PAIR_FIRST, PAIR_LAST, PAIR_NARROW = 1, 2, 4, 8


def _route_kernel(n_experts, x_ref, w_ref, b_ref, col_ref, row_ref, gate_ref, cnt_ref, run_ref):
    @pl.when(pl.program_id(0) == 0)
    def _():
        run_ref[...] = jnp.zeros_like(run_ref)

    x = x_ref[...]
    w = w_ref[...]
    tm = x.shape[0]
    x_hi = x.astype(BF16)
    x_lo = (x - x_hi.astype(F32)).astype(BF16)
    w_hi = w.astype(BF16)
    w_lo = (w - w_hi.astype(F32)).astype(BF16)
    logits = _dot(x_hi, w_hi) + (_dot(x_hi, w_lo) + _dot(x_lo, w_hi)) + b_ref[...]
    lane = lax.broadcasted_iota(jnp.int32, logits.shape, 1).astype(F32)
    neg_inf = jnp.asarray(-jnp.inf, F32)
    logits = jnp.where(lane < n_experts, logits, neg_inf)
    top1 = jnp.max(logits, axis=-1, keepdims=True)
    idx1 = jnp.min(jnp.where(logits == top1, lane, float(LANES)), axis=-1, keepdims=True)
    rest = jnp.where(lane == idx1, neg_inf, logits)
    top2 = jnp.max(rest, axis=-1, keepdims=True)
    idx2 = jnp.min(jnp.where(rest == top2, lane, float(LANES)), axis=-1, keepdims=True)
    e2 = jnp.exp(top2 - top1)
    denom = 1.0 + e2

    chosen = jnp.where(jnp.logical_or(lane == idx1, lane == idx2), 1.0, 0.0)
    r = lax.broadcasted_iota(jnp.int32, (tm, tm), 0)
    c = lax.broadcasted_iota(jnp.int32, (tm, tm), 1)
    earlier = (c < r).astype(BF16)
    before = _dot(earlier, chosen.astype(BF16)) + run_ref[...]
    rank1 = jnp.sum(jnp.where(lane == idx1, before, 0.0), axis=-1, keepdims=True)
    rank2 = jnp.sum(jnp.where(lane == idx2, before, 0.0), axis=-1, keepdims=True)
    total = jnp.sum(chosen, axis=0, keepdims=True)
    cnt_ref[0] = total
    run_ref[...] += total

    fields = {ROUTE_E1: idx1, ROUTE_E2: idx2, ROUTE_RANK1: rank1, ROUTE_RANK2: rank2}
    meta = jnp.zeros_like(logits)
    for field, value in fields.items():
        meta = jnp.where(lane == field, value, meta)
    col_ref[...] = meta
    row_ref[0] = meta.T[0:ROUTE_FIELDS, :]
    gates = jnp.where(lane == idx1, 1.0 / denom, jnp.where(lane == idx2, e2 / denom, 0.0))
    gates_hi = gates.astype(BF16)
    gate_ref[:, 0:LANES] = gates_hi
    gate_ref[:, LANES:2 * LANES] = (gates - gates_hi.astype(F32)).astype(BF16)


def _route(x, w_router, b_router, tm):
    t, d = x.shape
    n_experts = w_router.shape[1]
    nb = t // tm
    w = jnp.pad(w_router.astype(F32), ((0, 0), (0, LANES - n_experts)))
    b = jnp.pad(b_router.astype(F32), (0, LANES - n_experts)).reshape(1, LANES)
    return pl.pallas_call(
        functools.partial(_route_kernel, n_experts),
        out_shape=[jax.ShapeDtypeStruct((t, LANES), F32),
                   jax.ShapeDtypeStruct((nb, ROUTE_FIELDS, tm), F32),
                   jax.ShapeDtypeStruct((t, 2 * LANES), BF16),
                   jax.ShapeDtypeStruct((nb, 1, LANES), F32)],
        grid=(nb,),
        in_specs=[pl.BlockSpec((tm, d), lambda i: (i, 0)),
                  pl.BlockSpec((d, LANES), lambda i: (0, 0)),
                  pl.BlockSpec((1, LANES), lambda i: (0, 0))],
        out_specs=[pl.BlockSpec((tm, LANES), lambda i: (i, 0)),
                   pl.BlockSpec((1, ROUTE_FIELDS, tm), lambda i: (i, 0, 0)),
                   pl.BlockSpec((tm, 2 * LANES), lambda i: (i, 0)),
                   pl.BlockSpec((1, 1, LANES), lambda i: (i, 0, 0))],
        scratch_shapes=[pltpu.VMEM((1, LANES), F32)],
        compiler_params=_params("arbitrary"),
        name="route",
    )(x, w, b)


def _moe_plan(cnt, n_experts, tile, n_tiles):
    nb = cnt.shape[0]
    counts = cnt[:, 0, :n_experts].astype(jnp.int32)
    cum = jnp.cumsum(counts, axis=0) - counts
    total = jnp.sum(counts, axis=0)
    tiles_e = (total + tile - 1) // tile
    tile_end = jnp.cumsum(tiles_e)
    row_off = (tile_end - tiles_e) * tile
    start = row_off[None, :] + cum
    j0 = start // tile
    j1 = (start + counts - 1) // tile
    has = counts > 0
    tiles = jnp.stack([j0, j0 + 1], axis=-1)
    valid = jnp.stack([has, has & (j1 > j0)], axis=-1)
    lo = jnp.clip(start[..., None] - tiles * tile, 0, tile)
    hi = jnp.clip((start + counts)[..., None] - tiles * tile, 0, tile)
    win = min(MOE_WINDOW, tile)
    window = jnp.minimum(lo // ROW_ALIGN * ROW_ALIGN, tile - win)
    narrow = hi <= window + win

    n_pairs = nb * n_experts + n_tiles
    slot = jnp.arange(n_pairs)
    block_id = jnp.broadcast_to(jnp.arange(nb)[:, None, None], tiles.shape)
    expert_id = jnp.broadcast_to(jnp.arange(n_experts)[None, :, None], tiles.shape)

    def listed(order, group_key):
        flat = lambda a: a.transpose(order).reshape(-1)
        n_live = jnp.sum(valid)
        idx = jnp.nonzero(flat(valid), size=n_pairs, fill_value=0)[0]
        idx = jnp.where(slot < n_live, idx, idx[n_live - 1])
        fields = {name: flat(a)[idx].astype(jnp.int32)
                  for name, a in (("tile", tiles), ("block", block_id), ("expert", expert_id),
                                  ("window", window), ("narrow", narrow))}
        key = fields[group_key]
        live = slot < n_live
        first = live & (key != jnp.concatenate([jnp.full((1,), -1, key.dtype), key[:-1]]))
        last = live & ((slot == n_live - 1) | (key != jnp.concatenate([key[1:], key[-1:]])))
        fields["flag"] = (PAIR_LIVE * live + PAIR_FIRST * first + PAIR_LAST * last
                          + PAIR_NARROW * fields.pop("narrow")).astype(jnp.int32)
        return fields

    by_expert = listed((1, 0, 2), "tile")
    by_block = listed((0, 1, 2), "block")
    n_used = tile_end[-1:].astype(jnp.int32)
    tile_expert = jnp.minimum(jnp.searchsorted(tile_end, jnp.arange(n_tiles), side="right"),
                              n_experts - 1).astype(jnp.int32)
    return dict(by_expert=by_expert, by_block=by_block, row_off=row_off.astype(jnp.int32),
                n_used=n_used, tile_expert=tile_expert)


def _with_positions(meta, row_off, field_axis):
    def take(first):
        return lax.slice_in_dim(meta, first, first + TOP_K, axis=field_axis)

    expert, offset = take(ROUTE_E1), jnp.zeros_like(take(ROUTE_E1))
    for e in range(row_off.shape[0]):
        offset = jnp.where(expert == float(e), row_off[e].astype(F32), offset)
    n_fields = meta.shape[field_axis]
    return jnp.concatenate([take(ROUTE_E1), take(ROUTE_RANK1) + offset,
                            lax.slice_in_dim(meta, ROUTE_RANK1 + TOP_K, n_fields, axis=field_axis)], axis=field_axis)


def _gather_kernel(tile_ref, block_ref, window_ref, flag_ref, row_ref, gate_ref, x_ref, xs_ref, gs_ref, acc_ref):
    pair = pl.program_id(0)
    flag = flag_ref[pair]
    narrow = (flag & PAIR_NARROW) != 0
    tile, tm = xs_ref.shape[0], x_ref.shape[0]

    @pl.when((flag & PAIR_LIVE) != 0)
    def _():
        @pl.when((flag & PAIR_FIRST) != 0)
        def _():
            acc_ref[...] = jnp.zeros_like(acc_ref)
            gs_ref[...] = jnp.zeros_like(gs_ref)

        pos1 = row_ref[0, ROUTE_RANK1:ROUTE_RANK1 + 1, :].astype(jnp.int32)
        pos2 = row_ref[0, ROUTE_RANK2:ROUTE_RANK2 + 1, :].astype(jnp.int32)

        def place(row0, n_rows):
            rows = tile_ref[pair] * tile + row0 + lax.broadcasted_iota(jnp.int32, (n_rows, tm), 0)
            onehot = jnp.where(jnp.logical_or(rows == pos1, rows == pos2), 1.0, 0.0).astype(BF16)
            acc_ref[pl.ds(row0, n_rows), :] += _dot(onehot, x_ref[...])
            gs_ref[pl.ds(row0, n_rows), :] += _dot(onehot, gate_ref[...])

        @pl.when(narrow)
        def _():
            place(pl.multiple_of(window_ref[pair], ROW_ALIGN), min(MOE_WINDOW, tile))

        @pl.when(jnp.logical_not(narrow))
        def _():
            place(0, tile)

        @pl.when((flag & PAIR_LAST) != 0)
        def _():
            xs_ref[...] = acc_ref[...].astype(BF16)


def _moe_gather(plan, pos_row, gates, xb, tile, n_tiles):
    t, d = xb.shape
    _, _, tm = pos_row.shape
    pairs = plan["by_expert"]
    return pl.pallas_call(
        _gather_kernel,
        out_shape=[jax.ShapeDtypeStruct((n_tiles * tile, d), BF16),
                   jax.ShapeDtypeStruct((n_tiles * tile, 2 * LANES), F32)],
        grid_spec=pltpu.PrefetchScalarGridSpec(
            num_scalar_prefetch=4,
            grid=(pairs["tile"].shape[0],),
            in_specs=[pl.BlockSpec((1, ROUTE_FIELDS, tm), lambda p, tiles, blocks, *_: (blocks[p], 0, 0)),
                      pl.BlockSpec((tm, 2 * LANES), lambda p, tiles, blocks, *_: (blocks[p], 0)),
                      pl.BlockSpec((tm, d), lambda p, tiles, blocks, *_: (blocks[p], 0))],
            out_specs=[pl.BlockSpec((tile, d), lambda p, tiles, *_: (tiles[p], 0)),
                       pl.BlockSpec((tile, 2 * LANES), lambda p, tiles, *_: (tiles[p], 0))],
            scratch_shapes=[pltpu.VMEM((tile, d), F32)]),
        compiler_params=_params("arbitrary"),
        name="moe_gather",
    )(pairs["tile"], pairs["block"], pairs["window"], pairs["flag"], pos_row, gates, xb)


def _moe_ffn_kernel(te_ref, used_ref, xs_ref, gs_ref, wg_ref, wu_ref, wd_ref, o_ref, acc_ref):
    f = pl.program_id(1)
    tile_id = pl.program_id(0)

    @pl.when(tile_id < used_ref[0])
    def _():
        @pl.when(f == 0)
        def _():
            acc_ref[...] = jnp.zeros_like(acc_ref)

        xs = xs_ref[...]
        gate = _dot(xs, wg_ref[0])
        up = _dot(xs, wu_ref[0])
        hidden = (gate / (1.0 + jnp.exp(-gate)) * up).astype(BF16)
        acc_ref[...] += _dot(hidden, wd_ref[0])

        @pl.when(f == pl.num_programs(1) - 1)
        def _():
            by_expert = gs_ref[:, 0:LANES] + gs_ref[:, LANES:2 * LANES]
            lane = lax.broadcasted_iota(jnp.int32, by_expert.shape, 1)
            row_gate = jnp.sum(jnp.where(lane == te_ref[tile_id], by_expert, 0.0), axis=1, keepdims=True)
            o_ref[...] = (acc_ref[...] * row_gate).astype(o_ref.dtype)


def _moe_ffn(plan, xs, gs, wg, wu, wd, tile, tf_target=1792):
    rows, d = xs.shape
    ff = wg.shape[2]
    tf = _tile(ff, tf_target)
    nf = ff // tf

    def tile_of(i, used):
        return jnp.minimum(i, used[0] - 1)

    def chunk_of(i, f, used):
        return jnp.where(i < used[0], f, nf - 1)

    return pl.pallas_call(
        _moe_ffn_kernel,
        out_shape=jax.ShapeDtypeStruct((rows, d), BF16),
        grid_spec=pltpu.PrefetchScalarGridSpec(
            num_scalar_prefetch=2,
            grid=(rows // tile, nf),
            in_specs=[pl.BlockSpec((tile, d), lambda i, f, te, used: (tile_of(i, used), 0)),
                      pl.BlockSpec((tile, 2 * LANES), lambda i, f, te, used: (tile_of(i, used), 0)),
                      pl.BlockSpec((1, d, tf), lambda i, f, te, used: (te[tile_of(i, used)], 0, chunk_of(i, f, used))),
                      pl.BlockSpec((1, d, tf), lambda i, f, te, used: (te[tile_of(i, used)], 0, chunk_of(i, f, used))),
                      pl.BlockSpec((1, tf, d), lambda i, f, te, used: (te[tile_of(i, used)], chunk_of(i, f, used), 0))],
            out_specs=pl.BlockSpec((tile, d), lambda i, f, te, used: (tile_of(i, used), 0)),
            scratch_shapes=[pltpu.VMEM((tile, d), F32)]),
        compiler_params=_params("arbitrary", "arbitrary"),
        name="moe_ffn",
    )(plan["tile_expert"], plan["n_used"], xs, gs, wg, wu, wd)


def _combine_ln_kernel(alpha, tile_ref, block_ref, window_ref, flag_ref, col_ref, os_ref, x_ref, g_ref, b_ref, o_ref,
                       acc_ref):
    pair = pl.program_id(0)
    flag = flag_ref[pair]
    narrow = (flag & PAIR_NARROW) != 0
    tm, tile = x_ref.shape[0], os_ref.shape[0]

    @pl.when((flag & PAIR_LIVE) != 0)
    def _():
        @pl.when((flag & PAIR_FIRST) != 0)
        def _():
            acc_ref[...] = jnp.zeros_like(acc_ref)

        pos1 = col_ref[:, ROUTE_RANK1:ROUTE_RANK1 + 1].astype(jnp.int32)
        pos2 = col_ref[:, ROUTE_RANK2:ROUTE_RANK2 + 1].astype(jnp.int32)

        def collect(row0, n_rows):
            cols = tile_ref[pair] * tile + row0 + lax.broadcasted_iota(jnp.int32, (tm, n_rows), 1)
            onehot = jnp.where(jnp.logical_or(cols == pos1, cols == pos2), 1.0, 0.0).astype(BF16)
            acc_ref[...] += _dot(onehot, os_ref[pl.ds(row0, n_rows), :])

        @pl.when(narrow)
        def _():
            collect(pl.multiple_of(window_ref[pair], ROW_ALIGN), min(MOE_WINDOW, tile))

        @pl.when(jnp.logical_not(narrow))
        def _():
            collect(0, tile)

    @pl.when((flag & PAIR_LAST) != 0)
    def _():
        o_ref[...] = _layer_norm(alpha * x_ref[...] + acc_ref[...], g_ref[...], b_ref[...])


def _moe_combine_ln(plan, pos_col, o_sorted, x, g, b, alpha, tile, tm):
    t, d = x.shape
    pairs = plan["by_block"]
    return pl.pallas_call(
        functools.partial(_combine_ln_kernel, alpha),
        out_shape=jax.ShapeDtypeStruct((t, d), F32),
        grid_spec=pltpu.PrefetchScalarGridSpec(
            num_scalar_prefetch=4,
            grid=(pairs["tile"].shape[0],),
            in_specs=[pl.BlockSpec((tm, LANES), lambda p, tiles, blocks, *_: (blocks[p], 0)),
                      pl.BlockSpec((tile, d), lambda p, tiles, *_: (tiles[p], 0)),
                      pl.BlockSpec((tm, d), lambda p, tiles, blocks, *_: (blocks[p], 0)),
                      pl.BlockSpec((1, d), lambda p, *_: (0, 0)),
                      pl.BlockSpec((1, d), lambda p, *_: (0, 0))],
            out_specs=pl.BlockSpec((tm, d), lambda p, tiles, blocks, *_: (blocks[p], 0)),
            scratch_shapes=[pltpu.VMEM((tm, d), F32)]),
        compiler_params=_params("arbitrary"),
        name="moe_combine_ln",
    )(pairs["tile"], pairs["block"], pairs["window"], pairs["flag"], pos_col, o_sorted, x,
      g.reshape(1, d), b.reshape(1, d))


def _moe_ln(x, xb, w_router, b_router, wg, wu, wd, g, b, alpha):
    t, _ = x.shape
    n_experts = wg.shape[0]
    tile = _tile(t, MOE_TILE)
    n_tiles = TOP_K * t // tile + n_experts
    meta_col, meta_row, gates, cnt = _route(x, w_router, b_router, tile)
    plan = _moe_plan(cnt, n_experts, tile, n_tiles)
    pos_row = _with_positions(meta_row, plan["row_off"], 1)
    pos_col = _with_positions(meta_col, plan["row_off"], 1)
    xs, gs = _moe_gather(plan, pos_row, gates, xb, tile, n_tiles)
    o_sorted = _moe_ffn(plan, xs, gs, wg, wu, wd, tile)
    return _moe_combine_ln(plan, pos_col, o_sorted, x, g, b, alpha, tile, tile)


def _even_layer(x, bsz, s, alpha, ln1_g, ln1_b, w_in, ret_gn_g, w_out, ln2_g, ln2_b, w_gate, w_up, w_down):
    d = x.shape[1]
    n_heads = d // HEAD_DIM
    n_sb = n_heads // 2
    n_ret = n_heads - n_sb
    h = _proj(x, w_in.astype(BF16))
    h3 = h.reshape(bsz, s, h.shape[1])
    o_sb = _stick_breaking(h3, n_sb, 0)
    o_ret = _retention(h3, ret_gn_g, n_ret, 3 * n_sb * HEAD_DIM // LANES)
    x = _out_ln([o_sb.reshape(bsz * s, -1), o_ret.reshape(bsz * s, -1)], w_out.astype(BF16),
                x, ln1_g, ln1_b, alpha)
    return _swiglu_ln(x, w_gate.astype(BF16), w_up.astype(BF16), w_down.astype(BF16), ln2_g, ln2_b, alpha)


def _odd_layer(x, bsz, s, alpha, ln1_g, ln1_b, w_qkv, rel_bias, w_out, ln2_g, ln2_b, w_router, b_router,
               w_gate, w_up, w_down):
    d = x.shape[1]
    qkv = _proj(x, w_qkv.astype(BF16))
    o = _chunk_attention(qkv.reshape(bsz, s, 3 * d), rel_bias, d // HEAD_DIM)
    x, xb = _out_ln([o.reshape(bsz * s, d)], w_out.astype(BF16), x, ln1_g, ln1_b, alpha, also_bf16=True)
    return _moe_ln(x, xb, w_router, b_router, w_gate.astype(BF16), w_up.astype(BF16), w_down.astype(BF16),
                   ln2_g, ln2_b, alpha)


def kernel(x, even_ln1_g, even_ln1_b, even_w_in, even_ret_gn_g, even_w_out, even_ln2_g, even_ln2_b,
           even_w_gate, even_w_up, even_w_down, odd_ln1_g, odd_ln1_b, odd_w_qkv, odd_rel_bias, odd_w_out,
           odd_ln2_g, odd_ln2_b, odd_w_router, odd_b_router, odd_w_gate, odd_w_up, odd_w_down):
    bsz, s, d = x.shape
    depth = even_w_in.shape[0] + odd_w_qkv.shape[0]
    alpha = (2 * depth) ** 0.25
    xt = x.reshape(bsz * s, d)
    for layer in range(depth):
        i = layer // 2
        if layer % 2 == 0:
            xt = _even_layer(xt, bsz, s, alpha, even_ln1_g[i], even_ln1_b[i], even_w_in[i], even_ret_gn_g[i],
                             even_w_out[i], even_ln2_g[i], even_ln2_b[i], even_w_gate[i], even_w_up[i],
                             even_w_down[i])
        else:
            xt = _odd_layer(xt, bsz, s, alpha, odd_ln1_g[i], odd_ln1_b[i], odd_w_qkv[i], odd_rel_bias[i],
                            odd_w_out[i], odd_ln2_g[i], odd_ln2_b[i], odd_w_router[i], odd_b_router[i],
                            odd_w_gate[i], odd_w_up[i], odd_w_down[i])
    return xt.reshape(bsz, s, d)
```

```python
import functools

import jax
import jax.numpy as jnp
from jax import lax
from jax.experimental import pallas as pl
from jax.experimental.pallas import tpu as pltpu

HEAD_DIM = 64
LANES = 128
CHUNK = 64
LEFT_CHUNKS = 8
MAX_REL = 4 * CHUNK
TOP_K = 2
ROPE_BASE = 10000.0
LN_EPS = 1e-5
NEG_BIG = -1e30
VMEM_LIMIT = 56 * 1024 * 1024

BF16 = jnp.bfloat16
F32 = jnp.float32


def _params(*sem):
    return pltpu.CompilerParams(dimension_semantics=sem, vmem_limit_bytes=VMEM_LIMIT)


def _tile(n, target):
    if n <= target:
        return n
    t = target - target % LANES
    while t >= LANES:
        if n % t == 0:
            return t
        t -= LANES
    return n


def _dot(a, b):
    return jnp.dot(a, b, preferred_element_type=F32)


def _dot_nt(a, b):
    return lax.dot_general(a, b, (((1,), (1,)), ((), ())), preferred_element_type=F32)


def _split_dot(a, b_bf16):
    hi = a.astype(BF16)
    lo = (a - hi.astype(F32)).astype(BF16)
    return _dot(hi, b_bf16) + _dot(lo, b_bf16)


def _layer_norm(r, g, b):
    mu = jnp.mean(r, axis=-1, keepdims=True)
    d = r - mu
    var = jnp.mean(d * d, axis=-1, keepdims=True)
    return d * lax.rsqrt(var + LN_EPS) * g + b


def _proj_kernel(x_ref, w_ref, o_ref, xb_ref):
    @pl.when(pl.program_id(1) == 0)
    def _():
        xb_ref[...] = x_ref[...].astype(BF16)

    o_ref[...] = _dot(xb_ref[...], w_ref[...]).astype(o_ref.dtype)


def _proj(x, w, tm_target=1024, tn_target=1792):
    t, k = x.shape
    n = w.shape[1]
    tm, tn = _tile(t, tm_target), _tile(n, tn_target)
    return pl.pallas_call(
        _proj_kernel,
        out_shape=jax.ShapeDtypeStruct((t, n), BF16),
        grid=(t // tm, n // tn),
        in_specs=[pl.BlockSpec((tm, k), lambda i, j: (i, 0)),
                  pl.BlockSpec((k, tn), lambda i, j: (0, j))],
        out_specs=pl.BlockSpec((tm, tn), lambda i, j: (i, j)),
        scratch_shapes=[pltpu.VMEM((tm, k), BF16)],
        compiler_params=_params("parallel", "arbitrary"),
        name="proj",
    )(x, w)


def _out_ln_kernel(alpha, n_a, *refs):
    a_refs = refs[:n_a]
    w_ref, x_ref, g_ref, b_ref = refs[n_a:n_a + 4]
    o_refs = refs[n_a + 4:]
    ka = a_refs[0].shape[1]
    y = _dot(a_refs[0][...], w_ref[0:ka, :])
    for idx in range(1, n_a):
        y = y + _dot(a_refs[idx][...], w_ref[idx * ka:(idx + 1) * ka, :])
    out = _layer_norm(alpha * x_ref[...] + y, g_ref[...], b_ref[...])
    for o_ref in o_refs:
        o_ref[...] = out.astype(o_ref.dtype)


def _out_ln(a_list, w, x, g, b, alpha, also_bf16=False, tm_target=512):
    t, d = x.shape
    tm = _tile(t, tm_target)
    ka = a_list[0].shape[1]
    row = lambda i: (i, 0)
    fixed = lambda i: (0, 0)
    dtypes = [F32, BF16] if also_bf16 else [F32]
    outs = pl.pallas_call(
        functools.partial(_out_ln_kernel, alpha, len(a_list)),
        out_shape=[jax.ShapeDtypeStruct((t, d), dt) for dt in dtypes],
        grid=(t // tm,),
        in_specs=[pl.BlockSpec((tm, ka), row) for _ in a_list]
        + [pl.BlockSpec(w.shape, fixed), pl.BlockSpec((tm, d), row),
           pl.BlockSpec((1, d), fixed), pl.BlockSpec((1, d), fixed)],
        out_specs=[pl.BlockSpec((tm, d), row) for _ in dtypes],
        compiler_params=_params("parallel"),
        name="out_ln",
    )(*a_list, w, x, g.reshape(1, d), b.reshape(1, d))
    return outs if also_bf16 else outs[0]


def _sb_kernel(q_ref, k_ref, v_ref, o_ref, acc_ref, z_ref, w_ref, carry_ref):
    tq = q_ref.shape[1]
    n_heads = 2 * (q_ref.shape[2] // LANES)
    qi = pl.program_id(2)
    lane = lax.broadcasted_iota(jnp.int32, (1, LANES), 1)
    head0 = lane < HEAD_DIM
    row = lax.broadcasted_iota(jnp.int32, (tq, tq), 0)
    col = lax.broadcasted_iota(jnp.int32, (tq, tq), 1)
    after = (row > col).astype(BF16)
    strictly_causal = col < row

    def pair_lanes(h):
        return slice((h // 2) * LANES, (h // 2 + 1) * LANES)

    def masked_q(h):
        q = q_ref[0, :, pair_lanes(h)] * jnp.asarray(HEAD_DIM ** -0.5, BF16)
        return jnp.where(head0, q, jnp.zeros_like(q)) if h % 2 == 0 else jnp.where(head0, jnp.zeros_like(q), q)

    q_heads = [masked_q(h) for h in range(n_heads)]

    def key_block(ref, kb, h):
        return ref[0, pl.ds(pl.multiple_of(kb * tq, tq), tq), pair_lanes(h)]

    def add_values(kb, h):
        acc_ref[h] += _dot(w_ref[h], key_block(v_ref, kb, h))

    def step(kb, diagonal=False):
        log_betas, log_1ms, suffixes = [], [], []
        kb_next = jnp.maximum(kb - 1, 0)
        for h in range(n_heads):
            z = z_ref[h]
            neg_z = -z
            log_1m = jnp.minimum(neg_z, 0.0) - jnp.log(1.0 + jnp.exp(jnp.minimum(z, neg_z)))
            log_betas.append(z + log_1m)
            if diagonal:
                log_1m = jnp.where(strictly_causal, log_1m, 0.0)
            log_1ms.append(log_1m[:, 0:1])
            suffixes.append(_dot(log_1m.astype(BF16), after))
            z_ref[h] = _dot_nt(q_heads[h], key_block(k_ref, kb_next, h))
            if not diagonal:
                add_values(kb + 1, h)
        for h in range(n_heads):
            carry = carry_ref[h]
            w = jnp.exp(log_betas[h] + suffixes[h] + carry)
            if diagonal:
                w = jnp.where(strictly_causal, w, 0.0)
            w_ref[h] = w.astype(BF16)
            carry_ref[h] = carry + suffixes[h][:, 0:1] + log_1ms[h]

    def later_step(it, _):
        step(qi - 1 - it)
        return 0

    for h in range(n_heads):
        z_ref[h] = _dot_nt(q_heads[h], key_block(k_ref, qi, h))
    acc_ref[...] = jnp.zeros_like(acc_ref)
    carry_ref[...] = jnp.zeros_like(carry_ref)
    step(qi, True)
    lax.fori_loop(0, qi, later_step, 0)
    for h in range(n_heads):
        add_values(0, h)
    for h in range(0, n_heads, 2):
        o_ref[0, :, pair_lanes(h)] = jnp.where(head0, acc_ref[h], acc_ref[h + 1]).astype(o_ref.dtype)


def _stick_breaking(h3, n_heads, col0, tq_target=256):
    bsz, s, _ = h3.shape
    tq = _tile(s, tq_target)
    pairs = n_heads // 2
    per_step = 2 if pairs % 2 == 0 and col0 % 2 == 0 else 1
    groups, lanes, first = pairs // per_step, per_step * LANES, col0 // per_step
    return pl.pallas_call(
        _sb_kernel,
        out_shape=jax.ShapeDtypeStruct((bsz, s, pairs * LANES), BF16),
        grid=(bsz, groups, s // tq),
        in_specs=[pl.BlockSpec((1, tq, lanes), lambda b, p, i: (b, i, first + p)),
                  pl.BlockSpec((1, s, lanes), lambda b, p, i: (b, 0, first + groups + p)),
                  pl.BlockSpec((1, s, lanes), lambda b, p, i: (b, 0, first + 2 * groups + p))],
        out_specs=pl.BlockSpec((1, tq, lanes), lambda b, p, i: (b, i, p)),
        scratch_shapes=[pltpu.VMEM((2 * per_step, tq, LANES), F32), pltpu.VMEM((2 * per_step, tq, tq), F32),
                        pltpu.VMEM((2 * per_step, tq, tq), BF16), pltpu.VMEM((2 * per_step, tq, 1), F32)],
        compiler_params=_params("parallel", "parallel", "arbitrary"),
        name="stick_breaking",
    )(h3, h3, h3)


def _ret_kernel(q_ref, k_ref, v_ref, gate_ref, cos_ref, sin_ref, din_ref, dq_ref, dk_ref,
                dc_ref, gn_ref, o_ref, state_ref):
    @pl.when(pl.program_id(2) == 0)
    def _():
        state_ref[...] = jnp.zeros_like(state_ref)

    lane = lax.broadcasted_iota(jnp.int32, (1, LANES), 1)
    head0 = lane < HEAD_DIM
    first_half = (lane & (HEAD_DIM // 2)) == 0
    row = lax.broadcasted_iota(jnp.int32, (LANES, LANES), 0) < HEAD_DIM
    col = lax.broadcasted_iota(jnp.int32, (LANES, LANES), 1) < HEAD_DIM
    same_head = row == col
    group_mean = jnp.where(same_head, 1.0 / HEAD_DIM, 0.0).astype(BF16)
    cos = cos_ref[...]
    sin = sin_ref[...]

    def rotary(x):
        half = HEAD_DIM // 2
        swapped = jnp.where(first_half, pltpu.roll(x, LANES - half, 1), pltpu.roll(x, half, 1))
        return x * cos + swapped * sin

    pairs = range(q_ref.shape[2] // LANES)
    lanes = lambda p: slice(p * LANES, (p + 1) * LANES)
    qb, kb, v, scores, cross, outer = [], [], [], [], [], []
    for p in pairs:
        q = rotary(q_ref[0, :, lanes(p)].astype(F32)) * (HEAD_DIM ** -0.5)
        k = rotary(k_ref[0, :, lanes(p)].astype(F32))
        v.append(v_ref[0, :, lanes(p)])
        qb.append(q.astype(BF16))
        kb.append(k.astype(BF16))
        scores.append([_dot_nt(jnp.where(mask, qb[p], jnp.zeros_like(qb[p])), kb[p])
                       for mask in (head0, jnp.logical_not(head0))])
        cross.append(_dot(qb[p], state_ref[p].astype(BF16)))
        outer.append(_dot((k * dk_ref[p]).T.astype(BF16), v[p]))
    intra = [[_dot((scores[p][h] * din_ref[2 * p + h]).astype(BF16), v[p]) for h in range(2)] for p in pairs]
    for p in pairs:
        state_ref[p] = state_ref[p] * dc_ref[p] + jnp.where(same_head, outer[p], 0.0)
        o = jnp.where(head0, intra[p][0], intra[p][1]) + cross[p] * dq_ref[p]
        mu = _split_dot(o, group_mean)
        d = o - mu
        var = _split_dot(d * d, group_mean)
        y = d * lax.rsqrt(var + LN_EPS) * gn_ref[:, lanes(p)]
        g = gate_ref[0, :, lanes(p)].astype(F32)
        o_ref[0, :, lanes(p)] = (g / (1.0 + jnp.exp(-g)) * y).astype(o_ref.dtype)


def _retention(h3, gn_g, n_heads, col0, chunk_target=256):
    bsz, s, _ = h3.shape
    c = _tile(s, chunk_target)
    pairs = n_heads // 2
    half = HEAD_DIM // 2

    pos = jnp.arange(s, dtype=F32)
    inv_freq = ROPE_BASE ** (-jnp.arange(half, dtype=F32) / half)
    ang = pos[:, None] * inv_freq[None, :]
    cos = jnp.tile(jnp.cos(ang), (1, LANES // half))
    sin = jnp.tile(jnp.concatenate([-jnp.sin(ang), jnp.sin(ang)], axis=1), (1, LANES // HEAD_DIM))

    gamma = 1.0 - 2.0 ** (-5.0 - jnp.arange(n_heads, dtype=F32))
    log_g = jnp.log(gamma)
    i = jnp.arange(c, dtype=F32)
    diff = i[:, None] - i[None, :]
    d_in = jnp.where(diff >= 0, jnp.exp(log_g[:, None, None] * jnp.maximum(diff, 0.0)), 0.0)
    per_lane = lambda t: jnp.repeat(t.reshape(pairs, 2, -1), HEAD_DIM, axis=1)
    d_q = jnp.swapaxes(per_lane(jnp.exp(log_g[:, None] * (i + 1.0))), 1, 2)
    d_k = jnp.swapaxes(per_lane(jnp.exp(log_g[:, None] * (c - 1.0 - i))), 1, 2)
    d_c = jnp.swapaxes(per_lane(jnp.exp(log_g * c)[:, None]), 1, 2)

    per_step = 2 if pairs % 2 == 0 and col0 % 2 == 0 else 1
    groups, lanes, first = pairs // per_step, per_step * LANES, col0 // per_step
    blk = lambda off: pl.BlockSpec((1, c, lanes), lambda b, p, t: (b, t, first + off * groups + p))
    return pl.pallas_call(
        _ret_kernel,
        out_shape=jax.ShapeDtypeStruct((bsz, s, pairs * LANES), BF16),
        grid=(bsz, groups, s // c),
        in_specs=[blk(0), blk(1), blk(2), blk(3),
                  pl.BlockSpec((c, LANES), lambda b, p, t: (t, 0)),
                  pl.BlockSpec((c, LANES), lambda b, p, t: (t, 0)),
                  pl.BlockSpec((2 * per_step, c, c), lambda b, p, t: (p, 0, 0)),
                  pl.BlockSpec((per_step, c, LANES), lambda b, p, t: (p, 0, 0)),
                  pl.BlockSpec((per_step, c, LANES), lambda b, p, t: (p, 0, 0)),
                  pl.BlockSpec((per_step, 1, LANES), lambda b, p, t: (p, 0, 0)),
                  pl.BlockSpec((1, lanes), lambda b, p, t: (0, p))],
        out_specs=pl.BlockSpec((1, c, lanes), lambda b, p, t: (b, t, p)),
        scratch_shapes=[pltpu.VMEM((per_step, LANES, LANES), F32)],
        compiler_params=_params("parallel", "parallel", "arbitrary"),
        name="retention",
    )(h3, h3, h3, h3, cos, sin, d_in, d_q, d_k, d_c, gn_g.reshape(1, -1))


def _chunk_attn_kernel(q_ref, k_ref, v_ref, bias_ref, o_ref):
    tq = q_ref.shape[1]
    t0 = pl.program_id(2) * tq
    n_pairs = q_ref.shape[2] // LANES
    lane = lax.broadcasted_iota(jnp.int32, (1, LANES), 1)
    head0 = lane < HEAD_DIM

    width = bias_ref.shape[3]
    start = pl.multiple_of(jnp.maximum(t0 - LEFT_CHUNKS * CHUNK, 0), tq)

    def pair_slice(ref, pair, rows):
        return ref[0, rows, pair * LANES:(pair + 1) * LANES]

    logits = []
    for pair in range(n_pairs):
        q = pair_slice(q_ref, pair, slice(None)) * jnp.asarray(HEAD_DIM ** -0.5, BF16)
        k = pair_slice(k_ref, pair, pl.ds(start, width))
        logits.append(_dot_nt(jnp.where(head0, q, jnp.zeros_like(q)), k))
        logits.append(_dot_nt(jnp.where(head0, jnp.zeros_like(q), q), k))
    for pair in range(n_pairs):
        probs, denoms = [], []
        for h in range(2):
            head = 2 * pair + h
            s = logits[head] + bias_ref[0, 0, head * tq:(head + 1) * tq, :]
            p = jnp.exp(s - jnp.max(s, axis=-1, keepdims=True))
            denoms.append(jnp.sum(p, axis=-1, keepdims=True))
            probs.append(p.astype(BF16))
        v = pair_slice(v_ref, pair, pl.ds(start, width))
        outs = [_dot(probs[h], v) / denoms[h] for h in range(2)]
        o_ref[0, :, pair * LANES:(pair + 1) * LANES] = jnp.where(head0, outs[0], outs[1]).astype(o_ref.dtype)


def _window_bias(rel_bias, tq, width):
    n_heads = rel_bias.shape[0]
    left = LEFT_CHUNKS * CHUNK
    rel_min, rel_max = left - (width - 1), left + tq - 1
    assert rel_min >= -MAX_REL and rel_max >= MAX_REL
    by_rel = jnp.concatenate(
        [rel_bias[:, rel_min + MAX_REL:], jnp.broadcast_to(rel_bias[:, -1:], (n_heads, rel_max - MAX_REL))], axis=1)
    m = rel_max - rel_min + 1
    rev = jnp.concatenate([by_rel[:, ::-1], jnp.zeros((n_heads, 1), by_rel.dtype)], axis=1)
    rows = jnp.tile(rev, (1, tq))[:, :tq * m].reshape(n_heads, tq, m)
    bias = rows[:, :, tq - 1:tq - 1 + width].astype(F32)
    q_chunk = jnp.arange(tq)[:, None] // CHUNK + LEFT_CHUNKS
    k_chunk = jnp.arange(width)[None, :] // CHUNK
    visible = (k_chunk >= q_chunk - LEFT_CHUNKS) & (k_chunk <= q_chunk)
    return jnp.where(visible[None], bias, NEG_BIG)


def _chunk_attention(qkv3, rel_bias, n_heads, tq=256):
    bsz, s, _ = qkv3.shape
    pairs = n_heads // 2
    n_shift = (LEFT_CHUNKS * CHUNK) // tq
    width = (n_shift + 1) * tq
    assert s >= width
    bias = _window_bias(rel_bias, tq, width)
    per_step = next(n for n in (4, 2, 1) if pairs % n == 0)
    groups, heads = pairs // per_step, 2 * per_step
    lanes = per_step * LANES
    bias = jnp.stack([jnp.pad(bias[:, :, v * tq:], ((0, 0), (0, 0), (0, v * tq)), constant_values=NEG_BIG)
                      for v in range(n_shift + 1)], axis=1).reshape(groups, heads, n_shift + 1, tq, width)
    bias = bias.transpose(0, 2, 1, 3, 4).reshape(groups, n_shift + 1, heads * tq, width)

    return pl.pallas_call(
        _chunk_attn_kernel,
        out_shape=jax.ShapeDtypeStruct((bsz, s, pairs * LANES), BF16),
        grid=(bsz, groups, s // tq),
        in_specs=[pl.BlockSpec((1, tq, lanes), lambda b, p, t: (b, t, p)),
                  pl.BlockSpec((1, s, lanes), lambda b, p, t: (b, 0, groups + p)),
                  pl.BlockSpec((1, s, lanes), lambda b, p, t: (b, 0, 2 * groups + p)),
                  pl.BlockSpec((1, 1, heads * tq, width), lambda b, p, t: (p, jnp.maximum(n_shift - t, 0), 0, 0))],
        out_specs=pl.BlockSpec((1, tq, lanes), lambda b, p, t: (b, t, p)),
        compiler_params=_params("parallel", "parallel", "arbitrary"),
        name="chunk_attention",
    )(qkv3, qkv3, qkv3, bias)


def _swiglu_ln_kernel(alpha, x_ref, wg_ref, wu_ref, wd_ref, g_ref, b_ref, o_ref, xb_ref, acc_ref):
    f = pl.program_id(1)

    @pl.when(f == 0)
    def _():
        xb_ref[...] = x_ref[...].astype(BF16)
        acc_ref[...] = jnp.zeros_like(acc_ref)

    xb = xb_ref[...]
    gate = _dot(xb, wg_ref[...])
    up = _dot(xb, wu_ref[...])
    hidden = (gate / (1.0 + jnp.exp(-gate)) * up).astype(BF16)
    acc_ref[...] += _dot(hidden, wd_ref[...])

    @pl.when(f == pl.num_programs(1) - 1)
    def _():
        o_ref[...] = _layer_norm(alpha * x_ref[...] + acc_ref[...], g_ref[...], b_ref[...])


def _swiglu_ln(x, wg, wu, wd, g, b, alpha, tm_target=512, tf_target=2816):
    t, d = x.shape
    ff = wg.shape[1]
    tm, tf = _tile(t, tm_target), _tile(ff, tf_target)
    once = dict(pipeline_mode=pl.Buffered(1)) if tf == ff else {}
    return pl.pallas_call(
        functools.partial(_swiglu_ln_kernel, alpha),
        out_shape=jax.ShapeDtypeStruct((t, d), F32),
        grid=(t // tm, ff // tf),
        in_specs=[pl.BlockSpec((tm, d), lambda i, f: (i, 0)),
                  pl.BlockSpec((d, tf), lambda i, f: (0, f), **once),
                  pl.BlockSpec((d, tf), lambda i, f: (0, f), **once),
                  pl.BlockSpec((tf, d), lambda i, f: (f, 0), **once),
                  pl.BlockSpec((1, d), lambda i, f: (0, 0)),
                  pl.BlockSpec((1, d), lambda i, f: (0, 0))],
        out_specs=pl.BlockSpec((tm, d), lambda i, f: (i, 0)),
        scratch_shapes=[pltpu.VMEM((tm, d), BF16), pltpu.VMEM((tm, d), F32)],
        compiler_params=_params("parallel", "arbitrary"),
        name="swiglu_ln",
    )(x, wg, wu, wd, g.reshape(1, d), b.reshape(1, d))


ROUTE_E1, ROUTE_E2, ROUTE_RANK1, ROUTE_RANK2 = range(4)
ROUTE_FIELDS = 8
MOE_TILE = 512
MOE_WINDOW = 256
ROW_ALIGN = 16
PAIR_LIVE, PAIR_FIRST, PAIR_LAST, PAIR_NARROW = 1, 2, 4, 8


def _route_kernel(n_experts, x_ref, w_ref, b_ref, col_ref, row_ref, gate_ref, cnt_ref, run_ref):
    @pl.when(pl.program_id(0) == 0)
    def _():
        run_ref[...] = jnp.zeros_like(run_ref)

    x = x_ref[...]
    w = w_ref[...]
    tm = x.shape[0]
    x_hi = x.astype(BF16)
    x_lo = (x - x_hi.astype(F32)).astype(BF16)
    w_hi = w.astype(BF16)
    w_lo = (w - w_hi.astype(F32)).astype(BF16)
    logits = _dot(x_hi, w_hi) + (_dot(x_hi, w_lo) + _dot(x_lo, w_hi)) + b_ref[...]
    lane = lax.broadcasted_iota(jnp.int32, logits.shape, 1).astype(F32)
    neg_inf = jnp.asarray(-jnp.inf, F32)
    logits = jnp.where(lane < n_experts, logits, neg_inf)
    top1 = jnp.max(logits, axis=-1, keepdims=True)
    idx1 = jnp.min(jnp.where(logits == top1, lane, float(LANES)), axis=-1, keepdims=True)
    rest = jnp.where(lane == idx1, neg_inf, logits)
    top2 = jnp.max(rest, axis=-1, keepdims=True)
    idx2 = jnp.min(jnp.where(rest == top2, lane, float(LANES)), axis=-1, keepdims=True)
    e2 = jnp.exp(top2 - top1)
    denom = 1.0 + e2

    chosen = jnp.where(jnp.logical_or(lane == idx1, lane == idx2), 1.0, 0.0)
    r = lax.broadcasted_iota(jnp.int32, (tm, tm), 0)
    c = lax.broadcasted_iota(jnp.int32, (tm, tm), 1)
    earlier = (c < r).astype(BF16)
    before = _dot(earlier, chosen.astype(BF16)) + run_ref[...]
    rank1 = jnp.sum(jnp.where(lane == idx1, before, 0.0), axis=-1, keepdims=True)
    rank2 = jnp.sum(jnp.where(lane == idx2, before, 0.0), axis=-1, keepdims=True)
    total = jnp.sum(chosen, axis=0, keepdims=True)
    cnt_ref[0] = total
    run_ref[...] += total

    fields = {ROUTE_E1: idx1, ROUTE_E2: idx2, ROUTE_RANK1: rank1, ROUTE_RANK2: rank2}
    meta = jnp.zeros_like(logits)
    for field, value in fields.items():
        meta = jnp.where(lane == field, value, meta)
    col_ref[...] = meta
    row_ref[0] = meta.T[0:ROUTE_FIELDS, :]
    gates = jnp.where(lane == idx1, 1.0 / denom, jnp.where(lane == idx2, e2 / denom, 0.0))
    gates_hi = gates.astype(BF16)
    gate_ref[:, 0:LANES] = gates_hi
    gate_ref[:, LANES:2 * LANES] = (gates - gates_hi.astype(F32)).astype(BF16)


def _route(x, w_router, b_router, tm):
    t, d = x.shape
    n_experts = w_router.shape[1]
    nb = t // tm
    w = jnp.pad(w_router.astype(F32), ((0, 0), (0, LANES - n_experts)))
    b = jnp.pad(b_router.astype(F32), (0, LANES - n_experts)).reshape(1, LANES)
    return pl.pallas_call(
        functools.partial(_route_kernel, n_experts),
        out_shape=[jax.ShapeDtypeStruct((t, LANES), F32),
                   jax.ShapeDtypeStruct((nb, ROUTE_FIELDS, tm), F32),
                   jax.ShapeDtypeStruct((t, 2 * LANES), BF16),
                   jax.ShapeDtypeStruct((nb, 1, LANES), F32)],
        grid=(nb,),
        in_specs=[pl.BlockSpec((tm, d), lambda i: (i, 0)),
                  pl.BlockSpec((d, LANES), lambda i: (0, 0)),
                  pl.BlockSpec((1, LANES), lambda i: (0, 0))],
        out_specs=[pl.BlockSpec((tm, LANES), lambda i: (i, 0)),
                   pl.BlockSpec((1, ROUTE_FIELDS, tm), lambda i: (i, 0, 0)),
                   pl.BlockSpec((tm, 2 * LANES), lambda i: (i, 0)),
                   pl.BlockSpec((1, 1, LANES), lambda i: (i, 0, 0))],
        scratch_shapes=[pltpu.VMEM((1, LANES), F32)],
        compiler_params=_params("arbitrary"),
        name="route",
    )(x, w, b)


def _moe_plan(cnt, n_experts, tile, n_tiles):
    nb = cnt.shape[0]
    counts = cnt[:, 0, :n_experts].astype(jnp.int32)
    cum = jnp.cumsum(counts, axis=0) - counts
    total = jnp.sum(counts, axis=0)
    tiles_e = (total + tile - 1) // tile
    tile_end = jnp.cumsum(tiles_e)
    row_off = (tile_end - tiles_e) * tile
    start = row_off[None, :] + cum
    j0 = start // tile
    j1 = (start + counts - 1) // tile
    has = counts > 0
    tiles = jnp.stack([j0, j0 + 1], axis=-1)
    valid = jnp.stack([has, has & (j1 > j0)], axis=-1)
    lo = jnp.clip(start[..., None] - tiles * tile, 0, tile)
    hi = jnp.clip((start + counts)[..., None] - tiles * tile, 0, tile)
    win = min(MOE_WINDOW, tile)
    window = jnp.minimum(lo // ROW_ALIGN * ROW_ALIGN, tile - win)
    narrow = hi <= window + win

    n_pairs = nb * n_experts + n_tiles
    slot = jnp.arange(n_pairs)
    block_id = jnp.broadcast_to(jnp.arange(nb)[:, None, None], tiles.shape)
    expert_id = jnp.broadcast_to(jnp.arange(n_experts)[None, :, None], tiles.shape)

    def listed(order, group_key):
        flat = lambda a: a.transpose(order).reshape(-1)
        n_live = jnp.sum(valid)
        idx = jnp.nonzero(flat(valid), size=n_pairs, fill_value=0)[0]
        idx = jnp.where(slot < n_live, idx, idx[n_live - 1])
        fields = {name: flat(a)[idx].astype(jnp.int32)
                  for name, a in (("tile", tiles), ("block", block_id), ("expert", expert_id),
                                  ("window", window), ("narrow", narrow))}
        key = fields[group_key]
        live = slot < n_live
        first = live & (key != jnp.concatenate([jnp.full((1,), -1, key.dtype), key[:-1]]))
        last = live & ((slot == n_live - 1) | (key != jnp.concatenate([key[1:], key[-1:]])))
        fields["flag"] = (PAIR_LIVE * live + PAIR_FIRST * first + PAIR_LAST * last
                          + PAIR_NARROW * fields.pop("narrow")).astype(jnp.int32)
        return fields

    by_expert = listed((1, 0, 2), "tile")
    by_block = listed((0, 1, 2), "block")
    n_used = tile_end[-1:].astype(jnp.int32)
    tile_expert = jnp.minimum(jnp.searchsorted(tile_end, jnp.arange(n_tiles), side="right"),
                              n_experts - 1).astype(jnp.int32)
    return dict(by_expert=by_expert, by_block=by_block, row_off=row_off.astype(jnp.int32),
                n_used=n_used, tile_expert=tile_expert)


def _with_positions(meta, row_off, field_axis):
    def take(first):
        return lax.slice_in_dim(meta, first, first + TOP_K, axis=field_axis)

    expert, offset = take(ROUTE_E1), jnp.zeros_like(take(ROUTE_E1))
    for e in range(row_off.shape[0]):
        offset = jnp.where(expert == float(e), row_off[e].astype(F32), offset)
    n_fields = meta.shape[field_axis]
    return jnp.concatenate([take(ROUTE_E1), take(ROUTE_RANK1) + offset,
                            lax.slice_in_dim(meta, ROUTE_RANK1 + TOP_K, n_fields, axis=field_axis)], axis=field_axis)


def _gather_kernel(tile_ref, block_ref, window_ref, flag_ref, row_ref, gate_ref, x_ref, xs_ref, gs_ref, acc_ref):
    pair = pl.program_id(0)
    flag = flag_ref[pair]
    narrow = (flag & PAIR_NARROW) != 0
    tile, tm = xs_ref.shape[0], x_ref.shape[0]

    @pl.when((flag & PAIR_LIVE) != 0)
    def _():
        @pl.when((flag & PAIR_FIRST) != 0)
        def _():
            acc_ref[...] = jnp.zeros_like(acc_ref)
            gs_ref[...] = jnp.zeros_like(gs_ref)

        pos1 = row_ref[0, ROUTE_RANK1:ROUTE_RANK1 + 1, :].astype(jnp.int32)
        pos2 = row_ref[0, ROUTE_RANK2:ROUTE_RANK2 + 1, :].astype(jnp.int32)

        def place(row0, n_rows):
            rows = tile_ref[pair] * tile + row0 + lax.broadcasted_iota(jnp.int32, (n_rows, tm), 0)
            onehot = jnp.where(jnp.logical_or(rows == pos1, rows == pos2), 1.0, 0.0).astype(BF16)
            acc_ref[pl.ds(row0, n_rows), :] += _dot(onehot, x_ref[...])
            gs_ref[pl.ds(row0, n_rows), :] += _dot(onehot, gate_ref[...])

        @pl.when(narrow)
        def _():
            place(pl.multiple_of(window_ref[pair], ROW_ALIGN), min(MOE_WINDOW, tile))

        @pl.when(jnp.logical_not(narrow))
        def _():
            place(0, tile)

        @pl.when((flag & PAIR_LAST) != 0)
        def _():
            xs_ref[...] = acc_ref[...].astype(BF16)


def _moe_gather(plan, pos_row, gates, xb, tile, n_tiles):
    t, d = xb.shape
    _, _, tm = pos_row.shape
    pairs = plan["by_expert"]
    return pl.pallas_call(
        _gather_kernel,
        out_shape=[jax.ShapeDtypeStruct((n_tiles * tile, d), BF16),
                   jax.ShapeDtypeStruct((n_tiles * tile, 2 * LANES), F32)],
        grid_spec=pltpu.PrefetchScalarGridSpec(
            num_scalar_prefetch=4,
            grid=(pairs["tile"].shape[0],),
            in_specs=[pl.BlockSpec((1, ROUTE_FIELDS, tm), lambda p, tiles, blocks, *_: (blocks[p], 0, 0)),
                      pl.BlockSpec((tm, 2 * LANES), lambda p, tiles, blocks, *_: (blocks[p], 0)),
                      pl.BlockSpec((tm, d), lambda p, tiles, blocks, *_: (blocks[p], 0))],
            out_specs=[pl.BlockSpec((tile, d), lambda p, tiles, *_: (tiles[p], 0)),
                       pl.BlockSpec((tile, 2 * LANES), lambda p, tiles, *_: (tiles[p], 0))],
            scratch_shapes=[pltpu.VMEM((tile, d), F32)]),
        compiler_params=_params("arbitrary"),
        name="moe_gather",
    )(pairs["tile"], pairs["block"], pairs["window"], pairs["flag"], pos_row, gates, xb)


def _moe_ffn_kernel(te_ref, used_ref, xs_ref, gs_ref, wg_ref, wu_ref, wd_ref, o_ref, acc_ref):
    f = pl.program_id(1)
    tile_id = pl.program_id(0)

    @pl.when(tile_id < used_ref[0])
    def _():
        @pl.when(f == 0)
        def _():
            acc_ref[...] = jnp.zeros_like(acc_ref)

        xs = xs_ref[...]
        gate = _dot(xs, wg_ref[0])
        up = _dot(xs, wu_ref[0])
        hidden = (gate / (1.0 + jnp.exp(-gate)) * up).astype(BF16)
        acc_ref[...] += _dot(hidden, wd_ref[0])

        @pl.when(f == pl.num_programs(1) - 1)
        def _():
            by_expert = gs_ref[:, 0:LANES] + gs_ref[:, LANES:2 * LANES]
            lane = lax.broadcasted_iota(jnp.int32, by_expert.shape, 1)
            row_gate = jnp.sum(jnp.where(lane == te_ref[tile_id], by_expert, 0.0), axis=1, keepdims=True)
            o_ref[...] = (acc_ref[...] * row_gate).astype(o_ref.dtype)


def _moe_ffn(plan, xs, gs, wg, wu, wd, tile, tf_target=1792):
    rows, d = xs.shape
    ff = wg.shape[2]
    tf = _tile(ff, tf_target)
    nf = ff // tf

    def tile_of(i, used):
        return jnp.minimum(i, used[0] - 1)

    def chunk_of(i, f, used):
        return jnp.where(i < used[0], f, nf - 1)

    return pl.pallas_call(
        _moe_ffn_kernel,
        out_shape=jax.ShapeDtypeStruct((rows, d), BF16),
        grid_spec=pltpu.PrefetchScalarGridSpec(
            num_scalar_prefetch=2,
            grid=(rows // tile, nf),
            in_specs=[pl.BlockSpec((tile, d), lambda i, f, te, used: (tile_of(i, used), 0)),
                      pl.BlockSpec((tile, 2 * LANES), lambda i, f, te, used: (tile_of(i, used), 0)),
                      pl.BlockSpec((1, d, tf), lambda i, f, te, used: (te[tile_of(i, used)], 0, chunk_of(i, f, used))),
                      pl.BlockSpec((1, d, tf), lambda i, f, te, used: (te[tile_of(i, used)], 0, chunk_of(i, f, used))),
                      pl.BlockSpec((1, tf, d), lambda i, f, te, used: (te[tile_of(i, used)], chunk_of(i, f, used), 0))],
            out_specs=pl.BlockSpec((tile, d), lambda i, f, te, used: (tile_of(i, used), 0)),
            scratch_shapes=[pltpu.VMEM((tile, d), F32)]),
        compiler_params=_params("arbitrary", "arbitrary"),
        name="moe_ffn",
    )(plan["tile_expert"], plan["n_used"], xs, gs, wg, wu, wd)


def _combine_ln_kernel(alpha, tile_ref, block_ref, window_ref, flag_ref, col_ref, os_ref, x_ref, g_ref, b_ref, o_ref,
                       acc_ref):
    pair = pl.program_id(0)
    flag = flag_ref[pair]
    narrow = (flag & PAIR_NARROW) != 0
    tm, tile = x_ref.shape[0], os_ref.shape[0]

    @pl.when((flag & PAIR_LIVE) != 0)
    def _():
        @pl.when((flag & PAIR_FIRST) != 0)
        def _():
            acc_ref[...] = jnp.zeros_like(acc_ref)

        pos1 = col_ref[:, ROUTE_RANK1:ROUTE_RANK1 + 1].astype(jnp.int32)
        pos2 = col_ref[:, ROUTE_RANK2:ROUTE_RANK2 + 1].astype(jnp.int32)

        def collect(row0, n_rows):
            cols = tile_ref[pair] * tile + row0 + lax.broadcasted_iota(jnp.int32, (tm, n_rows), 1)
            onehot = jnp.where(jnp.logical_or(cols == pos1, cols == pos2), 1.0, 0.0).astype(BF16)
            acc_ref[...] += _dot(onehot, os_ref[pl.ds(row0, n_rows), :])

        @pl.when(narrow)
        def _():
            collect(pl.multiple_of(window_ref[pair], ROW_ALIGN), min(MOE_WINDOW, tile))

        @pl.when(jnp.logical_not(narrow))
        def _():
            collect(0, tile)

    @pl.when((flag & PAIR_LAST) != 0)
    def _():
        o_ref[...] = _layer_norm(alpha * x_ref[...] + acc_ref[...], g_ref[...], b_ref[...])


def _moe_combine_ln(plan, pos_col, o_sorted, x, g, b, alpha, tile, tm):
    t, d = x.shape
    pairs = plan["by_block"]
    return pl.pallas_call(
        functools.partial(_combine_ln_kernel, alpha),
        out_shape=jax.ShapeDtypeStruct((t, d), F32),
        grid_spec=pltpu.PrefetchScalarGridSpec(
            num_scalar_prefetch=4,
            grid=(pairs["tile"].shape[0],),
            in_specs=[pl.BlockSpec((tm, LANES), lambda p, tiles, blocks, *_: (blocks[p], 0)),
                      pl.BlockSpec((tile, d), lambda p, tiles, *_: (tiles[p], 0)),
                      pl.BlockSpec((tm, d), lambda p, tiles, blocks, *_: (blocks[p], 0)),
                      pl.BlockSpec((1, d), lambda p, *_: (0, 0)),
                      pl.BlockSpec((1, d), lambda p, *_: (0, 0))],
            out_specs=pl.BlockSpec((tm, d), lambda p, tiles, blocks, *_: (blocks[p], 0)),
            scratch_shapes=[pltpu.VMEM((tm, d), F32)]),
        compiler_params=_params("arbitrary"),
        name="moe_combine_ln",
    )(pairs["tile"], pairs["block"], pairs["window"], pairs["flag"], pos_col, o_sorted, x,
      g.reshape(1, d), b.reshape(1, d))


def _moe_ln(x, xb, w_router, b_router, wg, wu, wd, g, b, alpha):
    t, _ = x.shape
    n_experts = wg.shape[0]
    tile = _tile(t, MOE_TILE)
    n_tiles = TOP_K * t // tile + n_experts
    meta_col, meta_row, gates, cnt = _route(x, w_router, b_router, tile)
    plan = _moe_plan(cnt, n_experts, tile, n_tiles)
    pos_row = _with_positions(meta_row, plan["row_off"], 1)
    pos_col = _with_positions(meta_col, plan["row_off"], 1)
    xs, gs = _moe_gather(plan, pos_row, gates, xb, tile, n_tiles)
    o_sorted = _moe_ffn(plan, xs, gs, wg, wu, wd, tile)
    return _moe_combine_ln(plan, pos_col, o_sorted, x, g, b, alpha, tile, tile)


def _even_layer(x, bsz, s, alpha, ln1_g, ln1_b, w_in, ret_gn_g, w_out, ln2_g, ln2_b, w_gate, w_up, w_down):
    d = x.shape[1]
    n_heads = d // HEAD_DIM
    n_sb = n_heads // 2
    n_ret = n_heads - n_sb
    h = _proj(x, w_in.astype(BF16))
    h3 = h.reshape(bsz, s, h.shape[1])
    o_sb = _stick_breaking(h3, n_sb, 0)
    o_ret = _retention(h3, ret_gn_g, n_ret, 3 * n_sb * HEAD_DIM // LANES)
    x = _out_ln([o_sb.reshape(bsz * s, -1), o_ret.reshape(bsz * s, -1)], w_out.astype(BF16),
                x, ln1_g, ln1_b, alpha)
    return _swiglu_ln(x, w_gate.astype(BF16), w_up.astype(BF16), w_down.astype(BF16), ln2_g, ln2_b, alpha)


def _odd_layer(x, bsz, s, alpha, ln1_g, ln1_b, w_qkv, rel_bias, w_out, ln2_g, ln2_b, w_router, b_router,
               w_gate, w_up, w_down):
    d = x.shape[1]
    qkv = _proj(x, w_qkv.astype(BF16))
    o = _chunk_attention(qkv.reshape(bsz, s, 3 * d), rel_bias, d // HEAD_DIM)
    x, xb = _out_ln([o.reshape(bsz * s, d)], w_out.astype(BF16), x, ln1_g, ln1_b, alpha, also_bf16=True)
    return _moe_ln(x, xb, w_router, b_router, w_gate.astype(BF16), w_up.astype(BF16), w_down.astype(BF16),
                   ln2_g, ln2_b, alpha)


def kernel(x, even_ln1_g, even_ln1_b, even_w_in, even_ret_gn_g, even_w_out, even_ln2_g, even_ln2_b,
           even_w_gate, even_w_up, even_w_down, odd_ln1_g, odd_ln1_b, odd_w_qkv, odd_rel_bias, odd_w_out,
           odd_ln2_g, odd_ln2_b, odd_w_router, odd_b_router, odd_w_gate, odd_w_up, odd_w_down):
    bsz, s, d = x.shape
    depth = even_w_in.shape[0] + odd_w_qkv.shape[0]
    alpha = (2 * depth) ** 0.25
    xt = x.reshape(bsz * s, d)
    for layer in range(depth):
        i = layer // 2
        if layer % 2 == 0:
            xt = _even_layer(xt, bsz, s, alpha, even_ln1_g[i], even_ln1_b[i], even_w_in[i], even_ret_gn_g[i],
                             even_w_out[i], even_ln2_g[i], even_ln2_b[i], even_w_gate[i], even_w_up[i],
                             even_w_down[i])
        else:
            xt = _odd_layer(xt, bsz, s, alpha, odd_ln1_g[i], odd_ln1_b[i], odd_w_qkv[i], odd_rel_bias[i],
                            odd_w_out[i], odd_ln2_g[i], odd_ln2_b[i], odd_w_router[i], odd_b_router[i],
                            odd_w_gate[i], odd_w_up[i], odd_w_down[i])
    return xt.reshape(bsz, s, d)
```

```python
import functools

import jax
import jax.numpy as jnp
from jax import lax
from jax.experimental import pallas as pl
from jax.experimental.pallas import tpu as pltpu

HEAD_DIM = 64
LANES = 128
CHUNK = 64
LEFT_CHUNKS = 8
MAX_REL = 4 * CHUNK
TOP_K = 2
ROPE_BASE = 10000.0
LN_EPS = 1e-5
NEG_BIG = -1e30
VMEM_LIMIT = 56 * 1024 * 1024

BF16 = jnp.bfloat16
F32 = jnp.float32


def _params(*sem):
    return pltpu.CompilerParams(dimension_semantics=sem, vmem_limit_bytes=VMEM_LIMIT)


def _tile(n, target):
    if n <= target:
        return n
    t = target - target % LANES
    while t >= LANES:
        if n % t == 0:
            return t
        t -= LANES
    return n


def _dot(a, b):
    return jnp.dot(a, b, preferred_element_type=F32)


def _dot_nt(a, b):
    return lax.dot_general(a, b, (((1,), (1,)), ((), ())), preferred_element_type=F32)


def _split_dot(a, b_bf16):
    hi = a.astype(BF16)
    lo = (a - hi.astype(F32)).astype(BF16)
    return _dot(hi, b_bf16) + _dot(lo, b_bf16)


def _layer_norm(r, g, b):
    mu = jnp.mean(r, axis=-1, keepdims=True)
    d = r - mu
    var = jnp.mean(d * d, axis=-1, keepdims=True)
    return d * lax.rsqrt(var + LN_EPS) * g + b


def _proj_kernel(x_ref, w_ref, o_ref, xb_ref):
    @pl.when(pl.program_id(1) == 0)
    def _():
        xb_ref[...] = x_ref[...].astype(BF16)

    o_ref[...] = _dot(xb_ref[...], w_ref[...]).astype(o_ref.dtype)


def _proj(x, w, tm_target=1024, tn_target=1792):
    t, k = x.shape
    n = w.shape[1]
    tm, tn = _tile(t, tm_target), _tile(n, tn_target)
    return pl.pallas_call(
        _proj_kernel,
        out_shape=jax.ShapeDtypeStruct((t, n), BF16),
        grid=(t // tm, n // tn),
        in_specs=[pl.BlockSpec((tm, k), lambda i, j: (i, 0)),
                  pl.BlockSpec((k, tn), lambda i, j: (0, j))],
        out_specs=pl.BlockSpec((tm, tn), lambda i, j: (i, j)),
        scratch_shapes=[pltpu.VMEM((tm, k), BF16)],
        compiler_params=_params("parallel", "arbitrary"),
        name="proj",
    )(x, w)


def _out_ln_kernel(alpha, n_a, *refs):
    a_refs = refs[:n_a]
    w_ref, x_ref, g_ref, b_ref = refs[n_a:n_a + 4]
    o_refs = refs[n_a + 4:]
    ka = a_refs[0].shape[1]
    y = _dot(a_refs[0][...], w_ref[0:ka, :])
    for idx in range(1, n_a):
        y = y + _dot(a_refs[idx][...], w_ref[idx * ka:(idx + 1) * ka, :])
    out = _layer_norm(alpha * x_ref[...] + y, g_ref[...], b_ref[...])
    for o_ref in o_refs:
        o_ref[...] = out.astype(o_ref.dtype)


def _out_ln(a_list, w, x, g, b, alpha, also_bf16=False, tm_target=1024):
    t, d = x.shape
    tm = _tile(t, tm_target)
    ka = a_list[0].shape[1]
    row = lambda i: (i, 0)
    fixed = lambda i: (0, 0)
    dtypes = [F32, BF16] if also_bf16 else [F32]
    outs = pl.pallas_call(
        functools.partial(_out_ln_kernel, alpha, len(a_list)),
        out_shape=[jax.ShapeDtypeStruct((t, d), dt) for dt in dtypes],
        grid=(t // tm,),
        in_specs=[pl.BlockSpec((tm, ka), row) for _ in a_list]
        + [pl.BlockSpec(w.shape, fixed), pl.BlockSpec((tm, d), row),
           pl.BlockSpec((1, d), fixed), pl.BlockSpec((1, d), fixed)],
        out_specs=[pl.BlockSpec((tm, d), row) for _ in dtypes],
        compiler_params=_params("parallel"),
        name="out_ln",
    )(*a_list, w, x, g.reshape(1, d), b.reshape(1, d))
    return outs if also_bf16 else outs[0]


def _sb_kernel(q_ref, k_ref, v_ref, o_ref, acc_ref, z_ref, w_ref, carry_ref):
    tq = q_ref.shape[1]
    n_heads = 2 * (q_ref.shape[2] // LANES)
    qi = pl.program_id(2)
    lane = lax.broadcasted_iota(jnp.int32, (1, LANES), 1)
    head0 = lane < HEAD_DIM
    row = lax.broadcasted_iota(jnp.int32, (tq, tq), 0)
    col = lax.broadcasted_iota(jnp.int32, (tq, tq), 1)
    after = (row > col).astype(BF16)
    strictly_causal = col < row

    def pair_lanes(h):
        return slice((h // 2) * LANES, (h // 2 + 1) * LANES)

    def masked_q(h):
        q = q_ref[0, :, pair_lanes(h)] * jnp.asarray(HEAD_DIM ** -0.5, BF16)
        return jnp.where(head0, q, jnp.zeros_like(q)) if h % 2 == 0 else jnp.where(head0, jnp.zeros_like(q), q)

    q_heads = [masked_q(h) for h in range(n_heads)]

    def key_block(ref, kb, h):
        return ref[0, pl.ds(pl.multiple_of(kb * tq, tq), tq), pair_lanes(h)]

    def add_values(kb, h):
        acc_ref[h] += _dot(w_ref[h], key_block(v_ref, kb, h))

    def step(kb, diagonal=False):
        log_betas, log_1ms, suffixes = [], [], []
        kb_next = jnp.maximum(kb - 1, 0)
        for h in range(n_heads):
            z = z_ref[h]
            neg_z = -z
            log_1m = jnp.minimum(neg_z, 0.0) - jnp.log(1.0 + jnp.exp(jnp.minimum(z, neg_z)))
            log_betas.append(z + log_1m)
            if diagonal:
                log_1m = jnp.where(strictly_causal, log_1m, 0.0)
            log_1ms.append(log_1m[:, 0:1])
            suffixes.append(_dot(log_1m.astype(BF16), after))
            z_ref[h] = _dot_nt(q_heads[h], key_block(k_ref, kb_next, h))
            if not diagonal:
                add_values(kb + 1, h)
        for h in range(n_heads):
            carry = carry_ref[h]
            w = jnp.exp(log_betas[h] + suffixes[h] + carry)
            if diagonal:
                w = jnp.where(strictly_causal, w, 0.0)
            w_ref[h] = w.astype(BF16)
            carry_ref[h] = carry + suffixes[h][:, 0:1] + log_1ms[h]

    def later_step(it, _):
        step(qi - 1 - it)
        return 0

    for h in range(n_heads):
        z_ref[h] = _dot_nt(q_heads[h], key_block(k_ref, qi, h))
    acc_ref[...] = jnp.zeros_like(acc_ref)
    carry_ref[...] = jnp.zeros_like(carry_ref)
    step(qi, True)
    lax.fori_loop(0, qi, later_step, 0)
    for h in range(n_heads):
        add_values(0, h)
    for h in range(0, n_heads, 2):
        o_ref[0, :, pair_lanes(h)] = jnp.where(head0, acc_ref[h], acc_ref[h + 1]).astype(o_ref.dtype)


def _stick_breaking(h3, n_heads, col0, tq_target=256):
    bsz, s, _ = h3.shape
    tq = _tile(s, tq_target)
    pairs = n_heads // 2
    per_step = 2 if pairs % 2 == 0 and col0 % 2 == 0 else 1
    groups, lanes, first = pairs // per_step, per_step * LANES, col0 // per_step
    return pl.pallas_call(
        _sb_kernel,
        out_shape=jax.ShapeDtypeStruct((bsz, s, pairs * LANES), BF16),
        grid=(bsz, groups, s // tq),
        in_specs=[pl.BlockSpec((1, tq, lanes), lambda b, p, i: (b, i, first + p)),
                  pl.BlockSpec((1, s, lanes), lambda b, p, i: (b, 0, first + groups + p)),
                  pl.BlockSpec((1, s, lanes), lambda b, p, i: (b, 0, first + 2 * groups + p))],
        out_specs=pl.BlockSpec((1, tq, lanes), lambda b, p, i: (b, i, p)),
        scratch_shapes=[pltpu.VMEM((2 * per_step, tq, LANES), F32), pltpu.VMEM((2 * per_step, tq, tq), F32),
                        pltpu.VMEM((2 * per_step, tq, tq), BF16), pltpu.VMEM((2 * per_step, tq, 1), F32)],
        compiler_params=_params("parallel", "parallel", "arbitrary"),
        name="stick_breaking",
    )(h3, h3, h3)


def _ret_kernel(q_ref, k_ref, v_ref, gate_ref, cos_ref, sin_ref, din_ref, dq_ref, dk_ref,
                dc_ref, gn_ref, o_ref, state_ref):
    @pl.when(pl.program_id(2) == 0)
    def _():
        state_ref[...] = jnp.zeros_like(state_ref)

    lane = lax.broadcasted_iota(jnp.int32, (1, LANES), 1)
    head0 = lane < HEAD_DIM
    first_half = (lane & (HEAD_DIM // 2)) == 0
    row = lax.broadcasted_iota(jnp.int32, (LANES, LANES), 0) < HEAD_DIM
    col = lax.broadcasted_iota(jnp.int32, (LANES, LANES), 1) < HEAD_DIM
    same_head = row == col
    group_mean = jnp.where(same_head, 1.0 / HEAD_DIM, 0.0).astype(BF16)
    cos = cos_ref[...]
    sin = sin_ref[...]

    def rotary(x):
        half = HEAD_DIM // 2
        swapped = jnp.where(first_half, pltpu.roll(x, LANES - half, 1), pltpu.roll(x, half, 1))
        return x * cos + swapped * sin

    pairs = range(q_ref.shape[2] // LANES)
    lanes = lambda p: slice(p * LANES, (p + 1) * LANES)
    qb, kb, v, scores, cross, outer = [], [], [], [], [], []
    for p in pairs:
        q = rotary(q_ref[0, :, lanes(p)].astype(F32)) * (HEAD_DIM ** -0.5)
        k = rotary(k_ref[0, :, lanes(p)].astype(F32))
        v.append(v_ref[0, :, lanes(p)])
        qb.append(q.astype(BF16))
        kb.append(k.astype(BF16))
        scores.append([_dot_nt(jnp.where(mask, qb[p], jnp.zeros_like(qb[p])), kb[p])
                       for mask in (head0, jnp.logical_not(head0))])
        cross.append(_dot(qb[p], state_ref[p].astype(BF16)))
        outer.append(_dot((k * dk_ref[p]).T.astype(BF16), v[p]))
    intra = [[_dot((scores[p][h] * din_ref[2 * p + h]).astype(BF16), v[p]) for h in range(2)] for p in pairs]
    for p in pairs:
        state_ref[p] = state_ref[p] * dc_ref[p] + jnp.where(same_head, outer[p], 0.0)
        o = jnp.where(head0, intra[p][0], intra[p][1]) + cross[p] * dq_ref[p]
        mu = _split_dot(o, group_mean)
        d = o - mu
        var = _split_dot(d * d, group_mean)
        y = d * lax.rsqrt(var + LN_EPS) * gn_ref[:, lanes(p)]
        g = gate_ref[0, :, lanes(p)].astype(F32)
        o_ref[0, :, lanes(p)] = (g / (1.0 + jnp.exp(-g)) * y).astype(o_ref.dtype)


def _retention(h3, gn_g, n_heads, col0, chunk_target=256):
    bsz, s, _ = h3.shape
    c = _tile(s, chunk_target)
    pairs = n_heads // 2
    half = HEAD_DIM // 2

    pos = jnp.arange(s, dtype=F32)
    inv_freq = ROPE_BASE ** (-jnp.arange(half, dtype=F32) / half)
    ang = pos[:, None] * inv_freq[None, :]
    cos = jnp.tile(jnp.cos(ang), (1, LANES // half))
    sin = jnp.tile(jnp.concatenate([-jnp.sin(ang), jnp.sin(ang)], axis=1), (1, LANES // HEAD_DIM))

    gamma = 1.0 - 2.0 ** (-5.0 - jnp.arange(n_heads, dtype=F32))
    log_g = jnp.log(gamma)
    i = jnp.arange(c, dtype=F32)
    diff = i[:, None] - i[None, :]
    d_in = jnp.where(diff >= 0, jnp.exp(log_g[:, None, None] * jnp.maximum(diff, 0.0)), 0.0)
    per_lane = lambda t: jnp.repeat(t.reshape(pairs, 2, -1), HEAD_DIM, axis=1)
    d_q = jnp.swapaxes(per_lane(jnp.exp(log_g[:, None] * (i + 1.0))), 1, 2)
    d_k = jnp.swapaxes(per_lane(jnp.exp(log_g[:, None] * (c - 1.0 - i))), 1, 2)
    d_c = jnp.swapaxes(per_lane(jnp.exp(log_g * c)[:, None]), 1, 2)

    per_step = 2 if pairs % 2 == 0 and col0 % 2 == 0 else 1
    groups, lanes, first = pairs // per_step, per_step * LANES, col0 // per_step
    blk = lambda off: pl.BlockSpec((1, c, lanes), lambda b, p, t: (b, t, first + off * groups + p))
    return pl.pallas_call(
        _ret_kernel,
        out_shape=jax.ShapeDtypeStruct((bsz, s, pairs * LANES), BF16),
        grid=(bsz, groups, s // c),
        in_specs=[blk(0), blk(1), blk(2), blk(3),
                  pl.BlockSpec((c, LANES), lambda b, p, t: (t, 0)),
                  pl.BlockSpec((c, LANES), lambda b, p, t: (t, 0)),
                  pl.BlockSpec((2 * per_step, c, c), lambda b, p, t: (p, 0, 0)),
                  pl.BlockSpec((per_step, c, LANES), lambda b, p, t: (p, 0, 0)),
                  pl.BlockSpec((per_step, c, LANES), lambda b, p, t: (p, 0, 0)),
                  pl.BlockSpec((per_step, 1, LANES), lambda b, p, t: (p, 0, 0)),
                  pl.BlockSpec((1, lanes), lambda b, p, t: (0, p))],
        out_specs=pl.BlockSpec((1, c, lanes), lambda b, p, t: (b, t, p)),
        scratch_shapes=[pltpu.VMEM((per_step, LANES, LANES), F32)],
        compiler_params=_params("parallel", "parallel", "arbitrary"),
        name="retention",
    )(h3, h3, h3, h3, cos, sin, d_in, d_q, d_k, d_c, gn_g.reshape(1, -1))


def _chunk_attn_kernel(q_ref, k_ref, v_ref, bias_ref, o_ref):
    tq = q_ref.shape[1]
    t0 = pl.program_id(2) * tq
    n_pairs = q_ref.shape[2] // LANES
    lane = lax.broadcasted_iota(jnp.int32, (1, LANES), 1)
    head0 = lane < HEAD_DIM

    width = bias_ref.shape[3]
    start = pl.multiple_of(jnp.maximum(t0 - LEFT_CHUNKS * CHUNK, 0), tq)

    def pair_slice(ref, pair, rows):
        return ref[0, rows, pair * LANES:(pair + 1) * LANES]

    logits = []
    for pair in range(n_pairs):
        q = pair_slice(q_ref, pair, slice(None)) * jnp.asarray(HEAD_DIM ** -0.5, BF16)
        k = pair_slice(k_ref, pair, pl.ds(start, width))
        logits.append(_dot_nt(jnp.where(head0, q, jnp.zeros_like(q)), k))
        logits.append(_dot_nt(jnp.where(head0, jnp.zeros_like(q), q), k))
    for pair in range(n_pairs):
        probs, denoms = [], []
        for h in range(2):
            head = 2 * pair + h
            s = logits[head] + bias_ref[0, 0, head * tq:(head + 1) * tq, :]
            p = jnp.exp(s - jnp.max(s, axis=-1, keepdims=True))
            denoms.append(jnp.sum(p, axis=-1, keepdims=True))
            probs.append(p.astype(BF16))
        v = pair_slice(v_ref, pair, pl.ds(start, width))
        outs = [_dot(probs[h], v) / denoms[h] for h in range(2)]
        o_ref[0, :, pair * LANES:(pair + 1) * LANES] = jnp.where(head0, outs[0], outs[1]).astype(o_ref.dtype)


def _window_bias(rel_bias, tq, width):
    n_heads = rel_bias.shape[0]
    left = LEFT_CHUNKS * CHUNK
    rel_min, rel_max = left - (width - 1), left + tq - 1
    assert rel_min >= -MAX_REL and rel_max >= MAX_REL
    by_rel = jnp.concatenate(
        [rel_bias[:, rel_min + MAX_REL:], jnp.broadcast_to(rel_bias[:, -1:], (n_heads, rel_max - MAX_REL))], axis=1)
    m = rel_max - rel_min + 1
    rev = jnp.concatenate([by_rel[:, ::-1], jnp.zeros((n_heads, 1), by_rel.dtype)], axis=1)
    rows = jnp.tile(rev, (1, tq))[:, :tq * m].reshape(n_heads, tq, m)
    bias = rows[:, :, tq - 1:tq - 1 + width].astype(F32)
    q_chunk = jnp.arange(tq)[:, None] // CHUNK + LEFT_CHUNKS
    k_chunk = jnp.arange(width)[None, :] // CHUNK
    visible = (k_chunk >= q_chunk - LEFT_CHUNKS) & (k_chunk <= q_chunk)
    return jnp.where(visible[None], bias, NEG_BIG)


def _chunk_attention(qkv3, rel_bias, n_heads, tq=256):
    bsz, s, _ = qkv3.shape
    pairs = n_heads // 2
    n_shift = (LEFT_CHUNKS * CHUNK) // tq
    width = (n_shift + 1) * tq
    assert s >= width
    bias = _window_bias(rel_bias, tq, width)
    per_step = next(n for n in (4, 2, 1) if pairs % n == 0)
    groups, heads = pairs // per_step, 2 * per_step
    lanes = per_step * LANES
    bias = jnp.stack([jnp.pad(bias[:, :, v * tq:], ((0, 0), (0, 0), (0, v * tq)), constant_values=NEG_BIG)
                      for v in range(n_shift + 1)], axis=1).reshape(groups, heads, n_shift + 1, tq, width)
    bias = bias.transpose(0, 2, 1, 3, 4).reshape(groups, n_shift + 1, heads * tq, width)

    return pl.pallas_call(
        _chunk_attn_kernel,
        out_shape=jax.ShapeDtypeStruct((bsz, s, pairs * LANES), BF16),
        grid=(bsz, groups, s // tq),
        in_specs=[pl.BlockSpec((1, tq, lanes), lambda b, p, t: (b, t, p)),
                  pl.BlockSpec((1, s, lanes), lambda b, p, t: (b, 0, groups + p)),
                  pl.BlockSpec((1, s, lanes), lambda b, p, t: (b, 0, 2 * groups + p)),
                  pl.BlockSpec((1, 1, heads * tq, width), lambda b, p, t: (p, jnp.maximum(n_shift - t, 0), 0, 0))],
        out_specs=pl.BlockSpec((1, tq, lanes), lambda b, p, t: (b, t, p)),
        compiler_params=_params("parallel", "parallel", "arbitrary"),
        name="chunk_attention",
    )(qkv3, qkv3, qkv3, bias)


def _swiglu_ln_kernel(alpha, x_ref, wg_ref, wu_ref, wd_ref, g_ref, b_ref, o_ref, xb_ref, acc_ref):
    f = pl.program_id(1)

    @pl.when(f == 0)
    def _():
        xb_ref[...] = x_ref[...].astype(BF16)
        acc_ref[...] = jnp.zeros_like(acc_ref)

    xb = xb_ref[...]
    gate = _dot(xb, wg_ref[...])
    up = _dot(xb, wu_ref[...])
    hidden = (gate / (1.0 + jnp.exp(-gate)) * up).astype(BF16)
    acc_ref[...] += _dot(hidden, wd_ref[...])

    @pl.when(f == pl.num_programs(1) - 1)
    def _():
        o_ref[...] = _layer_norm(alpha * x_ref[...] + acc_ref[...], g_ref[...], b_ref[...])


def _swiglu_ln(x, wg, wu, wd, g, b, alpha, tm_target=512, tf_target=2816):
    t, d = x.shape
    ff = wg.shape[1]
    tm, tf = _tile(t, tm_target), _tile(ff, tf_target)
    once = dict(pipeline_mode=pl.Buffered(1)) if tf == ff else {}
    return pl.pallas_call(
        functools.partial(_swiglu_ln_kernel, alpha),
        out_shape=jax.ShapeDtypeStruct((t, d), F32),
        grid=(t // tm, ff // tf),
        in_specs=[pl.BlockSpec((tm, d), lambda i, f: (i, 0)),
                  pl.BlockSpec((d, tf), lambda i, f: (0, f), **once),
                  pl.BlockSpec((d, tf), lambda i, f: (0, f), **once),
                  pl.BlockSpec((tf, d), lambda i, f: (f, 0), **once),
                  pl.BlockSpec((1, d), lambda i, f: (0, 0)),
                  pl.BlockSpec((1, d), lambda i, f: (0, 0))],
        out_specs=pl.BlockSpec((tm, d), lambda i, f: (i, 0)),
        scratch_shapes=[pltpu.VMEM((tm, d), BF16), pltpu.VMEM((tm, d), F32)],
        compiler_params=_params("parallel", "arbitrary"),
        name="swiglu_ln",
    )(x, wg, wu, wd, g.reshape(1, d), b.reshape(1, d))


ROUTE_E1, ROUTE_E2, ROUTE_RANK1, ROUTE_RANK2 = range(4)
ROUTE_FIELDS = 8
MOE_TILE = 512
MOE_WINDOW = 256
ROW_ALIGN = 16
PAIR_LIVE, PAIR_FIRST, PAIR_LAST, PAIR_NARROW = 1, 2, 4, 8


def _route_kernel(n_experts, x_ref, w_ref, b_ref, col_ref, row_ref, gate_ref, cnt_ref, run_ref):
    @pl.when(pl.program_id(0) == 0)
    def _():
        run_ref[...] = jnp.zeros_like(run_ref)

    x = x_ref[...]
    w = w_ref[...]
    tm = x.shape[0]
    x_hi = x.astype(BF16)
    x_lo = (x - x_hi.astype(F32)).astype(BF16)
    w_hi = w.astype(BF16)
    w_lo = (w - w_hi.astype(F32)).astype(BF16)
    logits = _dot(x_hi, w_hi) + (_dot(x_hi, w_lo) + _dot(x_lo, w_hi)) + b_ref[...]
    lane = lax.broadcasted_iota(jnp.int32, logits.shape, 1).astype(F32)
    neg_inf = jnp.asarray(-jnp.inf, F32)
    logits = jnp.where(lane < n_experts, logits, neg_inf)
    top1 = jnp.max(logits, axis=-1, keepdims=True)
    idx1 = jnp.min(jnp.where(logits == top1, lane, float(LANES)), axis=-1, keepdims=True)
    rest = jnp.where(lane == idx1, neg_inf, logits)
    top2 = jnp.max(rest, axis=-1, keepdims=True)
    idx2 = jnp.min(jnp.where(rest == top2, lane, float(LANES)), axis=-1, keepdims=True)
    e2 = jnp.exp(top2 - top1)
    denom = 1.0 + e2

    chosen = jnp.where(jnp.logical_or(lane == idx1, lane == idx2), 1.0, 0.0)
    r = lax.broadcasted_iota(jnp.int32, (tm, tm), 0)
    c = lax.broadcasted_iota(jnp.int32, (tm, tm), 1)
    earlier = (c < r).astype(BF16)
    before = _dot(earlier, chosen.astype(BF16)) + run_ref[...]
    rank1 = jnp.sum(jnp.where(lane == idx1, before, 0.0), axis=-1, keepdims=True)
    rank2 = jnp.sum(jnp.where(lane == idx2, before, 0.0), axis=-1, keepdims=True)
    total = jnp.sum(chosen, axis=0, keepdims=True)
    cnt_ref[0] = total
    run_ref[...] += total

    fields = {ROUTE_E1: idx1, ROUTE_E2: idx2, ROUTE_RANK1: rank1, ROUTE_RANK2: rank2}
    meta = jnp.zeros_like(logits)
    for field, value in fields.items():
        meta = jnp.where(lane == field, value, meta)
    col_ref[...] = meta
    row_ref[0] = meta.T[0:ROUTE_FIELDS, :]
    gates = jnp.where(lane == idx1, 1.0 / denom, jnp.where(lane == idx2, e2 / denom, 0.0))
    gates_hi = gates.astype(BF16)
    gate_ref[:, 0:LANES] = gates_hi
    gate_ref[:, LANES:2 * LANES] = (gates - gates_hi.astype(F32)).astype(BF16)


def _route(x, w_router, b_router, tm):
    t, d = x.shape
    n_experts = w_router.shape[1]
    nb = t // tm
    w = jnp.pad(w_router.astype(F32), ((0, 0), (0, LANES - n_experts)))
    b = jnp.pad(b_router.astype(F32), (0, LANES - n_experts)).reshape(1, LANES)
    return pl.pallas_call(
        functools.partial(_route_kernel, n_experts),
        out_shape=[jax.ShapeDtypeStruct((t, LANES), F32),
                   jax.ShapeDtypeStruct((nb, ROUTE_FIELDS, tm), F32),
                   jax.ShapeDtypeStruct((t, 2 * LANES), BF16),
                   jax.ShapeDtypeStruct((nb, 1, LANES), F32)],
        grid=(nb,),
        in_specs=[pl.BlockSpec((tm, d), lambda i: (i, 0)),
                  pl.BlockSpec((d, LANES), lambda i: (0, 0)),
                  pl.BlockSpec((1, LANES), lambda i: (0, 0))],
        out_specs=[pl.BlockSpec((tm, LANES), lambda i: (i, 0)),
                   pl.BlockSpec((1, ROUTE_FIELDS, tm), lambda i: (i, 0, 0)),
                   pl.BlockSpec((tm, 2 * LANES), lambda i: (i, 0)),
                   pl.BlockSpec((1, 1, LANES), lambda i: (i, 0, 0))],
        scratch_shapes=[pltpu.VMEM((1, LANES), F32)],
        compiler_params=_params("arbitrary"),
        name="route",
    )(x, w, b)


def _moe_plan(cnt, n_experts, tile, n_tiles):
    nb = cnt.shape[0]
    counts = cnt[:, 0, :n_experts].astype(jnp.int32)
    cum = jnp.cumsum(counts, axis=0) - counts
    total = jnp.sum(counts, axis=0)
    tiles_e = (total + tile - 1) // tile
    tile_end = jnp.cumsum(tiles_e)
    row_off = (tile_end - tiles_e) * tile
    start = row_off[None, :] + cum
    j0 = start // tile
    j1 = (start + counts - 1) // tile
    has = counts > 0
    tiles = jnp.stack([j0, j0 + 1], axis=-1)
    valid = jnp.stack([has, has & (j1 > j0)], axis=-1)
    lo = jnp.clip(start[..., None] - tiles * tile, 0, tile)
    hi = jnp.clip((start + counts)[..., None] - tiles * tile, 0, tile)
    win = min(MOE_WINDOW, tile)
    window = jnp.minimum(lo // ROW_ALIGN * ROW_ALIGN, tile - win)
    narrow = hi <= window + win

    n_pairs = nb * n_experts + n_tiles
    slot = jnp.arange(n_pairs)
    block_id = jnp.broadcast_to(jnp.arange(nb)[:, None, None], tiles.shape)

    def listed(order, group_key):
        flat = lambda a: a.transpose(order).reshape(-1)
        n_live = jnp.sum(valid)
        idx = jnp.nonzero(flat(valid), size=n_pairs, fill_value=0)[0]
        idx = jnp.where(slot < n_live, idx, idx[n_live - 1])
        fields = {name: flat(a)[idx].astype(jnp.int32)
                  for name, a in (("tile", tiles), ("block", block_id), ("window", window), ("narrow", narrow))}
        key = fields[group_key]
        live = slot < n_live
        first = live & (key != jnp.concatenate([jnp.full((1,), -1, key.dtype), key[:-1]]))
        last = live & ((slot == n_live - 1) | (key != jnp.concatenate([key[1:], key[-1:]])))
        fields["flag"] = (PAIR_LIVE * live + PAIR_FIRST * first + PAIR_LAST * last
                          + PAIR_NARROW * fields.pop("narrow")).astype(jnp.int32)
        return fields

    by_expert = listed((1, 0, 2), "tile")
    by_block = listed((0, 1, 2), "block")
    n_used = tile_end[-1:].astype(jnp.int32)
    tile_expert = jnp.minimum(jnp.searchsorted(tile_end, jnp.arange(n_tiles), side="right"),
                              n_experts - 1).astype(jnp.int32)
    return dict(by_expert=by_expert, by_block=by_block, row_off=row_off.astype(jnp.int32),
                n_used=n_used, tile_expert=tile_expert)


def _with_positions(meta, row_off, field_axis):
    def take(first):
        return lax.slice_in_dim(meta, first, first + TOP_K, axis=field_axis)

    expert, offset = take(ROUTE_E1), jnp.zeros_like(take(ROUTE_E1))
    for e in range(row_off.shape[0]):
        offset = jnp.where(expert == float(e), row_off[e].astype(F32), offset)
    n_fields = meta.shape[field_axis]
    return jnp.concatenate([take(ROUTE_E1), take(ROUTE_RANK1) + offset,
                            lax.slice_in_dim(meta, ROUTE_RANK1 + TOP_K, n_fields, axis=field_axis)], axis=field_axis)


def _gather_kernel(tile_ref, block_ref, window_ref, flag_ref, row_ref, gate_ref, x_ref, xs_ref, gs_ref, acc_ref):
    pair = pl.program_id(0)
    flag = flag_ref[pair]
    narrow = (flag & PAIR_NARROW) != 0
    tile, tm = xs_ref.shape[0], x_ref.shape[0]

    @pl.when((flag & PAIR_LIVE) != 0)
    def _():
        @pl.when((flag & PAIR_FIRST) != 0)
        def _():
            acc_ref[...] = jnp.zeros_like(acc_ref)
            gs_ref[...] = jnp.zeros_like(gs_ref)

        pos1 = row_ref[0, ROUTE_RANK1:ROUTE_RANK1 + 1, :].astype(jnp.int32)
        pos2 = row_ref[0, ROUTE_RANK2:ROUTE_RANK2 + 1, :].astype(jnp.int32)

        def place(row0, n_rows):
            rows = tile_ref[pair] * tile + row0 + lax.broadcasted_iota(jnp.int32, (n_rows, tm), 0)
            onehot = jnp.where(jnp.logical_or(rows == pos1, rows == pos2), 1.0, 0.0).astype(BF16)
            acc_ref[pl.ds(row0, n_rows), :] += _dot(onehot, x_ref[...])
            gs_ref[pl.ds(row0, n_rows), :] += _dot(onehot, gate_ref[...])

        @pl.when(narrow)
        def _():
            place(pl.multiple_of(window_ref[pair], ROW_ALIGN), min(MOE_WINDOW, tile))

        @pl.when(jnp.logical_not(narrow))
        def _():
            place(0, tile)

        @pl.when((flag & PAIR_LAST) != 0)
        def _():
            xs_ref[...] = acc_ref[...].astype(BF16)


def _moe_gather(plan, pos_row, gates, xb, tile, n_tiles):
    t, d = xb.shape
    _, _, tm = pos_row.shape
    pairs = plan["by_expert"]
    return pl.pallas_call(
        _gather_kernel,
        out_shape=[jax.ShapeDtypeStruct((n_tiles * tile, d), BF16),
                   jax.ShapeDtypeStruct((n_tiles * tile, 2 * LANES), F32)],
        grid_spec=pltpu.PrefetchScalarGridSpec(
            num_scalar_prefetch=4,
            grid=(pairs["tile"].shape[0],),
            in_specs=[pl.BlockSpec((1, ROUTE_FIELDS, tm), lambda p, tiles, blocks, *_: (blocks[p], 0, 0)),
                      pl.BlockSpec((tm, 2 * LANES), lambda p, tiles, blocks, *_: (blocks[p], 0)),
                      pl.BlockSpec((tm, d), lambda p, tiles, blocks, *_: (blocks[p], 0))],
            out_specs=[pl.BlockSpec((tile, d), lambda p, tiles, *_: (tiles[p], 0)),
                       pl.BlockSpec((tile, 2 * LANES), lambda p, tiles, *_: (tiles[p], 0))],
            scratch_shapes=[pltpu.VMEM((tile, d), F32)]),
        compiler_params=_params("arbitrary"),
        name="moe_gather",
    )(pairs["tile"], pairs["block"], pairs["window"], pairs["flag"], pos_row, gates, xb)


def _moe_ffn_kernel(te_ref, used_ref, xs_ref, gs_ref, wg_ref, wu_ref, wd_ref, o_ref, acc_ref):
    f = pl.program_id(1)
    tile_id = pl.program_id(0)

    @pl.when(tile_id < used_ref[0])
    def _():
        @pl.when(f == 0)
        def _():
            acc_ref[...] = jnp.zeros_like(acc_ref)

        xs = xs_ref[...]
        gate = _dot(xs, wg_ref[0])
        up = _dot(xs, wu_ref[0])
        hidden = (gate / (1.0 + jnp.exp(-gate)) * up).astype(BF16)
        acc_ref[...] += _dot(hidden, wd_ref[0])

        @pl.when(f == pl.num_programs(1) - 1)
        def _():
            by_expert = gs_ref[:, 0:LANES] + gs_ref[:, LANES:2 * LANES]
            lane = lax.broadcasted_iota(jnp.int32, by_expert.shape, 1)
            row_gate = jnp.sum(jnp.where(lane == te_ref[tile_id], by_expert, 0.0), axis=1, keepdims=True)
            o_ref[...] = (acc_ref[...] * row_gate).astype(o_ref.dtype)


def _moe_ffn(plan, xs, gs, wg, wu, wd, tile, tf_target=1792):
    rows, d = xs.shape
    ff = wg.shape[2]
    tf = _tile(ff, tf_target)
    nf = ff // tf

    def tile_of(i, used):
        return jnp.minimum(i, used[0] - 1)

    def chunk_of(i, f, used):
        return jnp.where(i < used[0], f, nf - 1)

    return pl.pallas_call(
        _moe_ffn_kernel,
        out_shape=jax.ShapeDtypeStruct((rows, d), BF16),
        grid_spec=pltpu.PrefetchScalarGridSpec(
            num_scalar_prefetch=2,
            grid=(rows // tile, nf),
            in_specs=[pl.BlockSpec((tile, d), lambda i, f, te, used: (tile_of(i, used), 0)),
                      pl.BlockSpec((tile, 2 * LANES), lambda i, f, te, used: (tile_of(i, used), 0)),
                      pl.BlockSpec((1, d, tf), lambda i, f, te, used: (te[tile_of(i, used)], 0, chunk_of(i, f, used))),
                      pl.BlockSpec((1, d, tf), lambda i, f, te, used: (te[tile_of(i, used)], 0, chunk_of(i, f, used))),
                      pl.BlockSpec((1, tf, d), lambda i, f, te, used: (te[tile_of(i, used)], chunk_of(i, f, used), 0))],
            out_specs=pl.BlockSpec((tile, d), lambda i, f, te, used: (tile_of(i, used), 0)),
            scratch_shapes=[pltpu.VMEM((tile, d), F32)]),
        compiler_params=_params("arbitrary", "arbitrary"),
        name="moe_ffn",
    )(plan["tile_expert"], plan["n_used"], xs, gs, wg, wu, wd)


def _combine_ln_kernel(alpha, tile_ref, block_ref, window_ref, flag_ref, col_ref, os_ref, x_ref, g_ref, b_ref, o_ref,
                       acc_ref):
    pair = pl.program_id(0)
    flag = flag_ref[pair]
    narrow = (flag & PAIR_NARROW) != 0
    tm, tile = x_ref.shape[0], os_ref.shape[0]

    @pl.when((flag & PAIR_LIVE) != 0)
    def _():
        @pl.when((flag & PAIR_FIRST) != 0)
        def _():
            acc_ref[...] = jnp.zeros_like(acc_ref)

        pos1 = col_ref[:, ROUTE_RANK1:ROUTE_RANK1 + 1].astype(jnp.int32)
        pos2 = col_ref[:, ROUTE_RANK2:ROUTE_RANK2 + 1].astype(jnp.int32)

        def collect(row0, n_rows):
            cols = tile_ref[pair] * tile + row0 + lax.broadcasted_iota(jnp.int32, (tm, n_rows), 1)
            onehot = jnp.where(jnp.logical_or(cols == pos1, cols == pos2), 1.0, 0.0).astype(BF16)
            acc_ref[...] += _dot(onehot, os_ref[pl.ds(row0, n_rows), :])

        @pl.when(narrow)
        def _():
            collect(pl.multiple_of(window_ref[pair], ROW_ALIGN), min(MOE_WINDOW, tile))

        @pl.when(jnp.logical_not(narrow))
        def _():
            collect(0, tile)

    @pl.when((flag & PAIR_LAST) != 0)
    def _():
        o_ref[...] = _layer_norm(alpha * x_ref[...] + acc_ref[...], g_ref[...], b_ref[...])


def _moe_combine_ln(plan, pos_col, o_sorted, x, g, b, alpha, tile, tm):
    t, d = x.shape
    pairs = plan["by_block"]
    return pl.pallas_call(
        functools.partial(_combine_ln_kernel, alpha),
        out_shape=jax.ShapeDtypeStruct((t, d), F32),
        grid_spec=pltpu.PrefetchScalarGridSpec(
            num_scalar_prefetch=4,
            grid=(pairs["tile"].shape[0],),
            in_specs=[pl.BlockSpec((tm, LANES), lambda p, tiles, blocks, *_: (blocks[p], 0)),
                      pl.BlockSpec((tile, d), lambda p, tiles, *_: (tiles[p], 0)),
                      pl.BlockSpec((tm, d), lambda p, tiles, blocks, *_: (blocks[p], 0)),
                      pl.BlockSpec((1, d), lambda p, *_: (0, 0)),
                      pl.BlockSpec((1, d), lambda p, *_: (0, 0))],
            out_specs=pl.BlockSpec((tm, d), lambda p, tiles, blocks, *_: (blocks[p], 0)),
            scratch_shapes=[pltpu.VMEM((tm, d), F32)]),
        compiler_params=_params("arbitrary"),
        name="moe_combine_ln",
    )(pairs["tile"], pairs["block"], pairs["window"], pairs["flag"], pos_col, o_sorted, x,
      g.reshape(1, d), b.reshape(1, d))


def _moe_ln(x, xb, w_router, b_router, wg, wu, wd, g, b, alpha):
    t, _ = x.shape
    n_experts = wg.shape[0]
    tile = _tile(t, MOE_TILE)
    n_tiles = TOP_K * t // tile + n_experts
    meta_col, meta_row, gates, cnt = _route(x, w_router, b_router, tile)
    plan = _moe_plan(cnt, n_experts, tile, n_tiles)
    pos_row = _with_positions(meta_row, plan["row_off"], 1)
    pos_col = _with_positions(meta_col, plan["row_off"], 1)
    xs, gs = _moe_gather(plan, pos_row, gates, xb, tile, n_tiles)
    o_sorted = _moe_ffn(plan, xs, gs, wg, wu, wd, tile)
    return _moe_combine_ln(plan, pos_col, o_sorted, x, g, b, alpha, tile, tile)


def _even_layer(x, bsz, s, alpha, ln1_g, ln1_b, w_in, ret_gn_g, w_out, ln2_g, ln2_b, w_gate, w_up, w_down):
    d = x.shape[1]
    n_heads = d // HEAD_DIM
    n_sb = n_heads // 2
    n_ret = n_heads - n_sb
    h = _proj(x, w_in.astype(BF16))
    h3 = h.reshape(bsz, s, h.shape[1])
    o_sb = _stick_breaking(h3, n_sb, 0)
    o_ret = _retention(h3, ret_gn_g, n_ret, 3 * n_sb * HEAD_DIM // LANES)
    x = _out_ln([o_sb.reshape(bsz * s, -1), o_ret.reshape(bsz * s, -1)], w_out.astype(BF16),
                x, ln1_g, ln1_b, alpha)
    return _swiglu_ln(x, w_gate.astype(BF16), w_up.astype(BF16), w_down.astype(BF16), ln2_g, ln2_b, alpha)


def _odd_layer(x, bsz, s, alpha, ln1_g, ln1_b, w_qkv, rel_bias, w_out, ln2_g, ln2_b, w_router, b_router,
               w_gate, w_up, w_down):
    d = x.shape[1]
    qkv = _proj(x, w_qkv.astype(BF16))
    o = _chunk_attention(qkv.reshape(bsz, s, 3 * d), rel_bias, d // HEAD_DIM)
    x, xb = _out_ln([o.reshape(bsz * s, d)], w_out.astype(BF16), x, ln1_g, ln1_b, alpha, also_bf16=True)
    return _moe_ln(x, xb, w_router, b_router, w_gate.astype(BF16), w_up.astype(BF16), w_down.astype(BF16),
                   ln2_g, ln2_b, alpha)


def kernel(x, even_ln1_g, even_ln1_b, even_w_in, even_ret_gn_g, even_w_out, even_ln2_g, even_ln2_b,
           even_w_gate, even_w_up, even_w_down, odd_ln1_g, odd_ln1_b, odd_w_qkv, odd_rel_bias, odd_w_out,
           odd_ln2_g, odd_ln2_b, odd_w_router, odd_b_router, odd_w_gate, odd_w_up, odd_w_down):
    bsz, s, d = x.shape
    depth = even_w_in.shape[0] + odd_w_qkv.shape[0]
    alpha = (2 * depth) ** 0.25
    xt = x.reshape(bsz * s, d)
    for layer in range(depth):
        i = layer // 2
        if layer % 2 == 0:
            xt = _even_layer(xt, bsz, s, alpha, even_ln1_g[i], even_ln1_b[i], even_w_in[i], even_ret_gn_g[i],
                             even_w_out[i], even_ln2_g[i], even_ln2_b[i], even_w_gate[i], even_w_up[i],
                             even_w_down[i])
        else:
            xt = _odd_layer(xt, bsz, s, alpha, odd_ln1_g[i], odd_ln1_b[i], odd_w_qkv[i], odd_rel_bias[i],
                            odd_w_out[i], odd_ln2_g[i], odd_ln2_b[i], odd_w_router[i], odd_b_router[i],
                            odd_w_gate[i], odd_w_up[i], odd_w_down[i])
    return xt.reshape(bsz, s, d)
```

```python
import functools

import jax
import jax.numpy as jnp
from jax import lax
from jax.experimental import pallas as pl
from jax.experimental.pallas import tpu as pltpu

HEAD_DIM = 64
LANES = 128
CHUNK = 64
LEFT_CHUNKS = 8
MAX_REL = 4 * CHUNK
TOP_K = 2
ROPE_BASE = 10000.0
LN_EPS = 1e-5
NEG_BIG = -1e30
VMEM_LIMIT = 56 * 1024 * 1024

BF16 = jnp.bfloat16
F32 = jnp.float32


def _params(*sem):
    return pltpu.CompilerParams(dimension_semantics=sem, vmem_limit_bytes=VMEM_LIMIT)


def _tile(n, target):
    if n <= target:
        return n
    t = target - target % LANES
    while t >= LANES:
        if n % t == 0:
            return t
        t -= LANES
    return n


def _dot(a, b):
    return jnp.dot(a, b, preferred_element_type=F32)


def _dot_nt(a, b):
    return lax.dot_general(a, b, (((1,), (1,)), ((), ())), preferred_element_type=F32)


def _split_dot(a, b_bf16):
    hi = a.astype(BF16)
    lo = (a - hi.astype(F32)).astype(BF16)
    return _dot(hi, b_bf16) + _dot(lo, b_bf16)


def _layer_norm(r, g, b):
    mu = jnp.mean(r, axis=-1, keepdims=True)
    d = r - mu
    var = jnp.mean(d * d, axis=-1, keepdims=True)
    return d * lax.rsqrt(var + LN_EPS) * g + b


def _proj_kernel(x_ref, w_ref, o_ref, xb_ref):
    @pl.when(pl.program_id(1) == 0)
    def _():
        xb_ref[...] = x_ref[...].astype(BF16)

    o_ref[...] = _dot(xb_ref[...], w_ref[...]).astype(o_ref.dtype)


def _proj(x, w, tm_target=1024, tn_target=1792):
    t, k = x.shape
    n = w.shape[1]
    tm, tn = _tile(t, tm_target), _tile(n, tn_target)
    return pl.pallas_call(
        _proj_kernel,
        out_shape=jax.ShapeDtypeStruct((t, n), BF16),
        grid=(t // tm, n // tn),
        in_specs=[pl.BlockSpec((tm, k), lambda i, j: (i, 0)),
                  pl.BlockSpec((k, tn), lambda i, j: (0, j))],
        out_specs=pl.BlockSpec((tm, tn), lambda i, j: (i, j)),
        scratch_shapes=[pltpu.VMEM((tm, k), BF16)],
        compiler_params=_params("parallel", "arbitrary"),
        name="proj",
    )(x, w)


def _out_ln_kernel(alpha, n_a, *refs):
    a_refs = refs[:n_a]
    w_ref, x_ref, g_ref, b_ref = refs[n_a:n_a + 4]
    o_refs = refs[n_a + 4:]
    ka = a_refs[0].shape[1]
    y = _dot(a_refs[0][...], w_ref[0:ka, :])
    for idx in range(1, n_a):
        y = y + _dot(a_refs[idx][...], w_ref[idx * ka:(idx + 1) * ka, :])
    out = _layer_norm(alpha * x_ref[...] + y, g_ref[...], b_ref[...])
    for o_ref in o_refs:
        o_ref[...] = out.astype(o_ref.dtype)


def _out_ln(a_list, w, x, g, b, alpha, also_bf16=False, tm_target=1024):
    t, d = x.shape
    tm = _tile(t, tm_target)
    ka = a_list[0].shape[1]
    row = lambda i: (i, 0)
    fixed = lambda i: (0, 0)
    dtypes = [F32, BF16] if also_bf16 else [F32]
    outs = pl.pallas_call(
        functools.partial(_out_ln_kernel, alpha, len(a_list)),
        out_shape=[jax.ShapeDtypeStruct((t, d), dt) for dt in dtypes],
        grid=(t // tm,),
        in_specs=[pl.BlockSpec((tm, ka), row) for _ in a_list]
        + [pl.BlockSpec(w.shape, fixed), pl.BlockSpec((tm, d), row),
           pl.BlockSpec((1, d), fixed), pl.BlockSpec((1, d), fixed)],
        out_specs=[pl.BlockSpec((tm, d), row) for _ in dtypes],
        compiler_params=_params("parallel"),
        name="out_ln",
    )(*a_list, w, x, g.reshape(1, d), b.reshape(1, d))
    return outs if also_bf16 else outs[0]


def _sb_kernel(q_ref, k_ref, v_ref, o_ref, acc_ref, z_ref, w_ref, carry_ref):
    tq = q_ref.shape[1]
    n_heads = 2 * (q_ref.shape[2] // LANES)
    qi = pl.program_id(2)
    lane = lax.broadcasted_iota(jnp.int32, (1, LANES), 1)
    head0 = lane < HEAD_DIM
    row = lax.broadcasted_iota(jnp.int32, (tq, tq), 0)
    col = lax.broadcasted_iota(jnp.int32, (tq, tq), 1)
    after = (row > col).astype(BF16)
    strictly_causal = col < row

    def pair_lanes(h):
        return slice((h // 2) * LANES, (h // 2 + 1) * LANES)

    def masked_q(h):
        q = q_ref[0, :, pair_lanes(h)] * jnp.asarray(HEAD_DIM ** -0.5, BF16)
        return jnp.where(head0, q, jnp.zeros_like(q)) if h % 2 == 0 else jnp.where(head0, jnp.zeros_like(q), q)

    q_heads = [masked_q(h) for h in range(n_heads)]

    def key_block(ref, kb, h):
        return ref[0, pl.ds(pl.multiple_of(kb * tq, tq), tq), pair_lanes(h)]

    def add_values(kb, h):
        acc_ref[h] += _dot(w_ref[h], key_block(v_ref, kb, h))

    def step(kb, diagonal=False):
        log_betas, log_1ms, suffixes = [], [], []
        kb_next = jnp.maximum(kb - 1, 0)
        for h in range(n_heads):
            z = z_ref[h]
            neg_z = -z
            log_1m = jnp.minimum(neg_z, 0.0) - jnp.log(1.0 + jnp.exp(jnp.minimum(z, neg_z)))
            log_betas.append(z + log_1m)
            if diagonal:
                log_1m = jnp.where(strictly_causal, log_1m, 0.0)
            log_1ms.append(log_1m[:, 0:1])
            suffixes.append(_dot(log_1m.astype(BF16), after))
            z_ref[h] = _dot_nt(q_heads[h], key_block(k_ref, kb_next, h))
            if not diagonal:
                add_values(kb + 1, h)
        for h in range(n_heads):
            carry = carry_ref[h]
            w = jnp.exp(log_betas[h] + suffixes[h] + carry)
            if diagonal:
                w = jnp.where(strictly_causal, w, 0.0)
            w_ref[h] = w.astype(BF16)
            carry_ref[h] = carry + suffixes[h][:, 0:1] + log_1ms[h]

    def later_step(it, _):
        step(qi - 1 - it)
        return 0

    for h in range(n_heads):
        z_ref[h] = _dot_nt(q_heads[h], key_block(k_ref, qi, h))
    acc_ref[...] = jnp.zeros_like(acc_ref)
    carry_ref[...] = jnp.zeros_like(carry_ref)
    step(qi, True)
    lax.fori_loop(0, qi, later_step, 0)
    for h in range(n_heads):
        add_values(0, h)
    for h in range(0, n_heads, 2):
        o_ref[0, :, pair_lanes(h)] = jnp.where(head0, acc_ref[h], acc_ref[h + 1]).astype(o_ref.dtype)


def _stick_breaking(h3, n_heads, col0, tq_target=256):
    bsz, s, _ = h3.shape
    tq = _tile(s, tq_target)
    pairs = n_heads // 2
    per_step = 2 if pairs % 2 == 0 and col0 % 2 == 0 else 1
    groups, lanes, first = pairs // per_step, per_step * LANES, col0 // per_step
    return pl.pallas_call(
        _sb_kernel,
        out_shape=jax.ShapeDtypeStruct((bsz, s, pairs * LANES), BF16),
        grid=(bsz, groups, s // tq),
        in_specs=[pl.BlockSpec((1, tq, lanes), lambda b, p, i: (b, i, first + p)),
                  pl.BlockSpec((1, s, lanes), lambda b, p, i: (b, 0, first + groups + p)),
                  pl.BlockSpec((1, s, lanes), lambda b, p, i: (b, 0, first + 2 * groups + p))],
        out_specs=pl.BlockSpec((1, tq, lanes), lambda b, p, i: (b, i, p)),
        scratch_shapes=[pltpu.VMEM((2 * per_step, tq, LANES), F32), pltpu.VMEM((2 * per_step, tq, tq), F32),
                        pltpu.VMEM((2 * per_step, tq, tq), BF16), pltpu.VMEM((2 * per_step, tq, 1), F32)],
        compiler_params=_params("parallel", "parallel", "arbitrary"),
        name="stick_breaking",
    )(h3, h3, h3)


def _ret_kernel(q_ref, k_ref, v_ref, gate_ref, cos_ref, sin_ref, din_ref, dq_ref, dk_ref,
                dc_ref, gn_ref, o_ref, state_ref):
    @pl.when(pl.program_id(2) == 0)
    def _():
        state_ref[...] = jnp.zeros_like(state_ref)

    lane = lax.broadcasted_iota(jnp.int32, (1, LANES), 1)
    head0 = lane < HEAD_DIM
    first_half = (lane & (HEAD_DIM // 2)) == 0
    row = lax.broadcasted_iota(jnp.int32, (LANES, LANES), 0) < HEAD_DIM
    col = lax.broadcasted_iota(jnp.int32, (LANES, LANES), 1) < HEAD_DIM
    same_head = row == col
    group_mean = jnp.where(same_head, 1.0 / HEAD_DIM, 0.0).astype(BF16)
    cos = cos_ref[...]
    sin = sin_ref[...]

    def rotary(x):
        half = HEAD_DIM // 2
        swapped = jnp.where(first_half, pltpu.roll(x, LANES - half, 1), pltpu.roll(x, half, 1))
        return x * cos + swapped * sin

    pairs = range(q_ref.shape[2] // LANES)
    lanes = lambda p: slice(p * LANES, (p + 1) * LANES)
    qb, kb, v, scores, cross, outer = [], [], [], [], [], []
    for p in pairs:
        q = rotary(q_ref[0, :, lanes(p)].astype(F32)) * (HEAD_DIM ** -0.5)
        k = rotary(k_ref[0, :, lanes(p)].astype(F32))
        v.append(v_ref[0, :, lanes(p)])
        qb.append(q.astype(BF16))
        kb.append(k.astype(BF16))
        scores.append([_dot_nt(jnp.where(mask, qb[p], jnp.zeros_like(qb[p])), kb[p])
                       for mask in (head0, jnp.logical_not(head0))])
        cross.append(_dot(qb[p], state_ref[p].astype(BF16)))
        outer.append(_dot((k * dk_ref[p]).T.astype(BF16), v[p]))
    intra = [[_dot((scores[p][h] * din_ref[2 * p + h]).astype(BF16), v[p]) for h in range(2)] for p in pairs]
    for p in pairs:
        state_ref[p] = state_ref[p] * dc_ref[p] + jnp.where(same_head, outer[p], 0.0)
        o = jnp.where(head0, intra[p][0], intra[p][1]) + cross[p] * dq_ref[p]
        mu = _split_dot(o, group_mean)
        d = o - mu
        var = _split_dot(d * d, group_mean)
        y = d * lax.rsqrt(var + LN_EPS) * gn_ref[:, lanes(p)]
        g = gate_ref[0, :, lanes(p)].astype(F32)
        o_ref[0, :, lanes(p)] = (g / (1.0 + jnp.exp(-g)) * y).astype(o_ref.dtype)


def _retention(h3, gn_g, n_heads, col0, chunk_target=256):
    bsz, s, _ = h3.shape
    c = _tile(s, chunk_target)
    pairs = n_heads // 2
    half = HEAD_DIM // 2

    pos = jnp.arange(s, dtype=F32)
    inv_freq = ROPE_BASE ** (-jnp.arange(half, dtype=F32) / half)
    ang = pos[:, None] * inv_freq[None, :]
    cos = jnp.tile(jnp.cos(ang), (1, LANES // half))
    sin = jnp.tile(jnp.concatenate([-jnp.sin(ang), jnp.sin(ang)], axis=1), (1, LANES // HEAD_DIM))

    gamma = 1.0 - 2.0 ** (-5.0 - jnp.arange(n_heads, dtype=F32))
    log_g = jnp.log(gamma)
    i = jnp.arange(c, dtype=F32)
    diff = i[:, None] - i[None, :]
    d_in = jnp.where(diff >= 0, jnp.exp(log_g[:, None, None] * jnp.maximum(diff, 0.0)), 0.0)
    per_lane = lambda t: jnp.repeat(t.reshape(pairs, 2, -1), HEAD_DIM, axis=1)
    d_q = jnp.swapaxes(per_lane(jnp.exp(log_g[:, None] * (i + 1.0))), 1, 2)
    d_k = jnp.swapaxes(per_lane(jnp.exp(log_g[:, None] * (c - 1.0 - i))), 1, 2)
    d_c = jnp.swapaxes(per_lane(jnp.exp(log_g * c)[:, None]), 1, 2)

    per_step = 2 if pairs % 2 == 0 and col0 % 2 == 0 else 1
    groups, lanes, first = pairs // per_step, per_step * LANES, col0 // per_step
    blk = lambda off: pl.BlockSpec((1, c, lanes), lambda b, p, t: (b, t, first + off * groups + p))
    return pl.pallas_call(
        _ret_kernel,
        out_shape=jax.ShapeDtypeStruct((bsz, s, pairs * LANES), BF16),
        grid=(bsz, groups, s // c),
        in_specs=[blk(0), blk(1), blk(2), blk(3),
                  pl.BlockSpec((c, LANES), lambda b, p, t: (t, 0)),
                  pl.BlockSpec((c, LANES), lambda b, p, t: (t, 0)),
                  pl.BlockSpec((2 * per_step, c, c), lambda b, p, t: (p, 0, 0)),
                  pl.BlockSpec((per_step, c, LANES), lambda b, p, t: (p, 0, 0)),
                  pl.BlockSpec((per_step, c, LANES), lambda b, p, t: (p, 0, 0)),
                  pl.BlockSpec((per_step, 1, LANES), lambda b, p, t: (p, 0, 0)),
                  pl.BlockSpec((1, lanes), lambda b, p, t: (0, p))],
        out_specs=pl.BlockSpec((1, c, lanes), lambda b, p, t: (b, t, p)),
        scratch_shapes=[pltpu.VMEM((per_step, LANES, LANES), F32)],
        compiler_params=_params("parallel", "parallel", "arbitrary"),
        name="retention",
    )(h3, h3, h3, h3, cos, sin, d_in, d_q, d_k, d_c, gn_g.reshape(1, -1))


def _chunk_attn_kernel(q_ref, k_ref, v_ref, bias_ref, o_ref):
    tq = q_ref.shape[1]
    t0 = pl.program_id(2) * tq
    n_pairs = q_ref.shape[2] // LANES
    lane = lax.broadcasted_iota(jnp.int32, (1, LANES), 1)
    head0 = lane < HEAD_DIM

    width = bias_ref.shape[3]
    start = pl.multiple_of(jnp.maximum(t0 - LEFT_CHUNKS * CHUNK, 0), tq)

    def pair_slice(ref, pair, rows):
        return ref[0, rows, pair * LANES:(pair + 1) * LANES]

    logits = []
    for pair in range(n_pairs):
        q = pair_slice(q_ref, pair, slice(None)) * jnp.asarray(HEAD_DIM ** -0.5, BF16)
        k = pair_slice(k_ref, pair, pl.ds(start, width))
        logits.append(_dot_nt(jnp.where(head0, q, jnp.zeros_like(q)), k))
        logits.append(_dot_nt(jnp.where(head0, jnp.zeros_like(q), q), k))
    for pair in range(n_pairs):
        probs, denoms = [], []
        for h in range(2):
            head = 2 * pair + h
            s = logits[head] + bias_ref[0, 0, head * tq:(head + 1) * tq, :]
            p = jnp.exp(s - jnp.max(s, axis=-1, keepdims=True))
            denoms.append(jnp.sum(p, axis=-1, keepdims=True))
            probs.append(p.astype(BF16))
        v = pair_slice(v_ref, pair, pl.ds(start, width))
        outs = [_dot(probs[h], v) / denoms[h] for h in range(2)]
        o_ref[0, :, pair * LANES:(pair + 1) * LANES] = jnp.where(head0, outs[0], outs[1]).astype(o_ref.dtype)


def _window_bias(rel_bias, tq, width):
    n_heads = rel_bias.shape[0]
    left = LEFT_CHUNKS * CHUNK
    rel_min, rel_max = left - (width - 1), left + tq - 1
    assert rel_min >= -MAX_REL and rel_max >= MAX_REL
    by_rel = jnp.concatenate(
        [rel_bias[:, rel_min + MAX_REL:], jnp.broadcast_to(rel_bias[:, -1:], (n_heads, rel_max - MAX_REL))], axis=1)
    m = rel_max - rel_min + 1
    rev = jnp.concatenate([by_rel[:, ::-1], jnp.zeros((n_heads, 1), by_rel.dtype)], axis=1)
    rows = jnp.tile(rev, (1, tq))[:, :tq * m].reshape(n_heads, tq, m)
    bias = rows[:, :, tq - 1:tq - 1 + width].astype(F32)
    q_chunk = jnp.arange(tq)[:, None] // CHUNK + LEFT_CHUNKS
    k_chunk = jnp.arange(width)[None, :] // CHUNK
    visible = (k_chunk >= q_chunk - LEFT_CHUNKS) & (k_chunk <= q_chunk)
    return jnp.where(visible[None], bias, NEG_BIG)


def _chunk_attention(qkv3, rel_bias, n_heads, tq=256):
    bsz, s, _ = qkv3.shape
    pairs = n_heads // 2
    n_shift = (LEFT_CHUNKS * CHUNK) // tq
    width = (n_shift + 1) * tq
    assert s >= width
    bias = _window_bias(rel_bias, tq, width)
    per_step = next(n for n in (4, 2, 1) if pairs % n == 0)
    groups, heads = pairs // per_step, 2 * per_step
    lanes = per_step * LANES
    bias = jnp.stack([jnp.pad(bias[:, :, v * tq:], ((0, 0), (0, 0), (0, v * tq)), constant_values=NEG_BIG)
                      for v in range(n_shift + 1)], axis=1).reshape(groups, heads, n_shift + 1, tq, width)
    bias = bias.transpose(0, 2, 1, 3, 4).reshape(groups, n_shift + 1, heads * tq, width)

    return pl.pallas_call(
        _chunk_attn_kernel,
        out_shape=jax.ShapeDtypeStruct((bsz, s, pairs * LANES), BF16),
        grid=(bsz, groups, s // tq),
        in_specs=[pl.BlockSpec((1, tq, lanes), lambda b, p, t: (b, t, p)),
                  pl.BlockSpec((1, s, lanes), lambda b, p, t: (b, 0, groups + p)),
                  pl.BlockSpec((1, s, lanes), lambda b, p, t: (b, 0, 2 * groups + p)),
                  pl.BlockSpec((1, 1, heads * tq, width), lambda b, p, t: (p, jnp.maximum(n_shift - t, 0), 0, 0))],
        out_specs=pl.BlockSpec((1, tq, lanes), lambda b, p, t: (b, t, p)),
        compiler_params=_params("parallel", "parallel", "arbitrary"),
        name="chunk_attention",
    )(qkv3, qkv3, qkv3, bias)


def _swiglu_ln_kernel(alpha, x_ref, wg_ref, wu_ref, wd_ref, g_ref, b_ref, o_ref, xb_ref, acc_ref):
    f = pl.program_id(1)

    @pl.when(f == 0)
    def _():
        xb_ref[...] = x_ref[...].astype(BF16)
        acc_ref[...] = jnp.zeros_like(acc_ref)

    xb = xb_ref[...]
    gate = _dot(xb, wg_ref[...])
    up = _dot(xb, wu_ref[...])
    hidden = (gate / (1.0 + jnp.exp(-gate)) * up).astype(BF16)
    acc_ref[...] += _dot(hidden, wd_ref[...])

    @pl.when(f == pl.num_programs(1) - 1)
    def _():
        o_ref[...] = _layer_norm(alpha * x_ref[...] + acc_ref[...], g_ref[...], b_ref[...])


def _swiglu_ln(x, wg, wu, wd, g, b, alpha, tm_target=512, tf_target=2816):
    t, d = x.shape
    ff = wg.shape[1]
    tm, tf = _tile(t, tm_target), _tile(ff, tf_target)
    once = dict(pipeline_mode=pl.Buffered(1)) if tf == ff else {}
    return pl.pallas_call(
        functools.partial(_swiglu_ln_kernel, alpha),
        out_shape=jax.ShapeDtypeStruct((t, d), F32),
        grid=(t // tm, ff // tf),
        in_specs=[pl.BlockSpec((tm, d), lambda i, f: (i, 0)),
                  pl.BlockSpec((d, tf), lambda i, f: (0, f), **once),
                  pl.BlockSpec((d, tf), lambda i, f: (0, f), **once),
                  pl.BlockSpec((tf, d), lambda i, f: (f, 0), **once),
                  pl.BlockSpec((1, d), lambda i, f: (0, 0)),
                  pl.BlockSpec((1, d), lambda i, f: (0, 0))],
        out_specs=pl.BlockSpec((tm, d), lambda i, f: (i, 0)),
        scratch_shapes=[pltpu.VMEM((tm, d), BF16), pltpu.VMEM((tm, d), F32)],
        compiler_params=_params("parallel", "arbitrary"),
        name="swiglu_ln",
    )(x, wg, wu, wd, g.reshape(1, d), b.reshape(1, d))


ROUTE_E1, ROUTE_E2, ROUTE_RANK1, ROUTE_RANK2 = range(4)
ROUTE_FIELDS = 8
MOE_TILE = 512
MOE_WINDOW = 256
ROW_ALIGN = 16
PAIR_LIVE, PAIR_FIRST, PAIR_LAST, PAIR_NARROW, PAIR_LIVE_B, PAIR_NARROW_B = 1, 2, 4, 8, 16, 32


def _route_kernel(n_experts, x_ref, w_ref, b_ref, col_ref, row_ref, gate_ref, cnt_ref, run_ref):
    @pl.when(pl.program_id(0) == 0)
    def _():
        run_ref[...] = jnp.zeros_like(run_ref)

    x = x_ref[...]
    w = w_ref[...]
    tm = x.shape[0]
    x_hi = x.astype(BF16)
    x_lo = (x - x_hi.astype(F32)).astype(BF16)
    w_hi = w.astype(BF16)
    w_lo = (w - w_hi.astype(F32)).astype(BF16)
    logits = _dot(x_hi, w_hi) + (_dot(x_hi, w_lo) + _dot(x_lo, w_hi)) + b_ref[...]
    lane = lax.broadcasted_iota(jnp.int32, logits.shape, 1).astype(F32)
    neg_inf = jnp.asarray(-jnp.inf, F32)
    logits = jnp.where(lane < n_experts, logits, neg_inf)
    top1 = jnp.max(logits, axis=-1, keepdims=True)
    idx1 = jnp.min(jnp.where(logits == top1, lane, float(LANES)), axis=-1, keepdims=True)
    rest = jnp.where(lane == idx1, neg_inf, logits)
    top2 = jnp.max(rest, axis=-1, keepdims=True)
    idx2 = jnp.min(jnp.where(rest == top2, lane, float(LANES)), axis=-1, keepdims=True)
    e2 = jnp.exp(top2 - top1)
    denom = 1.0 + e2

    chosen = jnp.where(jnp.logical_or(lane == idx1, lane == idx2), 1.0, 0.0)
    r = lax.broadcasted_iota(jnp.int32, (tm, tm), 0)
    c = lax.broadcasted_iota(jnp.int32, (tm, tm), 1)
    earlier = (c < r).astype(BF16)
    before = _dot(earlier, chosen.astype(BF16)) + run_ref[...]
    rank1 = jnp.sum(jnp.where(lane == idx1, before, 0.0), axis=-1, keepdims=True)
    rank2 = jnp.sum(jnp.where(lane == idx2, before, 0.0), axis=-1, keepdims=True)
    total = jnp.sum(chosen, axis=0, keepdims=True)
    cnt_ref[0] = total
    run_ref[...] += total

    fields = {ROUTE_E1: idx1, ROUTE_E2: idx2, ROUTE_RANK1: rank1, ROUTE_RANK2: rank2}
    meta = jnp.zeros_like(logits)
    for field, value in fields.items():
        meta = jnp.where(lane == field, value, meta)
    col_ref[...] = meta
    row_ref[0] = meta.T[0:ROUTE_FIELDS, :]
    gates = jnp.where(lane == idx1, 1.0 / denom, jnp.where(lane == idx2, e2 / denom, 0.0))
    gates_hi = gates.astype(BF16)
    gate_ref[:, 0:LANES] = gates_hi
    gate_ref[:, LANES:2 * LANES] = (gates - gates_hi.astype(F32)).astype(BF16)


def _route(x, w_router, b_router, tm):
    t, d = x.shape
    n_experts = w_router.shape[1]
    nb = t // tm
    w = jnp.pad(w_router.astype(F32), ((0, 0), (0, LANES - n_experts)))
    b = jnp.pad(b_router.astype(F32), (0, LANES - n_experts)).reshape(1, LANES)
    return pl.pallas_call(
        functools.partial(_route_kernel, n_experts),
        out_shape=[jax.ShapeDtypeStruct((t, LANES), F32),
                   jax.ShapeDtypeStruct((nb, ROUTE_FIELDS, tm), F32),
                   jax.ShapeDtypeStruct((t, 2 * LANES), BF16),
                   jax.ShapeDtypeStruct((nb, 1, LANES), F32)],
        grid=(nb,),
        in_specs=[pl.BlockSpec((tm, d), lambda i: (i, 0)),
                  pl.BlockSpec((d, LANES), lambda i: (0, 0)),
                  pl.BlockSpec((1, LANES), lambda i: (0, 0))],
        out_specs=[pl.BlockSpec((tm, LANES), lambda i: (i, 0)),
                   pl.BlockSpec((1, ROUTE_FIELDS, tm), lambda i: (i, 0, 0)),
                   pl.BlockSpec((tm, 2 * LANES), lambda i: (i, 0)),
                   pl.BlockSpec((1, 1, LANES), lambda i: (i, 0, 0))],
        scratch_shapes=[pltpu.VMEM((1, LANES), F32)],
        compiler_params=_params("arbitrary"),
        name="route",
    )(x, w, b)


def _moe_plan(cnt, n_experts, tile, n_tiles):
    nb = cnt.shape[0]
    counts = cnt[:, 0, :n_experts].astype(jnp.int32)
    cum = jnp.cumsum(counts, axis=0) - counts
    total = jnp.sum(counts, axis=0)
    tiles_e = (total + tile - 1) // tile
    tile_end = jnp.cumsum(tiles_e)
    row_off = (tile_end - tiles_e) * tile
    start = row_off[None, :] + cum
    j0 = start // tile
    j1 = (start + counts - 1) // tile
    has = counts > 0
    tiles = jnp.stack([j0, j0 + 1], axis=-1)
    valid = jnp.stack([has, has & (j1 > j0)], axis=-1)
    lo = jnp.clip(start[..., None] - tiles * tile, 0, tile)
    hi = jnp.clip((start + counts)[..., None] - tiles * tile, 0, tile)
    win = min(MOE_WINDOW, tile)
    window = jnp.minimum(lo // ROW_ALIGN * ROW_ALIGN, tile - win)
    narrow = hi <= window + win

    n_pairs = nb * n_experts + n_tiles
    slot = jnp.arange(n_pairs)
    block_id = jnp.broadcast_to(jnp.arange(nb)[:, None, None], tiles.shape)

    def listed(order, group_key, n_runs):
        flat = lambda a: a.transpose(order).reshape(-1)
        n_live = jnp.sum(valid)
        idx = jnp.nonzero(flat(valid), size=n_pairs, fill_value=0)[0]
        idx = jnp.where(slot < n_live, idx, idx[n_live - 1])
        pair = {name: flat(a)[idx].astype(jnp.int32)
                for name, a in (("tile", tiles), ("block", block_id), ("window", window), ("narrow", narrow))}
        key = pair[group_key]
        live = slot < n_live
        first = live & (key != jnp.concatenate([jnp.full((1,), -1, key.dtype), key[:-1]]))
        last = live & ((slot == n_live - 1) | (key != jnp.concatenate([key[1:], key[-1:]])))
        run_start = lax.cummax(jnp.where(first, slot, 0), axis=0)
        opens = live & ((slot - run_start) % 2 == 0)
        n_steps = (n_pairs + n_runs) // 2 + 1
        n_open = jnp.sum(opens)
        step = jnp.arange(n_steps)
        ia = jnp.nonzero(opens, size=n_steps, fill_value=0)[0]
        ia = jnp.where(step < n_open, ia, ia[n_open - 1])
        nxt = jnp.minimum(ia + 1, n_pairs - 1)
        live_a = step < n_open
        live_b = live_a & (nxt != ia) & live[nxt] & (key[nxt] == key[ia])
        ib = jnp.where(live_b, nxt, ia)
        steps = {}
        for name in ("tile", "block", "window"):
            steps[name + "_a"], steps[name + "_b"] = pair[name][ia], pair[name][ib]
        steps["flag"] = (PAIR_LIVE * live_a + PAIR_FIRST * (live_a & first[ia])
                         + PAIR_LAST * (live_a & (last[ia] | (live_b & last[ib])))
                         + PAIR_NARROW * pair["narrow"][ia] + PAIR_LIVE_B * live_b
                         + PAIR_NARROW_B * pair["narrow"][ib]).astype(jnp.int32)
        return steps

    by_expert = listed((1, 0, 2), "tile", n_tiles)
    by_block = listed((0, 1, 2), "block", nb)
    n_used = tile_end[-1:].astype(jnp.int32)
    tile_expert = jnp.minimum(jnp.searchsorted(tile_end, jnp.arange(n_tiles), side="right"),
                              n_experts - 1).astype(jnp.int32)
    return dict(by_expert=by_expert, by_block=by_block, row_off=row_off.astype(jnp.int32),
                n_used=n_used, tile_expert=tile_expert)


def _with_positions(meta, row_off, field_axis):
    def take(first):
        return lax.slice_in_dim(meta, first, first + TOP_K, axis=field_axis)

    expert, offset = take(ROUTE_E1), jnp.zeros_like(take(ROUTE_E1))
    for e in range(row_off.shape[0]):
        offset = jnp.where(expert == float(e), row_off[e].astype(F32), offset)
    n_fields = meta.shape[field_axis]
    return jnp.concatenate([take(ROUTE_E1), take(ROUTE_RANK1) + offset,
                            lax.slice_in_dim(meta, ROUTE_RANK1 + TOP_K, n_fields, axis=field_axis)], axis=field_axis)


def _gather_kernel(tile_ref, block_a_ref, block_b_ref, window_a_ref, window_b_ref, flag_ref,
                   row_a_ref, gate_a_ref, x_a_ref, row_b_ref, gate_b_ref, x_b_ref, xs_ref, gs_ref, acc_ref):
    step = pl.program_id(0)
    flag = flag_ref[step]
    tile, tm = xs_ref.shape[0], x_a_ref.shape[0]

    def place_pair(row_ref, gate_ref, x_ref, window_ref, narrow):
        pos1 = row_ref[0, ROUTE_RANK1:ROUTE_RANK1 + 1, :].astype(jnp.int32)
        pos2 = row_ref[0, ROUTE_RANK2:ROUTE_RANK2 + 1, :].astype(jnp.int32)

        def place(row0, n_rows):
            rows = tile_ref[step] * tile + row0 + lax.broadcasted_iota(jnp.int32, (n_rows, tm), 0)
            onehot = jnp.where(jnp.logical_or(rows == pos1, rows == pos2), 1.0, 0.0).astype(BF16)
            acc_ref[pl.ds(row0, n_rows), :] += _dot(onehot, x_ref[...])
            gs_ref[pl.ds(row0, n_rows), :] += _dot(onehot, gate_ref[...])

        @pl.when(narrow)
        def _():
            place(pl.multiple_of(window_ref[step], ROW_ALIGN), min(MOE_WINDOW, tile))

        @pl.when(jnp.logical_not(narrow))
        def _():
            place(0, tile)

    @pl.when((flag & PAIR_LIVE) != 0)
    def _():
        @pl.when((flag & PAIR_FIRST) != 0)
        def _():
            acc_ref[...] = jnp.zeros_like(acc_ref)
            gs_ref[...] = jnp.zeros_like(gs_ref)

        place_pair(row_a_ref, gate_a_ref, x_a_ref, window_a_ref, (flag & PAIR_NARROW) != 0)

        @pl.when((flag & PAIR_LIVE_B) != 0)
        def _():
            place_pair(row_b_ref, gate_b_ref, x_b_ref, window_b_ref, (flag & PAIR_NARROW_B) != 0)

        @pl.when((flag & PAIR_LAST) != 0)
        def _():
            xs_ref[...] = acc_ref[...].astype(BF16)


def _moe_gather(plan, pos_row, gates, xb, tile, n_tiles):
    t, d = xb.shape
    _, _, tm = pos_row.shape
    steps = plan["by_expert"]

    def token_block(which):
        return [pl.BlockSpec((1, ROUTE_FIELDS, tm), lambda p, *s: (s[which][p], 0, 0)),
                pl.BlockSpec((tm, 2 * LANES), lambda p, *s: (s[which][p], 0)),
                pl.BlockSpec((tm, d), lambda p, *s: (s[which][p], 0))]

    return pl.pallas_call(
        _gather_kernel,
        out_shape=[jax.ShapeDtypeStruct((n_tiles * tile, d), BF16),
                   jax.ShapeDtypeStruct((n_tiles * tile, 2 * LANES), F32)],
        grid_spec=pltpu.PrefetchScalarGridSpec(
            num_scalar_prefetch=6,
            grid=(steps["flag"].shape[0],),
            in_specs=token_block(1) + token_block(2),
            out_specs=[pl.BlockSpec((tile, d), lambda p, tiles, *_: (tiles[p], 0)),
                       pl.BlockSpec((tile, 2 * LANES), lambda p, tiles, *_: (tiles[p], 0))],
            scratch_shapes=[pltpu.VMEM((tile, d), F32)]),
        compiler_params=_params("arbitrary"),
        name="moe_gather",
    )(steps["tile_a"], steps["block_a"], steps["block_b"], steps["window_a"], steps["window_b"], steps["flag"],
      pos_row, gates, xb, pos_row, gates, xb)


def _moe_ffn_kernel(te_ref, used_ref, xs_ref, gs_ref, wg_ref, wu_ref, wd_ref, o_ref, acc_ref):
    f = pl.program_id(1)
    tile_id = pl.program_id(0)

    @pl.when(tile_id < used_ref[0])
    def _():
        @pl.when(f == 0)
        def _():
            acc_ref[...] = jnp.zeros_like(acc_ref)

        xs = xs_ref[...]
        gate = _dot(xs, wg_ref[0])
        up = _dot(xs, wu_ref[0])
        hidden = (gate / (1.0 + jnp.exp(-gate)) * up).astype(BF16)
        acc_ref[...] += _dot(hidden, wd_ref[0])

        @pl.when(f == pl.num_programs(1) - 1)
        def _():
            by_expert = gs_ref[:, 0:LANES] + gs_ref[:, LANES:2 * LANES]
            lane = lax.broadcasted_iota(jnp.int32, by_expert.shape, 1)
            row_gate = jnp.sum(jnp.where(lane == te_ref[tile_id], by_expert, 0.0), axis=1, keepdims=True)
            o_ref[...] = (acc_ref[...] * row_gate).astype(o_ref.dtype)


def _moe_ffn(plan, xs, gs, wg, wu, wd, tile, tf_target=1792):
    rows, d = xs.shape
    ff = wg.shape[2]
    tf = _tile(ff, tf_target)
    nf = ff // tf

    def tile_of(i, used):
        return jnp.minimum(i, used[0] - 1)

    def chunk_of(i, f, used):
        return jnp.where(i < used[0], f, nf - 1)

    return pl.pallas_call(
        _moe_ffn_kernel,
        out_shape=jax.ShapeDtypeStruct((rows, d), BF16),
        grid_spec=pltpu.PrefetchScalarGridSpec(
            num_scalar_prefetch=2,
            grid=(rows // tile, nf),
            in_specs=[pl.BlockSpec((tile, d), lambda i, f, te, used: (tile_of(i, used), 0)),
                      pl.BlockSpec((tile, 2 * LANES), lambda i, f, te, used: (tile_of(i, used), 0)),
                      pl.BlockSpec((1, d, tf), lambda i, f, te, used: (te[tile_of(i, used)], 0, chunk_of(i, f, used))),
                      pl.BlockSpec((1, d, tf), lambda i, f, te, used: (te[tile_of(i, used)], 0, chunk_of(i, f, used))),
                      pl.BlockSpec((1, tf, d), lambda i, f, te, used: (te[tile_of(i, used)], chunk_of(i, f, used), 0))],
            out_specs=pl.BlockSpec((tile, d), lambda i, f, te, used: (tile_of(i, used), 0)),
            scratch_shapes=[pltpu.VMEM((tile, d), F32)]),
        compiler_params=_params("arbitrary", "arbitrary"),
        name="moe_ffn",
    )(plan["tile_expert"], plan["n_used"], xs, gs, wg, wu, wd)


def _combine_ln_kernel(alpha, block_ref, tile_a_ref, tile_b_ref, window_a_ref, window_b_ref, flag_ref,
                       col_ref, os_a_ref, os_b_ref, x_ref, g_ref, b_ref, o_ref, acc_ref):
    step = pl.program_id(0)
    flag = flag_ref[step]
    tm, tile = x_ref.shape[0], os_a_ref.shape[0]

    def collect_pair(os_ref, tile_ref, window_ref, narrow):
        pos1 = col_ref[:, ROUTE_RANK1:ROUTE_RANK1 + 1].astype(jnp.int32)
        pos2 = col_ref[:, ROUTE_RANK2:ROUTE_RANK2 + 1].astype(jnp.int32)

        def collect(row0, n_rows):
            cols = tile_ref[step] * tile + row0 + lax.broadcasted_iota(jnp.int32, (tm, n_rows), 1)
            onehot = jnp.where(jnp.logical_or(cols == pos1, cols == pos2), 1.0, 0.0).astype(BF16)
            acc_ref[...] += _dot(onehot, os_ref[pl.ds(row0, n_rows), :])

        @pl.when(narrow)
        def _():
            collect(pl.multiple_of(window_ref[step], ROW_ALIGN), min(MOE_WINDOW, tile))

        @pl.when(jnp.logical_not(narrow))
        def _():
            collect(0, tile)

    @pl.when((flag & PAIR_LIVE) != 0)
    def _():
        @pl.when((flag & PAIR_FIRST) != 0)
        def _():
            acc_ref[...] = jnp.zeros_like(acc_ref)

        collect_pair(os_a_ref, tile_a_ref, window_a_ref, (flag & PAIR_NARROW) != 0)

        @pl.when((flag & PAIR_LIVE_B) != 0)
        def _():
            collect_pair(os_b_ref, tile_b_ref, window_b_ref, (flag & PAIR_NARROW_B) != 0)

    @pl.when((flag & PAIR_LAST) != 0)
    def _():
        o_ref[...] = _layer_norm(alpha * x_ref[...] + acc_ref[...], g_ref[...], b_ref[...])


def _moe_combine_ln(plan, pos_col, o_sorted, x, g, b, alpha, tile, tm):
    t, d = x.shape
    steps = plan["by_block"]
    return pl.pallas_call(
        functools.partial(_combine_ln_kernel, alpha),
        out_shape=jax.ShapeDtypeStruct((t, d), F32),
        grid_spec=pltpu.PrefetchScalarGridSpec(
            num_scalar_prefetch=6,
            grid=(steps["flag"].shape[0],),
            in_specs=[pl.BlockSpec((tm, LANES), lambda p, blocks, *_: (blocks[p], 0)),
                      pl.BlockSpec((tile, d), lambda p, blocks, tiles_a, *_: (tiles_a[p], 0)),
                      pl.BlockSpec((tile, d), lambda p, blocks, tiles_a, tiles_b, *_: (tiles_b[p], 0)),
                      pl.BlockSpec((tm, d), lambda p, blocks, *_: (blocks[p], 0)),
                      pl.BlockSpec((1, d), lambda p, *_: (0, 0)),
                      pl.BlockSpec((1, d), lambda p, *_: (0, 0))],
            out_specs=pl.BlockSpec((tm, d), lambda p, blocks, *_: (blocks[p], 0)),
            scratch_shapes=[pltpu.VMEM((tm, d), F32)]),
        compiler_params=_params("arbitrary"),
        name="moe_combine_ln",
    )(steps["block_a"], steps["tile_a"], steps["tile_b"], steps["window_a"], steps["window_b"], steps["flag"],
      pos_col, o_sorted, o_sorted, x, g.reshape(1, d), b.reshape(1, d))


def _moe_ln(x, xb, w_router, b_router, wg, wu, wd, g, b, alpha):
    t, _ = x.shape
    n_experts = wg.shape[0]
    tile = _tile(t, MOE_TILE)
    n_tiles = TOP_K * t // tile + n_experts
    meta_col, meta_row, gates, cnt = _route(x, w_router, b_router, tile)
    plan = _moe_plan(cnt, n_experts, tile, n_tiles)
    pos_row = _with_positions(meta_row, plan["row_off"], 1)
    pos_col = _with_positions(meta_col, plan["row_off"], 1)
    xs, gs = _moe_gather(plan, pos_row, gates, xb, tile, n_tiles)
    o_sorted = _moe_ffn(plan, xs, gs, wg, wu, wd, tile)
    return _moe_combine_ln(plan, pos_col, o_sorted, x, g, b, alpha, tile, tile)


def _even_layer(x, bsz, s, alpha, ln1_g, ln1_b, w_in, ret_gn_g, w_out, ln2_g, ln2_b, w_gate, w_up, w_down):
    d = x.shape[1]
    n_heads = d // HEAD_DIM
    n_sb = n_heads // 2
    n_ret = n_heads - n_sb
    h = _proj(x, w_in.astype(BF16))
    h3 = h.reshape(bsz, s, h.shape[1])
    o_sb = _stick_breaking(h3, n_sb, 0)
    o_ret = _retention(h3, ret_gn_g, n_ret, 3 * n_sb * HEAD_DIM // LANES)
    x = _out_ln([o_sb.reshape(bsz * s, -1), o_ret.reshape(bsz * s, -1)], w_out.astype(BF16),
                x, ln1_g, ln1_b, alpha)
    return _swiglu_ln(x, w_gate.astype(BF16), w_up.astype(BF16), w_down.astype(BF16), ln2_g, ln2_b, alpha)


def _odd_layer(x, bsz, s, alpha, ln1_g, ln1_b, w_qkv, rel_bias, w_out, ln2_g, ln2_b, w_router, b_router,
               w_gate, w_up, w_down):
    d = x.shape[1]
    qkv = _proj(x, w_qkv.astype(BF16))
    o = _chunk_attention(qkv.reshape(bsz, s, 3 * d), rel_bias, d // HEAD_DIM)
    x, xb = _out_ln([o.reshape(bsz * s, d)], w_out.astype(BF16), x, ln1_g, ln1_b, alpha, also_bf16=True)
    return _moe_ln(x, xb, w_router, b_router, w_gate.astype(BF16), w_up.astype(BF16), w_down.astype(BF16),
                   ln2_g, ln2_b, alpha)


def kernel(x, even_ln1_g, even_ln1_b, even_w_in, even_ret_gn_g, even_w_out, even_ln2_g, even_ln2_b,
           even_w_gate, even_w_up, even_w_down, odd_ln1_g, odd_ln1_b, odd_w_qkv, odd_rel_bias, odd_w_out,
           odd_ln2_g, odd_ln2_b, odd_w_router, odd_b_router, odd_w_gate, odd_w_up, odd_w_down):
    bsz, s, d = x.shape
    depth = even_w_in.shape[0] + odd_w_qkv.shape[0]
    alpha = (2 * depth) ** 0.25
    xt = x.reshape(bsz * s, d)
    for layer in range(depth):
        i = layer // 2
        if layer % 2 == 0:
            xt = _even_layer(xt, bsz, s, alpha, even_ln1_g[i], even_ln1_b[i], even_w_in[i], even_ret_gn_g[i],
                             even_w_out[i], even_ln2_g[i], even_ln2_b[i], even_w_gate[i], even_w_up[i],
                             even_w_down[i])
        else:
            xt = _odd_layer(xt, bsz, s, alpha, odd_ln1_g[i], odd_ln1_b[i], odd_w_qkv[i], odd_rel_bias[i],
                            odd_w_out[i], odd_ln2_g[i], odd_ln2_b[i], odd_w_router[i], odd_b_router[i],
                            odd_w_gate[i], odd_w_up[i], odd_w_down[i])
    return xt.reshape(bsz, s, d)
```

```python
import functools

import jax
import jax.numpy as jnp
from jax import lax
from jax.experimental import pallas as pl
from jax.experimental.pallas import tpu as pltpu

HEAD_DIM = 64
LANES = 128
CHUNK = 64
LEFT_CHUNKS = 8
MAX_REL = 4 * CHUNK
TOP_K = 2
ROPE_BASE = 10000.0
LN_EPS = 1e-5
NEG_BIG = -1e30
VMEM_LIMIT = 56 * 1024 * 1024

BF16 = jnp.bfloat16
F32 = jnp.float32


def _params(*sem):
    return pltpu.CompilerParams(dimension_semantics=sem, vmem_limit_bytes=VMEM_LIMIT)


def _tile(n, target):
    if n <= target:
        return n
    t = target - target % LANES
    while t >= LANES:
        if n % t == 0:
            return t
        t -= LANES
    return n


def _dot(a, b):
    return jnp.dot(a, b, preferred_element_type=F32)


def _dot_nt(a, b):
    return lax.dot_general(a, b, (((1,), (1,)), ((), ())), preferred_element_type=F32)


def _split_dot(a, b_bf16):
    hi = a.astype(BF16)
    lo = (a - hi.astype(F32)).astype(BF16)
    return _dot(hi, b_bf16) + _dot(lo, b_bf16)


def _layer_norm(r, g, b):
    mu = jnp.mean(r, axis=-1, keepdims=True)
    d = r - mu
    var = jnp.mean(d * d, axis=-1, keepdims=True)
    return d * lax.rsqrt(var + LN_EPS) * g + b


def _proj_kernel(x_ref, w_ref, o_ref, xb_ref):
    @pl.when(pl.program_id(1) == 0)
    def _():
        xb_ref[...] = x_ref[...].astype(BF16)

    o_ref[...] = _dot(xb_ref[...], w_ref[...]).astype(o_ref.dtype)


def _proj(x, w, tm_target=1024, tn_target=1792):
    t, k = x.shape
    n = w.shape[1]
    tm, tn = _tile(t, tm_target), _tile(n, tn_target)
    return pl.pallas_call(
        _proj_kernel,
        out_shape=jax.ShapeDtypeStruct((t, n), BF16),
        grid=(t // tm, n // tn),
        in_specs=[pl.BlockSpec((tm, k), lambda i, j: (i, 0)),
                  pl.BlockSpec((k, tn), lambda i, j: (0, j))],
        out_specs=pl.BlockSpec((tm, tn), lambda i, j: (i, j)),
        scratch_shapes=[pltpu.VMEM((tm, k), BF16)],
        compiler_params=_params("parallel", "arbitrary"),
        name="proj",
    )(x, w)


def _out_ln_kernel(alpha, n_a, *refs):
    a_refs = refs[:n_a]
    w_ref, x_ref, g_ref, b_ref = refs[n_a:n_a + 4]
    o_refs = refs[n_a + 4:]
    ka = a_refs[0].shape[1]
    y = _dot(a_refs[0][...], w_ref[0:ka, :])
    for idx in range(1, n_a):
        y = y + _dot(a_refs[idx][...], w_ref[idx * ka:(idx + 1) * ka, :])
    out = _layer_norm(alpha * x_ref[...] + y, g_ref[...], b_ref[...])
    for o_ref in o_refs:
        o_ref[...] = out.astype(o_ref.dtype)


def _out_ln(a_list, w, x, g, b, alpha, also_bf16=False, tm_target=1024):
    t, d = x.shape
    tm = _tile(t, tm_target)
    ka = a_list[0].shape[1]
    row = lambda i: (i, 0)
    fixed = lambda i: (0, 0)
    dtypes = [F32, BF16] if also_bf16 else [F32]
    outs = pl.pallas_call(
        functools.partial(_out_ln_kernel, alpha, len(a_list)),
        out_shape=[jax.ShapeDtypeStruct((t, d), dt) for dt in dtypes],
        grid=(t // tm,),
        in_specs=[pl.BlockSpec((tm, ka), row) for _ in a_list]
        + [pl.BlockSpec(w.shape, fixed), pl.BlockSpec((tm, d), row),
           pl.BlockSpec((1, d), fixed), pl.BlockSpec((1, d), fixed)],
        out_specs=[pl.BlockSpec((tm, d), row) for _ in dtypes],
        compiler_params=_params("parallel"),
        name="out_ln",
    )(*a_list, w, x, g.reshape(1, d), b.reshape(1, d))
    return outs if also_bf16 else outs[0]


def _sb_kernel(q_ref, k_ref, v_ref, o_ref, acc_ref, z_ref, w_ref, carry_ref):
    tq = q_ref.shape[1]
    n_heads = 2 * (q_ref.shape[2] // LANES)
    qi = pl.program_id(2)
    lane = lax.broadcasted_iota(jnp.int32, (1, LANES), 1)
    head0 = lane < HEAD_DIM
    row = lax.broadcasted_iota(jnp.int32, (tq, tq), 0)
    col = lax.broadcasted_iota(jnp.int32, (tq, tq), 1)
    after = (row > col).astype(BF16)
    strictly_causal = col < row

    def pair_lanes(h):
        return slice((h // 2) * LANES, (h // 2 + 1) * LANES)

    def masked_q(h):
        q = q_ref[0, :, pair_lanes(h)] * jnp.asarray(HEAD_DIM ** -0.5, BF16)
        return jnp.where(head0, q, jnp.zeros_like(q)) if h % 2 == 0 else jnp.where(head0, jnp.zeros_like(q), q)

    q_heads = [masked_q(h) for h in range(n_heads)]

    def key_block(ref, kb, h):
        return ref[0, pl.ds(pl.multiple_of(kb * tq, tq), tq), pair_lanes(h)]

    def add_values(kb, h):
        acc_ref[h] += _dot(w_ref[h], key_block(v_ref, kb, h))

    def step(kb, diagonal=False):
        log_betas, log_1ms, suffixes = [], [], []
        kb_next = jnp.maximum(kb - 1, 0)
        for h in range(n_heads):
            z = z_ref[h]
            neg_z = -z
            log_1m = jnp.minimum(neg_z, 0.0) - jnp.log(1.0 + jnp.exp(jnp.minimum(z, neg_z)))
            log_betas.append(z + log_1m)
            if diagonal:
                log_1m = jnp.where(strictly_causal, log_1m, 0.0)
            log_1ms.append(log_1m[:, 0:1])
            suffixes.append(_dot(log_1m.astype(BF16), after))
            z_ref[h] = _dot_nt(q_heads[h], key_block(k_ref, kb_next, h))
            if not diagonal:
                add_values(kb + 1, h)
        for h in range(n_heads):
            carry = carry_ref[h]
            w = jnp.exp(log_betas[h] + suffixes[h] + carry)
            if diagonal:
                w = jnp.where(strictly_causal, w, 0.0)
            w_ref[h] = w.astype(BF16)
            carry_ref[h] = carry + suffixes[h][:, 0:1] + log_1ms[h]

    def later_step(it, _):
        step(qi - 1 - it)
        return 0

    for h in range(n_heads):
        z_ref[h] = _dot_nt(q_heads[h], key_block(k_ref, qi, h))
    acc_ref[...] = jnp.zeros_like(acc_ref)
    carry_ref[...] = jnp.zeros_like(carry_ref)
    step(qi, True)
    lax.fori_loop(0, qi, later_step, 0)
    for h in range(n_heads):
        add_values(0, h)
    for h in range(0, n_heads, 2):
        o_ref[0, :, pair_lanes(h)] = jnp.where(head0, acc_ref[h], acc_ref[h + 1]).astype(o_ref.dtype)


def _stick_breaking(h3, n_heads, col0, tq_target=256):
    bsz, s, _ = h3.shape
    tq = _tile(s, tq_target)
    pairs = n_heads // 2
    per_step = 2 if pairs % 2 == 0 and col0 % 2 == 0 else 1
    groups, lanes, first = pairs // per_step, per_step * LANES, col0 // per_step
    return pl.pallas_call(
        _sb_kernel,
        out_shape=jax.ShapeDtypeStruct((bsz, s, pairs * LANES), BF16),
        grid=(bsz, groups, s // tq),
        in_specs=[pl.BlockSpec((1, tq, lanes), lambda b, p, i: (b, i, first + p)),
                  pl.BlockSpec((1, s, lanes), lambda b, p, i: (b, 0, first + groups + p)),
                  pl.BlockSpec((1, s, lanes), lambda b, p, i: (b, 0, first + 2 * groups + p))],
        out_specs=pl.BlockSpec((1, tq, lanes), lambda b, p, i: (b, i, p)),
        scratch_shapes=[pltpu.VMEM((2 * per_step, tq, LANES), F32), pltpu.VMEM((2 * per_step, tq, tq), F32),
                        pltpu.VMEM((2 * per_step, tq, tq), BF16), pltpu.VMEM((2 * per_step, tq, 1), F32)],
        compiler_params=_params("parallel", "parallel", "arbitrary"),
        name="stick_breaking",
    )(h3, h3, h3)


def _ret_kernel(q_ref, k_ref, v_ref, gate_ref, cos_ref, sin_ref, din_ref, dq_ref, dk_ref,
                dc_ref, gn_ref, o_ref, state_ref):
    @pl.when(pl.program_id(2) == 0)
    def _():
        state_ref[...] = jnp.zeros_like(state_ref)

    lane = lax.broadcasted_iota(jnp.int32, (1, LANES), 1)
    head0 = lane < HEAD_DIM
    first_half = (lane & (HEAD_DIM // 2)) == 0
    row = lax.broadcasted_iota(jnp.int32, (LANES, LANES), 0) < HEAD_DIM
    col = lax.broadcasted_iota(jnp.int32, (LANES, LANES), 1) < HEAD_DIM
    same_head = row == col
    group_mean = jnp.where(same_head, 1.0 / HEAD_DIM, 0.0).astype(BF16)
    cos = cos_ref[...]
    sin = sin_ref[...]

    def rotary(x):
        half = HEAD_DIM // 2
        swapped = jnp.where(first_half, pltpu.roll(x, LANES - half, 1), pltpu.roll(x, half, 1))
        return x * cos + swapped * sin

    pairs = range(q_ref.shape[2] // LANES)
    lanes = lambda p: slice(p * LANES, (p + 1) * LANES)
    qb, kb, v, scores, cross, outer = [], [], [], [], [], []
    for p in pairs:
        q = rotary(q_ref[0, :, lanes(p)].astype(F32)) * (HEAD_DIM ** -0.5)
        k = rotary(k_ref[0, :, lanes(p)].astype(F32))
        v.append(v_ref[0, :, lanes(p)])
        qb.append(q.astype(BF16))
        kb.append(k.astype(BF16))
        scores.append([_dot_nt(jnp.where(mask, qb[p], jnp.zeros_like(qb[p])), kb[p])
                       for mask in (head0, jnp.logical_not(head0))])
        cross.append(_dot(qb[p], state_ref[p].astype(BF16)))
        outer.append(_dot((k * dk_ref[p]).T.astype(BF16), v[p]))
    intra = [[_dot((scores[p][h] * din_ref[2 * p + h]).astype(BF16), v[p]) for h in range(2)] for p in pairs]
    for p in pairs:
        state_ref[p] = state_ref[p] * dc_ref[p] + jnp.where(same_head, outer[p], 0.0)
        o = jnp.where(head0, intra[p][0], intra[p][1]) + cross[p] * dq_ref[p]
        mu = _split_dot(o, group_mean)
        d = o - mu
        var = _split_dot(d * d, group_mean)
        y = d * lax.rsqrt(var + LN_EPS) * gn_ref[:, lanes(p)]
        g = gate_ref[0, :, lanes(p)].astype(F32)
        o_ref[0, :, lanes(p)] = (g / (1.0 + jnp.exp(-g)) * y).astype(o_ref.dtype)


def _retention(h3, gn_g, n_heads, col0, chunk_target=256):
    bsz, s, _ = h3.shape
    c = _tile(s, chunk_target)
    pairs = n_heads // 2
    half = HEAD_DIM // 2

    pos = jnp.arange(s, dtype=F32)
    inv_freq = ROPE_BASE ** (-jnp.arange(half, dtype=F32) / half)
    ang = pos[:, None] * inv_freq[None, :]
    cos = jnp.tile(jnp.cos(ang), (1, LANES // half))
    sin = jnp.tile(jnp.concatenate([-jnp.sin(ang), jnp.sin(ang)], axis=1), (1, LANES // HEAD_DIM))

    gamma = 1.0 - 2.0 ** (-5.0 - jnp.arange(n_heads, dtype=F32))
    log_g = jnp.log(gamma)
    i = jnp.arange(c, dtype=F32)
    diff = i[:, None] - i[None, :]
    d_in = jnp.where(diff >= 0, jnp.exp(log_g[:, None, None] * jnp.maximum(diff, 0.0)), 0.0)
    per_lane = lambda t: jnp.repeat(t.reshape(pairs, 2, -1), HEAD_DIM, axis=1)
    d_q = jnp.swapaxes(per_lane(jnp.exp(log_g[:, None] * (i + 1.0))), 1, 2)
    d_k = jnp.swapaxes(per_lane(jnp.exp(log_g[:, None] * (c - 1.0 - i))), 1, 2)
    d_c = jnp.swapaxes(per_lane(jnp.exp(log_g * c)[:, None]), 1, 2)

    per_step = 2 if pairs % 2 == 0 and col0 % 2 == 0 else 1
    groups, lanes, first = pairs // per_step, per_step * LANES, col0 // per_step
    blk = lambda off: pl.BlockSpec((1, c, lanes), lambda b, p, t: (b, t, first + off * groups + p))
    return pl.pallas_call(
        _ret_kernel,
        out_shape=jax.ShapeDtypeStruct((bsz, s, pairs * LANES), BF16),
        grid=(bsz, groups, s // c),
        in_specs=[blk(0), blk(1), blk(2), blk(3),
                  pl.BlockSpec((c, LANES), lambda b, p, t: (t, 0)),
                  pl.BlockSpec((c, LANES), lambda b, p, t: (t, 0)),
                  pl.BlockSpec((2 * per_step, c, c), lambda b, p, t: (p, 0, 0)),
                  pl.BlockSpec((per_step, c, LANES), lambda b, p, t: (p, 0, 0)),
                  pl.BlockSpec((per_step, c, LANES), lambda b, p, t: (p, 0, 0)),
                  pl.BlockSpec((per_step, 1, LANES), lambda b, p, t: (p, 0, 0)),
                  pl.BlockSpec((1, lanes), lambda b, p, t: (0, p))],
        out_specs=pl.BlockSpec((1, c, lanes), lambda b, p, t: (b, t, p)),
        scratch_shapes=[pltpu.VMEM((per_step, LANES, LANES), F32)],
        compiler_params=_params("parallel", "parallel", "arbitrary"),
        name="retention",
    )(h3, h3, h3, h3, cos, sin, d_in, d_q, d_k, d_c, gn_g.reshape(1, -1))


def _chunk_attn_kernel(q_ref, k_ref, v_ref, bias_ref, o_ref):
    tq = q_ref.shape[1]
    t0 = pl.program_id(2) * tq
    n_pairs = q_ref.shape[2] // LANES
    lane = lax.broadcasted_iota(jnp.int32, (1, LANES), 1)
    head0 = lane < HEAD_DIM

    width = bias_ref.shape[3]
    start = pl.multiple_of(jnp.maximum(t0 - LEFT_CHUNKS * CHUNK, 0), tq)

    def pair_slice(ref, pair, rows):
        return ref[0, rows, pair * LANES:(pair + 1) * LANES]

    logits = []
    for pair in range(n_pairs):
        q = pair_slice(q_ref, pair, slice(None)) * jnp.asarray(HEAD_DIM ** -0.5, BF16)
        k = pair_slice(k_ref, pair, pl.ds(start, width))
        logits.append(_dot_nt(jnp.where(head0, q, jnp.zeros_like(q)), k))
        logits.append(_dot_nt(jnp.where(head0, jnp.zeros_like(q), q), k))
    for pair in range(n_pairs):
        probs, denoms = [], []
        for h in range(2):
            head = 2 * pair + h
            s = logits[head] + bias_ref[0, 0, head * tq:(head + 1) * tq, :]
            p = jnp.exp(s - jnp.max(s, axis=-1, keepdims=True))
            denoms.append(jnp.sum(p, axis=-1, keepdims=True))
            probs.append(p.astype(BF16))
        v = pair_slice(v_ref, pair, pl.ds(start, width))
        outs = [_dot(probs[h], v) / denoms[h] for h in range(2)]
        o_ref[0, :, pair * LANES:(pair + 1) * LANES] = jnp.where(head0, outs[0], outs[1]).astype(o_ref.dtype)


def _window_bias(rel_bias, tq, width):
    n_heads = rel_bias.shape[0]
    left = LEFT_CHUNKS * CHUNK
    rel_min, rel_max = left - (width - 1), left + tq - 1
    assert rel_min >= -MAX_REL and rel_max >= MAX_REL
    by_rel = jnp.concatenate(
        [rel_bias[:, rel_min + MAX_REL:], jnp.broadcast_to(rel_bias[:, -1:], (n_heads, rel_max - MAX_REL))], axis=1)
    m = rel_max - rel_min + 1
    rev = jnp.concatenate([by_rel[:, ::-1], jnp.zeros((n_heads, 1), by_rel.dtype)], axis=1)
    rows = jnp.tile(rev, (1, tq))[:, :tq * m].reshape(n_heads, tq, m)
    bias = rows[:, :, tq - 1:tq - 1 + width].astype(F32)
    q_chunk = jnp.arange(tq)[:, None] // CHUNK + LEFT_CHUNKS
    k_chunk = jnp.arange(width)[None, :] // CHUNK
    visible = (k_chunk >= q_chunk - LEFT_CHUNKS) & (k_chunk <= q_chunk)
    return jnp.where(visible[None], bias, NEG_BIG)


def _chunk_attention(qkv3, rel_bias, n_heads, tq=256):
    bsz, s, _ = qkv3.shape
    pairs = n_heads // 2
    n_shift = (LEFT_CHUNKS * CHUNK) // tq
    width = (n_shift + 1) * tq
    assert s >= width
    bias = _window_bias(rel_bias, tq, width)
    per_step = next(n for n in (4, 2, 1) if pairs % n == 0)
    groups, heads = pairs // per_step, 2 * per_step
    lanes = per_step * LANES
    bias = jnp.stack([jnp.pad(bias[:, :, v * tq:], ((0, 0), (0, 0), (0, v * tq)), constant_values=NEG_BIG)
                      for v in range(n_shift + 1)], axis=1).reshape(groups, heads, n_shift + 1, tq, width)
    bias = bias.transpose(0, 2, 1, 3, 4).reshape(groups, n_shift + 1, heads * tq, width)

    return pl.pallas_call(
        _chunk_attn_kernel,
        out_shape=jax.ShapeDtypeStruct((bsz, s, pairs * LANES), BF16),
        grid=(bsz, groups, s // tq),
        in_specs=[pl.BlockSpec((1, tq, lanes), lambda b, p, t: (b, t, p)),
                  pl.BlockSpec((1, s, lanes), lambda b, p, t: (b, 0, groups + p)),
                  pl.BlockSpec((1, s, lanes), lambda b, p, t: (b, 0, 2 * groups + p)),
                  pl.BlockSpec((1, 1, heads * tq, width), lambda b, p, t: (p, jnp.maximum(n_shift - t, 0), 0, 0))],
        out_specs=pl.BlockSpec((1, tq, lanes), lambda b, p, t: (b, t, p)),
        compiler_params=_params("parallel", "parallel", "arbitrary"),
        name="chunk_attention",
    )(qkv3, qkv3, qkv3, bias)


def _swiglu_ln_kernel(alpha, x_ref, wg_ref, wu_ref, wd_ref, g_ref, b_ref, o_ref, xb_ref, acc_ref):
    f = pl.program_id(1)

    @pl.when(f == 0)
    def _():
        xb_ref[...] = x_ref[...].astype(BF16)
        acc_ref[...] = jnp.zeros_like(acc_ref)

    xb = xb_ref[...]
    gate = _dot(xb, wg_ref[...])
    up = _dot(xb, wu_ref[...])
    hidden = (gate / (1.0 + jnp.exp(-gate)) * up).astype(BF16)
    acc_ref[...] += _dot(hidden, wd_ref[...])

    @pl.when(f == pl.num_programs(1) - 1)
    def _():
        o_ref[...] = _layer_norm(alpha * x_ref[...] + acc_ref[...], g_ref[...], b_ref[...])


def _swiglu_ln(x, wg, wu, wd, g, b, alpha, tm_target=512, tf_target=2816):
    t, d = x.shape
    ff = wg.shape[1]
    tm, tf = _tile(t, tm_target), _tile(ff, tf_target)
    once = dict(pipeline_mode=pl.Buffered(1)) if tf == ff else {}
    return pl.pallas_call(
        functools.partial(_swiglu_ln_kernel, alpha),
        out_shape=jax.ShapeDtypeStruct((t, d), F32),
        grid=(t // tm, ff // tf),
        in_specs=[pl.BlockSpec((tm, d), lambda i, f: (i, 0)),
                  pl.BlockSpec((d, tf), lambda i, f: (0, f), **once),
                  pl.BlockSpec((d, tf), lambda i, f: (0, f), **once),
                  pl.BlockSpec((tf, d), lambda i, f: (f, 0), **once),
                  pl.BlockSpec((1, d), lambda i, f: (0, 0)),
                  pl.BlockSpec((1, d), lambda i, f: (0, 0))],
        out_specs=pl.BlockSpec((tm, d), lambda i, f: (i, 0)),
        scratch_shapes=[pltpu.VMEM((tm, d), BF16), pltpu.VMEM((tm, d), F32)],
        compiler_params=_params("parallel", "arbitrary"),
        name="swiglu_ln",
    )(x, wg, wu, wd, g.reshape(1, d), b.reshape(1, d))


ROUTE_E1, ROUTE_E2, ROUTE_RANK1, ROUTE_RANK2 = range(4)
ROUTE_FIELDS = 8
MOE_TILE = 512
MOE_WINDOW = 256
ROW_ALIGN = 16
PAIRS_PER_STEP = 4
STEP_FIRST, STEP_LAST = 1, 2


def _pair_live(k):
    return 4 << (2 * k)


def _pair_narrow(k):
    return 8 << (2 * k)


def _route_kernel(n_experts, x_ref, w_ref, b_ref, col_ref, row_ref, gate_ref, cnt_ref, run_ref):
    @pl.when(pl.program_id(0) == 0)
    def _():
        run_ref[...] = jnp.zeros_like(run_ref)

    x = x_ref[...]
    w = w_ref[...]
    tm = x.shape[0]
    x_hi = x.astype(BF16)
    x_lo = (x - x_hi.astype(F32)).astype(BF16)
    w_hi = w.astype(BF16)
    w_lo = (w - w_hi.astype(F32)).astype(BF16)
    logits = _dot(x_hi, w_hi) + (_dot(x_hi, w_lo) + _dot(x_lo, w_hi)) + b_ref[...]
    lane = lax.broadcasted_iota(jnp.int32, logits.shape, 1).astype(F32)
    neg_inf = jnp.asarray(-jnp.inf, F32)
    logits = jnp.where(lane < n_experts, logits, neg_inf)
    top1 = jnp.max(logits, axis=-1, keepdims=True)
    idx1 = jnp.min(jnp.where(logits == top1, lane, float(LANES)), axis=-1, keepdims=True)
    rest = jnp.where(lane == idx1, neg_inf, logits)
    top2 = jnp.max(rest, axis=-1, keepdims=True)
    idx2 = jnp.min(jnp.where(rest == top2, lane, float(LANES)), axis=-1, keepdims=True)
    e2 = jnp.exp(top2 - top1)
    denom = 1.0 + e2

    chosen = jnp.where(jnp.logical_or(lane == idx1, lane == idx2), 1.0, 0.0)
    r = lax.broadcasted_iota(jnp.int32, (tm, tm), 0)
    c = lax.broadcasted_iota(jnp.int32, (tm, tm), 1)
    earlier = (c < r).astype(BF16)
    before = _dot(earlier, chosen.astype(BF16)) + run_ref[...]
    rank1 = jnp.sum(jnp.where(lane == idx1, before, 0.0), axis=-1, keepdims=True)
    rank2 = jnp.sum(jnp.where(lane == idx2, before, 0.0), axis=-1, keepdims=True)
    total = jnp.sum(chosen, axis=0, keepdims=True)
    cnt_ref[0] = total
    run_ref[...] += total

    fields = {ROUTE_E1: idx1, ROUTE_E2: idx2, ROUTE_RANK1: rank1, ROUTE_RANK2: rank2}
    meta = jnp.zeros_like(logits)
    for field, value in fields.items():
        meta = jnp.where(lane == field, value, meta)
    col_ref[...] = meta
    row_ref[0] = meta.T[0:ROUTE_FIELDS, :]
    gates = jnp.where(lane == idx1, 1.0 / denom, jnp.where(lane == idx2, e2 / denom, 0.0))
    gates_hi = gates.astype(BF16)
    gate_ref[:, 0:LANES] = gates_hi
    gate_ref[:, LANES:2 * LANES] = (gates - gates_hi.astype(F32)).astype(BF16)


def _route(x, w_router, b_router, tm):
    t, d = x.shape
    n_experts = w_router.shape[1]
    nb = t // tm
    w = jnp.pad(w_router.astype(F32), ((0, 0), (0, LANES - n_experts)))
    b = jnp.pad(b_router.astype(F32), (0, LANES - n_experts)).reshape(1, LANES)
    return pl.pallas_call(
        functools.partial(_route_kernel, n_experts),
        out_shape=[jax.ShapeDtypeStruct((t, LANES), F32),
                   jax.ShapeDtypeStruct((nb, ROUTE_FIELDS, tm), F32),
                   jax.ShapeDtypeStruct((t, 2 * LANES), BF16),
                   jax.ShapeDtypeStruct((nb, 1, LANES), F32)],
        grid=(nb,),
        in_specs=[pl.BlockSpec((tm, d), lambda i: (i, 0)),
                  pl.BlockSpec((d, LANES), lambda i: (0, 0)),
                  pl.BlockSpec((1, LANES), lambda i: (0, 0))],
        out_specs=[pl.BlockSpec((tm, LANES), lambda i: (i, 0)),
                   pl.BlockSpec((1, ROUTE_FIELDS, tm), lambda i: (i, 0, 0)),
                   pl.BlockSpec((tm, 2 * LANES), lambda i: (i, 0)),
                   pl.BlockSpec((1, 1, LANES), lambda i: (i, 0, 0))],
        scratch_shapes=[pltpu.VMEM((1, LANES), F32)],
        compiler_params=_params("arbitrary"),
        name="route",
    )(x, w, b)


def _moe_plan(cnt, n_experts, tile, n_tiles):
    nb = cnt.shape[0]
    counts = cnt[:, 0, :n_experts].astype(jnp.int32)
    cum = jnp.cumsum(counts, axis=0) - counts
    total = jnp.sum(counts, axis=0)
    tiles_e = (total + tile - 1) // tile
    tile_end = jnp.cumsum(tiles_e)
    row_off = (tile_end - tiles_e) * tile
    start = row_off[None, :] + cum
    j0 = start // tile
    j1 = (start + counts - 1) // tile
    has = counts > 0
    tiles = jnp.stack([j0, j0 + 1], axis=-1)
    valid = jnp.stack([has, has & (j1 > j0)], axis=-1)
    lo = jnp.clip(start[..., None] - tiles * tile, 0, tile)
    hi = jnp.clip((start + counts)[..., None] - tiles * tile, 0, tile)
    win = min(MOE_WINDOW, tile)
    window = jnp.minimum(lo // ROW_ALIGN * ROW_ALIGN, tile - win)
    narrow = hi <= window + win

    n_pairs = nb * n_experts + n_tiles
    slot = jnp.arange(n_pairs)
    block_id = jnp.broadcast_to(jnp.arange(nb)[:, None, None], tiles.shape)

    def listed(order, group_key, n_runs):
        flat = lambda a: a.transpose(order).reshape(-1)
        n_live = jnp.sum(valid)
        idx = jnp.nonzero(flat(valid), size=n_pairs, fill_value=0)[0]
        idx = jnp.where(slot < n_live, idx, idx[n_live - 1])
        pair = {name: flat(a)[idx].astype(jnp.int32)
                for name, a in (("tile", tiles), ("block", block_id), ("window", window), ("narrow", narrow))}
        key = pair[group_key]
        live = slot < n_live
        first = live & (key != jnp.concatenate([jnp.full((1,), -1, key.dtype), key[:-1]]))
        last = live & ((slot == n_live - 1) | (key != jnp.concatenate([key[1:], key[-1:]])))
        run_start = lax.cummax(jnp.where(first, slot, 0), axis=0)
        opens = live & ((slot - run_start) % PAIRS_PER_STEP == 0)
        n_steps = (n_pairs + (PAIRS_PER_STEP - 1) * n_runs) // PAIRS_PER_STEP + 1
        n_open = jnp.sum(opens)
        step = jnp.arange(n_steps)
        lead = jnp.nonzero(opens, size=n_steps, fill_value=0)[0]
        lead = jnp.where(step < n_open, lead, lead[n_open - 1])
        steps = {"tile": [], "block": [], "window": []}
        flag = STEP_FIRST * ((step < n_open) & first[lead])
        ends_run = jnp.zeros_like(step < n_open)
        for k in range(PAIRS_PER_STEP):
            nxt = jnp.minimum(lead + k, n_pairs - 1)
            live_k = (step < n_open) & (lead + k < n_pairs) & live[nxt] & (key[nxt] == key[lead])
            at = jnp.where(live_k, nxt, lead)
            for name in steps:
                steps[name].append(pair[name][at])
            flag = flag + _pair_live(k) * live_k + _pair_narrow(k) * (live_k & (pair["narrow"][at] != 0))
            ends_run = ends_run | (live_k & last[at])
        steps["flag"] = (flag + STEP_LAST * ends_run).astype(jnp.int32)
        return steps

    by_expert = listed((1, 0, 2), "tile", n_tiles)
    by_block = listed((0, 1, 2), "block", nb)
    n_used = tile_end[-1:].astype(jnp.int32)
    tile_expert = jnp.minimum(jnp.searchsorted(tile_end, jnp.arange(n_tiles), side="right"),
                              n_experts - 1).astype(jnp.int32)
    return dict(by_expert=by_expert, by_block=by_block, row_off=row_off.astype(jnp.int32),
                n_used=n_used, tile_expert=tile_expert)


def _with_positions(meta, row_off, field_axis):
    def take(first):
        return lax.slice_in_dim(meta, first, first + TOP_K, axis=field_axis)

    expert, offset = take(ROUTE_E1), jnp.zeros_like(take(ROUTE_E1))
    for e in range(row_off.shape[0]):
        offset = jnp.where(expert == float(e), row_off[e].astype(F32), offset)
    n_fields = meta.shape[field_axis]
    return jnp.concatenate([take(ROUTE_E1), take(ROUTE_RANK1) + offset,
                            lax.slice_in_dim(meta, ROUTE_RANK1 + TOP_K, n_fields, axis=field_axis)], axis=field_axis)


def _gather_kernel(tile_ref, flag_ref, *refs):
    n = PAIRS_PER_STEP
    window_refs = refs[n:2 * n]
    inputs = refs[2 * n:5 * n]
    xs_ref, gs_ref, acc_ref = refs[5 * n:]
    step = pl.program_id(0)
    flag = flag_ref[step]
    tile, tm = xs_ref.shape[0], inputs[2].shape[0]

    def place_pair(row_ref, gate_ref, x_ref, window_ref, narrow):
        pos1 = row_ref[0, ROUTE_RANK1:ROUTE_RANK1 + 1, :].astype(jnp.int32)
        pos2 = row_ref[0, ROUTE_RANK2:ROUTE_RANK2 + 1, :].astype(jnp.int32)

        def place(row0, n_rows):
            rows = tile_ref[step] * tile + row0 + lax.broadcasted_iota(jnp.int32, (n_rows, tm), 0)
            onehot = jnp.where(jnp.logical_or(rows == pos1, rows == pos2), 1.0, 0.0).astype(BF16)
            acc_ref[pl.ds(row0, n_rows), :] += _dot(onehot, x_ref[...])
            gs_ref[pl.ds(row0, n_rows), :] += _dot(onehot, gate_ref[...])

        @pl.when(narrow)
        def _():
            place(pl.multiple_of(window_ref[step], ROW_ALIGN), min(MOE_WINDOW, tile))

        @pl.when(jnp.logical_not(narrow))
        def _():
            place(0, tile)

    @pl.when((flag & STEP_FIRST) != 0)
    def _():
        acc_ref[...] = jnp.zeros_like(acc_ref)
        gs_ref[...] = jnp.zeros_like(gs_ref)

    for k in range(n):
        @pl.when((flag & _pair_live(k)) != 0)
        def _(k=k):
            place_pair(*inputs[3 * k:3 * k + 3], window_refs[k], (flag & _pair_narrow(k)) != 0)

    @pl.when((flag & STEP_LAST) != 0)
    def _():
        xs_ref[...] = acc_ref[...].astype(BF16)


def _moe_gather(plan, pos_row, gates, xb, tile, n_tiles):
    t, d = xb.shape
    _, _, tm = pos_row.shape
    steps = plan["by_expert"]

    def token_block(k):
        return [pl.BlockSpec((1, ROUTE_FIELDS, tm), lambda p, *s: (s[2 + k][p], 0, 0)),
                pl.BlockSpec((tm, 2 * LANES), lambda p, *s: (s[2 + k][p], 0)),
                pl.BlockSpec((tm, d), lambda p, *s: (s[2 + k][p], 0))]

    return pl.pallas_call(
        _gather_kernel,
        out_shape=[jax.ShapeDtypeStruct((n_tiles * tile, d), BF16),
                   jax.ShapeDtypeStruct((n_tiles * tile, 2 * LANES), F32)],
        grid_spec=pltpu.PrefetchScalarGridSpec(
            num_scalar_prefetch=2 + 2 * PAIRS_PER_STEP,
            grid=(steps["flag"].shape[0],),
            in_specs=[spec for k in range(PAIRS_PER_STEP) for spec in token_block(k)],
            out_specs=[pl.BlockSpec((tile, d), lambda p, tiles, *_: (tiles[p], 0)),
                       pl.BlockSpec((tile, 2 * LANES), lambda p, tiles, *_: (tiles[p], 0))],
            scratch_shapes=[pltpu.VMEM((tile, d), F32)]),
        compiler_params=_params("arbitrary"),
        name="moe_gather",
    )(steps["tile"][0], steps["flag"], *steps["block"], *steps["window"],
      *([pos_row, gates, xb] * PAIRS_PER_STEP))


def _moe_ffn_kernel(te_ref, used_ref, xs_ref, gs_ref, wg_ref, wu_ref, wd_ref, o_ref, acc_ref):
    f = pl.program_id(1)
    tile_id = pl.program_id(0)

    @pl.when(tile_id < used_ref[0])
    def _():
        @pl.when(f == 0)
        def _():
            acc_ref[...] = jnp.zeros_like(acc_ref)

        xs = xs_ref[...]
        gate = _dot(xs, wg_ref[0])
        up = _dot(xs, wu_ref[0])
        hidden = (gate / (1.0 + jnp.exp(-gate)) * up).astype(BF16)
        acc_ref[...] += _dot(hidden, wd_ref[0])

        @pl.when(f == pl.num_programs(1) - 1)
        def _():
            by_expert = gs_ref[:, 0:LANES] + gs_ref[:, LANES:2 * LANES]
            lane = lax.broadcasted_iota(jnp.int32, by_expert.shape, 1)
            row_gate = jnp.sum(jnp.where(lane == te_ref[tile_id], by_expert, 0.0), axis=1, keepdims=True)
            o_ref[...] = (acc_ref[...] * row_gate).astype(o_ref.dtype)


def _moe_ffn(plan, xs, gs, wg, wu, wd, tile, tf_target=1792):
    rows, d = xs.shape
    ff = wg.shape[2]
    tf = _tile(ff, tf_target)
    nf = ff // tf

    def tile_of(i, used):
        return jnp.minimum(i, used[0] - 1)

    def chunk_of(i, f, used):
        return jnp.where(i < used[0], f, nf - 1)

    return pl.pallas_call(
        _moe_ffn_kernel,
        out_shape=jax.ShapeDtypeStruct((rows, d), BF16),
        grid_spec=pltpu.PrefetchScalarGridSpec(
            num_scalar_prefetch=2,
            grid=(rows // tile, nf),
            in_specs=[pl.BlockSpec((tile, d), lambda i, f, te, used: (tile_of(i, used), 0)),
                      pl.BlockSpec((tile, 2 * LANES), lambda i, f, te, used: (tile_of(i, used), 0)),
                      pl.BlockSpec((1, d, tf), lambda i, f, te, used: (te[tile_of(i, used)], 0, chunk_of(i, f, used))),
                      pl.BlockSpec((1, d, tf), lambda i, f, te, used: (te[tile_of(i, used)], 0, chunk_of(i, f, used))),
                      pl.BlockSpec((1, tf, d), lambda i, f, te, used: (te[tile_of(i, used)], chunk_of(i, f, used), 0))],
            out_specs=pl.BlockSpec((tile, d), lambda i, f, te, used: (tile_of(i, used), 0)),
            scratch_shapes=[pltpu.VMEM((tile, d), F32)]),
        compiler_params=_params("arbitrary", "arbitrary"),
        name="moe_ffn",
    )(plan["tile_expert"], plan["n_used"], xs, gs, wg, wu, wd)


def _combine_ln_kernel(alpha, block_ref, flag_ref, *refs):
    n = PAIRS_PER_STEP
    tile_refs, window_refs = refs[0:n], refs[n:2 * n]
    col_ref = refs[2 * n]
    os_refs = refs[2 * n + 1:3 * n + 1]
    x_ref, g_ref, b_ref, o_ref, acc_ref = refs[3 * n + 1:]
    step = pl.program_id(0)
    flag = flag_ref[step]
    tm, tile = x_ref.shape[0], os_refs[0].shape[0]

    def collect_pair(os_ref, tile_ref, window_ref, narrow):
        pos1 = col_ref[:, ROUTE_RANK1:ROUTE_RANK1 + 1].astype(jnp.int32)
        pos2 = col_ref[:, ROUTE_RANK2:ROUTE_RANK2 + 1].astype(jnp.int32)

        def collect(row0, n_rows):
            cols = tile_ref[step] * tile + row0 + lax.broadcasted_iota(jnp.int32, (tm, n_rows), 1)
            onehot = jnp.where(jnp.logical_or(cols == pos1, cols == pos2), 1.0, 0.0).astype(BF16)
            acc_ref[...] += _dot(onehot, os_ref[pl.ds(row0, n_rows), :])

        @pl.when(narrow)
        def _():
            collect(pl.multiple_of(window_ref[step], ROW_ALIGN), min(MOE_WINDOW, tile))

        @pl.when(jnp.logical_not(narrow))
        def _():
            collect(0, tile)

    @pl.when((flag & STEP_FIRST) != 0)
    def _():
        acc_ref[...] = jnp.zeros_like(acc_ref)

    for k in range(n):
        @pl.when((flag & _pair_live(k)) != 0)
        def _(k=k):
            collect_pair(os_refs[k], tile_refs[k], window_refs[k], (flag & _pair_narrow(k)) != 0)

    @pl.when((flag & STEP_LAST) != 0)
    def _():
        o_ref[...] = _layer_norm(alpha * x_ref[...] + acc_ref[...], g_ref[...], b_ref[...])


def _moe_combine_ln(plan, pos_col, o_sorted, x, g, b, alpha, tile, tm):
    t, d = x.shape
    steps = plan["by_block"]

    def sorted_tile(k):
        return pl.BlockSpec((tile, d), lambda p, *s: (s[2 + k][p], 0))

    return pl.pallas_call(
        functools.partial(_combine_ln_kernel, alpha),
        out_shape=jax.ShapeDtypeStruct((t, d), F32),
        grid_spec=pltpu.PrefetchScalarGridSpec(
            num_scalar_prefetch=2 + 2 * PAIRS_PER_STEP,
            grid=(steps["flag"].shape[0],),
            in_specs=[pl.BlockSpec((tm, LANES), lambda p, blocks, *_: (blocks[p], 0))]
            + [sorted_tile(k) for k in range(PAIRS_PER_STEP)]
            + [pl.BlockSpec((tm, d), lambda p, blocks, *_: (blocks[p], 0)),
               pl.BlockSpec((1, d), lambda p, *_: (0, 0)),
               pl.BlockSpec((1, d), lambda p, *_: (0, 0))],
            out_specs=pl.BlockSpec((tm, d), lambda p, blocks, *_: (blocks[p], 0)),
            scratch_shapes=[pltpu.VMEM((tm, d), F32)]),
        compiler_params=_params("arbitrary"),
        name="moe_combine_ln",
    )(steps["block"][0], steps["flag"], *steps["tile"], *steps["window"],
      pos_col, *([o_sorted] * PAIRS_PER_STEP), x, g.reshape(1, d), b.reshape(1, d))


def _moe_ln(x, xb, w_router, b_router, wg, wu, wd, g, b, alpha):
    t, _ = x.shape
    n_experts = wg.shape[0]
    tile = _tile(t, MOE_TILE)
    n_tiles = TOP_K * t // tile + n_experts
    meta_col, meta_row, gates, cnt = _route(x, w_router, b_router, tile)
    plan = _moe_plan(cnt, n_experts, tile, n_tiles)
    pos_row = _with_positions(meta_row, plan["row_off"], 1)
    pos_col = _with_positions(meta_col, plan["row_off"], 1)
    xs, gs = _moe_gather(plan, pos_row, gates, xb, tile, n_tiles)
    o_sorted = _moe_ffn(plan, xs, gs, wg, wu, wd, tile)
    return _moe_combine_ln(plan, pos_col, o_sorted, x, g, b, alpha, tile, tile)


def _even_layer(x, bsz, s, alpha, ln1_g, ln1_b, w_in, ret_gn_g, w_out, ln2_g, ln2_b, w_gate, w_up, w_down):
    d = x.shape[1]
    n_heads = d // HEAD_DIM
    n_sb = n_heads // 2
    n_ret = n_heads - n_sb
    h = _proj(x, w_in.astype(BF16))
    h3 = h.reshape(bsz, s, h.shape[1])
    o_sb = _stick_breaking(h3, n_sb, 0)
    o_ret = _retention(h3, ret_gn_g, n_ret, 3 * n_sb * HEAD_DIM // LANES)
    x = _out_ln([o_sb.reshape(bsz * s, -1), o_ret.reshape(bsz * s, -1)], w_out.astype(BF16),
                x, ln1_g, ln1_b, alpha)
    return _swiglu_ln(x, w_gate.astype(BF16), w_up.astype(BF16), w_down.astype(BF16), ln2_g, ln2_b, alpha)


def _odd_layer(x, bsz, s, alpha, ln1_g, ln1_b, w_qkv, rel_bias, w_out, ln2_g, ln2_b, w_router, b_router,
               w_gate, w_up, w_down):
    d = x.shape[1]
    qkv = _proj(x, w_qkv.astype(BF16))
    o = _chunk_attention(qkv.reshape(bsz, s, 3 * d), rel_bias, d // HEAD_DIM)
    x, xb = _out_ln([o.reshape(bsz * s, d)], w_out.astype(BF16), x, ln1_g, ln1_b, alpha, also_bf16=True)
    return _moe_ln(x, xb, w_router, b_router, w_gate.astype(BF16), w_up.astype(BF16), w_down.astype(BF16),
                   ln2_g, ln2_b, alpha)


def kernel(x, even_ln1_g, even_ln1_b, even_w_in, even_ret_gn_g, even_w_out, even_ln2_g, even_ln2_b,
           even_w_gate, even_w_up, even_w_down, odd_ln1_g, odd_ln1_b, odd_w_qkv, odd_rel_bias, odd_w_out,
           odd_ln2_g, odd_ln2_b, odd_w_router, odd_b_router, odd_w_gate, odd_w_up, odd_w_down):
    bsz, s, d = x.shape
    depth = even_w_in.shape[0] + odd_w_qkv.shape[0]
    alpha = (2 * depth) ** 0.25
    xt = x.reshape(bsz * s, d)
    for layer in range(depth):
        i = layer // 2
        if layer % 2 == 0:
            xt = _even_layer(xt, bsz, s, alpha, even_ln1_g[i], even_ln1_b[i], even_w_in[i], even_ret_gn_g[i],
                             even_w_out[i], even_ln2_g[i], even_ln2_b[i], even_w_gate[i], even_w_up[i],
                             even_w_down[i])
        else:
            xt = _odd_layer(xt, bsz, s, alpha, odd_ln1_g[i], odd_ln1_b[i], odd_w_qkv[i], odd_rel_bias[i],
                            odd_w_out[i], odd_ln2_g[i], odd_ln2_b[i], odd_w_router[i], odd_b_router[i],
                            odd_w_gate[i], odd_w_up[i], odd_w_down[i])
    return xt.reshape(bsz, s, d)
```

```python
import functools

import jax
import jax.numpy as jnp
from jax import lax
from jax.experimental import pallas as pl
from jax.experimental.pallas import tpu as pltpu

HEAD_DIM = 64
LANES = 128
CHUNK = 64
LEFT_CHUNKS = 8
MAX_REL = 4 * CHUNK
TOP_K = 2
ROPE_BASE = 10000.0
LN_EPS = 1e-5
NEG_BIG = -1e30
VMEM_LIMIT = 56 * 1024 * 1024

BF16 = jnp.bfloat16
F32 = jnp.float32


def _params(*sem):
    return pltpu.CompilerParams(dimension_semantics=sem, vmem_limit_bytes=VMEM_LIMIT)


def _tile(n, target):
    if n <= target:
        return n
    t = target - target % LANES
    while t >= LANES:
        if n % t == 0:
            return t
        t -= LANES
    return n


def _dot(a, b):
    return jnp.dot(a, b, preferred_element_type=F32)


def _dot_nt(a, b):
    return lax.dot_general(a, b, (((1,), (1,)), ((), ())), preferred_element_type=F32)


def _split_dot(a, b_bf16):
    hi = a.astype(BF16)
    lo = (a - hi.astype(F32)).astype(BF16)
    return _dot(hi, b_bf16) + _dot(lo, b_bf16)


def _layer_norm(r, g, b):
    mu = jnp.mean(r, axis=-1, keepdims=True)
    d = r - mu
    var = jnp.mean(d * d, axis=-1, keepdims=True)
    return d * lax.rsqrt(var + LN_EPS) * g + b


def _proj_kernel(x_ref, w_ref, o_ref, xb_ref):
    @pl.when(pl.program_id(1) == 0)
    def _():
        xb_ref[...] = x_ref[...].astype(BF16)

    o_ref[...] = _dot(xb_ref[...], w_ref[...]).astype(o_ref.dtype)


def _proj(x, w, tm_target=1024, tn_target=1792):
    t, k = x.shape
    n = w.shape[1]
    tm, tn = _tile(t, tm_target), _tile(n, tn_target)
    return pl.pallas_call(
        _proj_kernel,
        out_shape=jax.ShapeDtypeStruct((t, n), BF16),
        grid=(t // tm, n // tn),
        in_specs=[pl.BlockSpec((tm, k), lambda i, j: (i, 0)),
                  pl.BlockSpec((k, tn), lambda i, j: (0, j))],
        out_specs=pl.BlockSpec((tm, tn), lambda i, j: (i, j)),
        scratch_shapes=[pltpu.VMEM((tm, k), BF16)],
        compiler_params=_params("parallel", "arbitrary"),
        name="proj",
    )(x, w)


def _out_ln_kernel(alpha, n_a, *refs):
    a_refs = refs[:n_a]
    w_ref, x_ref, g_ref, b_ref = refs[n_a:n_a + 4]
    o_refs = refs[n_a + 4:]
    ka = a_refs[0].shape[1]
    y = _dot(a_refs[0][...], w_ref[0:ka, :])
    for idx in range(1, n_a):
        y = y + _dot(a_refs[idx][...], w_ref[idx * ka:(idx + 1) * ka, :])
    out = _layer_norm(alpha * x_ref[...] + y, g_ref[...], b_ref[...])
    for o_ref in o_refs:
        o_ref[...] = out.astype(o_ref.dtype)


def _out_ln(a_list, w, x, g, b, alpha, also_bf16=False, tm_target=1024):
    t, d = x.shape
    tm = _tile(t, tm_target)
    ka = a_list[0].shape[1]
    row = lambda i: (i, 0)
    fixed = lambda i: (0, 0)
    dtypes = [F32, BF16] if also_bf16 else [F32]
    outs = pl.pallas_call(
        functools.partial(_out_ln_kernel, alpha, len(a_list)),
        out_shape=[jax.ShapeDtypeStruct((t, d), dt) for dt in dtypes],
        grid=(t // tm,),
        in_specs=[pl.BlockSpec((tm, ka), row) for _ in a_list]
        + [pl.BlockSpec(w.shape, fixed), pl.BlockSpec((tm, d), row),
           pl.BlockSpec((1, d), fixed), pl.BlockSpec((1, d), fixed)],
        out_specs=[pl.BlockSpec((tm, d), row) for _ in dtypes],
        compiler_params=_params("parallel"),
        name="out_ln",
    )(*a_list, w, x, g.reshape(1, d), b.reshape(1, d))
    return outs if also_bf16 else outs[0]


def _sb_kernel(q_ref, k_ref, v_ref, o_ref, acc_ref, z_ref, w_ref, carry_ref):
    tq = q_ref.shape[1]
    n_heads = 2 * (q_ref.shape[2] // LANES)
    qi = pl.program_id(2)
    lane = lax.broadcasted_iota(jnp.int32, (1, LANES), 1)
    head0 = lane < HEAD_DIM
    row = lax.broadcasted_iota(jnp.int32, (tq, tq), 0)
    col = lax.broadcasted_iota(jnp.int32, (tq, tq), 1)
    after = (row > col).astype(BF16)
    strictly_causal = col < row

    def pair_lanes(h):
        return slice((h // 2) * LANES, (h // 2 + 1) * LANES)

    def masked_q(h):
        q = q_ref[0, :, pair_lanes(h)] * jnp.asarray(HEAD_DIM ** -0.5, BF16)
        return jnp.where(head0, q, jnp.zeros_like(q)) if h % 2 == 0 else jnp.where(head0, jnp.zeros_like(q), q)

    q_heads = [masked_q(h) for h in range(n_heads)]

    def key_block(ref, kb, h):
        return ref[0, pl.ds(pl.multiple_of(kb * tq, tq), tq), pair_lanes(h)]

    def add_values(kb, h):
        acc_ref[h] += _dot(w_ref[h], key_block(v_ref, kb, h))

    def step(kb, diagonal=False):
        log_betas, log_1ms, suffixes = [], [], []
        kb_next = jnp.maximum(kb - 1, 0)
        for h in range(n_heads):
            z = z_ref[h]
            neg_z = -z
            log_1m = jnp.minimum(neg_z, 0.0) - jnp.log(1.0 + jnp.exp(jnp.minimum(z, neg_z)))
            log_betas.append(z + log_1m)
            if diagonal:
                log_1m = jnp.where(strictly_causal, log_1m, 0.0)
            log_1ms.append(log_1m[:, 0:1])
            suffixes.append(_dot(log_1m.astype(BF16), after))
            z_ref[h] = _dot_nt(q_heads[h], key_block(k_ref, kb_next, h))
            if not diagonal:
                add_values(kb + 1, h)
        for h in range(n_heads):
            carry = carry_ref[h]
            w = jnp.exp(log_betas[h] + suffixes[h] + carry)
            if diagonal:
                w = jnp.where(strictly_causal, w, 0.0)
            w_ref[h] = w.astype(BF16)
            carry_ref[h] = carry + suffixes[h][:, 0:1] + log_1ms[h]

    def later_step(it, _):
        step(qi - 1 - it)
        return 0

    for h in range(n_heads):
        z_ref[h] = _dot_nt(q_heads[h], key_block(k_ref, qi, h))
    acc_ref[...] = jnp.zeros_like(acc_ref)
    carry_ref[...] = jnp.zeros_like(carry_ref)
    step(qi, True)
    lax.fori_loop(0, qi, later_step, 0)
    for h in range(n_heads):
        add_values(0, h)
    for h in range(0, n_heads, 2):
        o_ref[0, :, pair_lanes(h)] = jnp.where(head0, acc_ref[h], acc_ref[h + 1]).astype(o_ref.dtype)


def _stick_breaking(h3, n_heads, col0, tq_target=256):
    bsz, s, _ = h3.shape
    tq = _tile(s, tq_target)
    pairs = n_heads // 2
    per_step = 2 if pairs % 2 == 0 and col0 % 2 == 0 else 1
    groups, lanes, first = pairs // per_step, per_step * LANES, col0 // per_step
    return pl.pallas_call(
        _sb_kernel,
        out_shape=jax.ShapeDtypeStruct((bsz, s, pairs * LANES), BF16),
        grid=(bsz, groups, s // tq),
        in_specs=[pl.BlockSpec((1, tq, lanes), lambda b, p, i: (b, i, first + p)),
                  pl.BlockSpec((1, s, lanes), lambda b, p, i: (b, 0, first + groups + p)),
                  pl.BlockSpec((1, s, lanes), lambda b, p, i: (b, 0, first + 2 * groups + p))],
        out_specs=pl.BlockSpec((1, tq, lanes), lambda b, p, i: (b, i, p)),
        scratch_shapes=[pltpu.VMEM((2 * per_step, tq, LANES), F32), pltpu.VMEM((2 * per_step, tq, tq), F32),
                        pltpu.VMEM((2 * per_step, tq, tq), BF16), pltpu.VMEM((2 * per_step, tq, 1), F32)],
        compiler_params=_params("parallel", "parallel", "arbitrary"),
        name="stick_breaking",
    )(h3, h3, h3)


def _ret_kernel(q_ref, k_ref, v_ref, gate_ref, cos_ref, sin_ref, din_ref, dq_ref, dk_ref,
                dc_ref, gn_ref, o_ref, state_ref):
    @pl.when(pl.program_id(2) == 0)
    def _():
        state_ref[...] = jnp.zeros_like(state_ref)

    lane = lax.broadcasted_iota(jnp.int32, (1, LANES), 1)
    head0 = lane < HEAD_DIM
    first_half = (lane & (HEAD_DIM // 2)) == 0
    row = lax.broadcasted_iota(jnp.int32, (LANES, LANES), 0) < HEAD_DIM
    col = lax.broadcasted_iota(jnp.int32, (LANES, LANES), 1) < HEAD_DIM
    same_head = row == col
    group_mean = jnp.where(same_head, 1.0 / HEAD_DIM, 0.0).astype(BF16)
    cos = cos_ref[...]
    sin = sin_ref[...]

    def rotary(x):
        half = HEAD_DIM // 2
        swapped = jnp.where(first_half, pltpu.roll(x, LANES - half, 1), pltpu.roll(x, half, 1))
        return x * cos + swapped * sin

    pairs = range(q_ref.shape[2] // LANES)
    lanes = lambda p: slice(p * LANES, (p + 1) * LANES)
    qb, kb, v, scores, cross, outer = [], [], [], [], [], []
    for p in pairs:
        q = rotary(q_ref[0, :, lanes(p)].astype(F32)) * (HEAD_DIM ** -0.5)
        k = rotary(k_ref[0, :, lanes(p)].astype(F32))
        v.append(v_ref[0, :, lanes(p)])
        qb.append(q.astype(BF16))
        kb.append(k.astype(BF16))
        scores.append([_dot_nt(jnp.where(mask, qb[p], jnp.zeros_like(qb[p])), kb[p])
                       for mask in (head0, jnp.logical_not(head0))])
        cross.append(_dot(qb[p], state_ref[p].astype(BF16)))
        outer.append(_dot((k * dk_ref[p]).T.astype(BF16), v[p]))
    intra = [[_dot((scores[p][h] * din_ref[2 * p + h]).astype(BF16), v[p]) for h in range(2)] for p in pairs]
    for p in pairs:
        state_ref[p] = state_ref[p] * dc_ref[p] + jnp.where(same_head, outer[p], 0.0)
        o = jnp.where(head0, intra[p][0], intra[p][1]) + cross[p] * dq_ref[p]
        mu = _split_dot(o, group_mean)
        d = o - mu
        var = _split_dot(d * d, group_mean)
        y = d * lax.rsqrt(var + LN_EPS) * gn_ref[:, lanes(p)]
        g = gate_ref[0, :, lanes(p)].astype(F32)
        o_ref[0, :, lanes(p)] = (g / (1.0 + jnp.exp(-g)) * y).astype(o_ref.dtype)


def _retention(h3, gn_g, n_heads, col0, chunk_target=256):
    bsz, s, _ = h3.shape
    c = _tile(s, chunk_target)
    pairs = n_heads // 2
    half = HEAD_DIM // 2

    pos = jnp.arange(s, dtype=F32)
    inv_freq = ROPE_BASE ** (-jnp.arange(half, dtype=F32) / half)
    ang = pos[:, None] * inv_freq[None, :]
    cos = jnp.tile(jnp.cos(ang), (1, LANES // half))
    sin = jnp.tile(jnp.concatenate([-jnp.sin(ang), jnp.sin(ang)], axis=1), (1, LANES // HEAD_DIM))

    gamma = 1.0 - 2.0 ** (-5.0 - jnp.arange(n_heads, dtype=F32))
    log_g = jnp.log(gamma)
    i = jnp.arange(c, dtype=F32)
    diff = i[:, None] - i[None, :]
    d_in = jnp.where(diff >= 0, jnp.exp(log_g[:, None, None] * jnp.maximum(diff, 0.0)), 0.0)
    per_lane = lambda t: jnp.repeat(t.reshape(pairs, 2, -1), HEAD_DIM, axis=1)
    d_q = jnp.swapaxes(per_lane(jnp.exp(log_g[:, None] * (i + 1.0))), 1, 2)
    d_k = jnp.swapaxes(per_lane(jnp.exp(log_g[:, None] * (c - 1.0 - i))), 1, 2)
    d_c = jnp.swapaxes(per_lane(jnp.exp(log_g * c)[:, None]), 1, 2)

    per_step = 2 if pairs % 2 == 0 and col0 % 2 == 0 else 1
    groups, lanes, first = pairs // per_step, per_step * LANES, col0 // per_step
    blk = lambda off: pl.BlockSpec((1, c, lanes), lambda b, p, t: (b, t, first + off * groups + p))
    return pl.pallas_call(
        _ret_kernel,
        out_shape=jax.ShapeDtypeStruct((bsz, s, pairs * LANES), BF16),
        grid=(bsz, groups, s // c),
        in_specs=[blk(0), blk(1), blk(2), blk(3),
                  pl.BlockSpec((c, LANES), lambda b, p, t: (t, 0)),
                  pl.BlockSpec((c, LANES), lambda b, p, t: (t, 0)),
                  pl.BlockSpec((2 * per_step, c, c), lambda b, p, t: (p, 0, 0)),
                  pl.BlockSpec((per_step, c, LANES), lambda b, p, t: (p, 0, 0)),
                  pl.BlockSpec((per_step, c, LANES), lambda b, p, t: (p, 0, 0)),
                  pl.BlockSpec((per_step, 1, LANES), lambda b, p, t: (p, 0, 0)),
                  pl.BlockSpec((1, lanes), lambda b, p, t: (0, p))],
        out_specs=pl.BlockSpec((1, c, lanes), lambda b, p, t: (b, t, p)),
        scratch_shapes=[pltpu.VMEM((per_step, LANES, LANES), F32)],
        compiler_params=_params("parallel", "parallel", "arbitrary"),
        name="retention",
    )(h3, h3, h3, h3, cos, sin, d_in, d_q, d_k, d_c, gn_g.reshape(1, -1))


def _chunk_attn_kernel(q_ref, k_ref, v_ref, bias_ref, o_ref):
    tq = q_ref.shape[1]
    t0 = pl.program_id(2) * tq
    n_pairs = q_ref.shape[2] // LANES
    lane = lax.broadcasted_iota(jnp.int32, (1, LANES), 1)
    head0 = lane < HEAD_DIM

    width = bias_ref.shape[3]
    start = pl.multiple_of(jnp.maximum(t0 - LEFT_CHUNKS * CHUNK, 0), tq)

    def pair_slice(ref, pair, rows):
        return ref[0, rows, pair * LANES:(pair + 1) * LANES]

    logits = []
    for pair in range(n_pairs):
        q = pair_slice(q_ref, pair, slice(None)) * jnp.asarray(HEAD_DIM ** -0.5, BF16)
        k = pair_slice(k_ref, pair, pl.ds(start, width))
        logits.append(_dot_nt(jnp.where(head0, q, jnp.zeros_like(q)), k))
        logits.append(_dot_nt(jnp.where(head0, jnp.zeros_like(q), q), k))
    for pair in range(n_pairs):
        probs, denoms = [], []
        for h in range(2):
            head = 2 * pair + h
            s = logits[head] + bias_ref[0, 0, head * tq:(head + 1) * tq, :]
            p = jnp.exp(s - jnp.max(s, axis=-1, keepdims=True))
            denoms.append(jnp.sum(p, axis=-1, keepdims=True))
            probs.append(p.astype(BF16))
        v = pair_slice(v_ref, pair, pl.ds(start, width))
        outs = [_dot(probs[h], v) / denoms[h] for h in range(2)]
        o_ref[0, :, pair * LANES:(pair + 1) * LANES] = jnp.where(head0, outs[0], outs[1]).astype(o_ref.dtype)


def _window_bias(rel_bias, tq, width):
    n_heads = rel_bias.shape[0]
    left = LEFT_CHUNKS * CHUNK
    rel_min, rel_max = left - (width - 1), left + tq - 1
    assert rel_min >= -MAX_REL and rel_max >= MAX_REL
    by_rel = jnp.concatenate(
        [rel_bias[:, rel_min + MAX_REL:], jnp.broadcast_to(rel_bias[:, -1:], (n_heads, rel_max - MAX_REL))], axis=1)
    m = rel_max - rel_min + 1
    rev = jnp.concatenate([by_rel[:, ::-1], jnp.zeros((n_heads, 1), by_rel.dtype)], axis=1)
    rows = jnp.tile(rev, (1, tq))[:, :tq * m].reshape(n_heads, tq, m)
    bias = rows[:, :, tq - 1:tq - 1 + width].astype(F32)
    q_chunk = jnp.arange(tq)[:, None] // CHUNK + LEFT_CHUNKS
    k_chunk = jnp.arange(width)[None, :] // CHUNK
    visible = (k_chunk >= q_chunk - LEFT_CHUNKS) & (k_chunk <= q_chunk)
    return jnp.where(visible[None], bias, NEG_BIG)


def _chunk_attention(qkv3, rel_bias, n_heads, tq=256):
    bsz, s, _ = qkv3.shape
    pairs = n_heads // 2
    n_shift = (LEFT_CHUNKS * CHUNK) // tq
    width = (n_shift + 1) * tq
    assert s >= width
    bias = _window_bias(rel_bias, tq, width)
    per_step = next(n for n in (4, 2, 1) if pairs % n == 0)
    groups, heads = pairs // per_step, 2 * per_step
    lanes = per_step * LANES
    bias = jnp.stack([jnp.pad(bias[:, :, v * tq:], ((0, 0), (0, 0), (0, v * tq)), constant_values=NEG_BIG)
                      for v in range(n_shift + 1)], axis=1).reshape(groups, heads, n_shift + 1, tq, width)
    bias = bias.transpose(0, 2, 1, 3, 4).reshape(groups, n_shift + 1, heads * tq, width)

    return pl.pallas_call(
        _chunk_attn_kernel,
        out_shape=jax.ShapeDtypeStruct((bsz, s, pairs * LANES), BF16),
        grid=(bsz, groups, s // tq),
        in_specs=[pl.BlockSpec((1, tq, lanes), lambda b, p, t: (b, t, p)),
                  pl.BlockSpec((1, s, lanes), lambda b, p, t: (b, 0, groups + p)),
                  pl.BlockSpec((1, s, lanes), lambda b, p, t: (b, 0, 2 * groups + p)),
                  pl.BlockSpec((1, 1, heads * tq, width), lambda b, p, t: (p, jnp.maximum(n_shift - t, 0), 0, 0))],
        out_specs=pl.BlockSpec((1, tq, lanes), lambda b, p, t: (b, t, p)),
        compiler_params=_params("parallel", "parallel", "arbitrary"),
        name="chunk_attention",
    )(qkv3, qkv3, qkv3, bias)


def _swiglu_ln_kernel(alpha, x_ref, wg_ref, wu_ref, wd_ref, g_ref, b_ref, o_ref, xb_ref, acc_ref):
    f = pl.program_id(1)

    @pl.when(f == 0)
    def _():
        xb_ref[...] = x_ref[...].astype(BF16)
        acc_ref[...] = jnp.zeros_like(acc_ref)

    xb = xb_ref[...]
    gate = _dot(xb, wg_ref[...])
    up = _dot(xb, wu_ref[...])
    hidden = (gate / (1.0 + jnp.exp(-gate)) * up).astype(BF16)
    acc_ref[...] += _dot(hidden, wd_ref[...])

    @pl.when(f == pl.num_programs(1) - 1)
    def _():
        o_ref[...] = _layer_norm(alpha * x_ref[...] + acc_ref[...], g_ref[...], b_ref[...])


def _swiglu_ln(x, wg, wu, wd, g, b, alpha, tm_target=512, tf_target=2816):
    t, d = x.shape
    ff = wg.shape[1]
    tm, tf = _tile(t, tm_target), _tile(ff, tf_target)
    once = dict(pipeline_mode=pl.Buffered(1)) if tf == ff else {}
    return pl.pallas_call(
        functools.partial(_swiglu_ln_kernel, alpha),
        out_shape=jax.ShapeDtypeStruct((t, d), F32),
        grid=(t // tm, ff // tf),
        in_specs=[pl.BlockSpec((tm, d), lambda i, f: (i, 0)),
                  pl.BlockSpec((d, tf), lambda i, f: (0, f), **once),
                  pl.BlockSpec((d, tf), lambda i, f: (0, f), **once),
                  pl.BlockSpec((tf, d), lambda i, f: (f, 0), **once),
                  pl.BlockSpec((1, d), lambda i, f: (0, 0)),
                  pl.BlockSpec((1, d), lambda i, f: (0, 0))],
        out_specs=pl.BlockSpec((tm, d), lambda i, f: (i, 0)),
        scratch_shapes=[pltpu.VMEM((tm, d), BF16), pltpu.VMEM((tm, d), F32)],
        compiler_params=_params("parallel", "arbitrary"),
        name="swiglu_ln",
    )(x, wg, wu, wd, g.reshape(1, d), b.reshape(1, d))


ROUTE_E1, ROUTE_E2, ROUTE_RANK1, ROUTE_RANK2 = range(4)
ROUTE_FIELDS = 8
MOE_TILE = 512
MOE_WINDOW = 256
ROW_ALIGN = 16
GATHER_PAIRS = 2
COMBINE_PAIRS = 4
STEP_FIRST, STEP_LAST = 1, 2


def _pair_live(k):
    return 4 << (2 * k)


def _pair_narrow(k):
    return 8 << (2 * k)


def _route_kernel(n_experts, x_ref, w_ref, b_ref, col_ref, row_ref, gate_ref, cnt_ref, run_ref):
    @pl.when(pl.program_id(0) == 0)
    def _():
        run_ref[...] = jnp.zeros_like(run_ref)

    x = x_ref[...]
    w = w_ref[...]
    tm = x.shape[0]
    x_hi = x.astype(BF16)
    x_lo = (x - x_hi.astype(F32)).astype(BF16)
    w_hi = w.astype(BF16)
    w_lo = (w - w_hi.astype(F32)).astype(BF16)
    logits = _dot(x_hi, w_hi) + (_dot(x_hi, w_lo) + _dot(x_lo, w_hi)) + b_ref[...]
    lane = lax.broadcasted_iota(jnp.int32, logits.shape, 1).astype(F32)
    neg_inf = jnp.asarray(-jnp.inf, F32)
    logits = jnp.where(lane < n_experts, logits, neg_inf)
    top1 = jnp.max(logits, axis=-1, keepdims=True)
    idx1 = jnp.min(jnp.where(logits == top1, lane, float(LANES)), axis=-1, keepdims=True)
    rest = jnp.where(lane == idx1, neg_inf, logits)
    top2 = jnp.max(rest, axis=-1, keepdims=True)
    idx2 = jnp.min(jnp.where(rest == top2, lane, float(LANES)), axis=-1, keepdims=True)
    e2 = jnp.exp(top2 - top1)
    denom = 1.0 + e2

    chosen = jnp.where(jnp.logical_or(lane == idx1, lane == idx2), 1.0, 0.0)
    r = lax.broadcasted_iota(jnp.int32, (tm, tm), 0)
    c = lax.broadcasted_iota(jnp.int32, (tm, tm), 1)
    earlier = (c < r).astype(BF16)
    before = _dot(earlier, chosen.astype(BF16)) + run_ref[...]
    rank1 = jnp.sum(jnp.where(lane == idx1, before, 0.0), axis=-1, keepdims=True)
    rank2 = jnp.sum(jnp.where(lane == idx2, before, 0.0), axis=-1, keepdims=True)
    total = jnp.sum(chosen, axis=0, keepdims=True)
    cnt_ref[0] = total
    run_ref[...] += total

    fields = {ROUTE_E1: idx1, ROUTE_E2: idx2, ROUTE_RANK1: rank1, ROUTE_RANK2: rank2}
    meta = jnp.zeros_like(logits)
    for field, value in fields.items():
        meta = jnp.where(lane == field, value, meta)
    col_ref[...] = meta
    row_ref[0] = meta.T[0:ROUTE_FIELDS, :]
    gates = jnp.where(lane == idx1, 1.0 / denom, jnp.where(lane == idx2, e2 / denom, 0.0))
    gates_hi = gates.astype(BF16)
    gate_ref[:, 0:LANES] = gates_hi
    gate_ref[:, LANES:2 * LANES] = (gates - gates_hi.astype(F32)).astype(BF16)


def _route(x, w_router, b_router, tm):
    t, d = x.shape
    n_experts = w_router.shape[1]
    nb = t // tm
    w = jnp.pad(w_router.astype(F32), ((0, 0), (0, LANES - n_experts)))
    b = jnp.pad(b_router.astype(F32), (0, LANES - n_experts)).reshape(1, LANES)
    return pl.pallas_call(
        functools.partial(_route_kernel, n_experts),
        out_shape=[jax.ShapeDtypeStruct((t, LANES), F32),
                   jax.ShapeDtypeStruct((nb, ROUTE_FIELDS, tm), F32),
                   jax.ShapeDtypeStruct((t, 2 * LANES), BF16),
                   jax.ShapeDtypeStruct((nb, 1, LANES), F32)],
        grid=(nb,),
        in_specs=[pl.BlockSpec((tm, d), lambda i: (i, 0)),
                  pl.BlockSpec((d, LANES), lambda i: (0, 0)),
                  pl.BlockSpec((1, LANES), lambda i: (0, 0))],
        out_specs=[pl.BlockSpec((tm, LANES), lambda i: (i, 0)),
                   pl.BlockSpec((1, ROUTE_FIELDS, tm), lambda i: (i, 0, 0)),
                   pl.BlockSpec((tm, 2 * LANES), lambda i: (i, 0)),
                   pl.BlockSpec((1, 1, LANES), lambda i: (i, 0, 0))],
        scratch_shapes=[pltpu.VMEM((1, LANES), F32)],
        compiler_params=_params("arbitrary"),
        name="route",
    )(x, w, b)


def _moe_plan(cnt, n_experts, tile, n_tiles):
    nb = cnt.shape[0]
    counts = cnt[:, 0, :n_experts].astype(jnp.int32)
    cum = jnp.cumsum(counts, axis=0) - counts
    total = jnp.sum(counts, axis=0)
    tiles_e = (total + tile - 1) // tile
    tile_end = jnp.cumsum(tiles_e)
    row_off = (tile_end - tiles_e) * tile
    start = row_off[None, :] + cum
    j0 = start // tile
    j1 = (start + counts - 1) // tile
    has = counts > 0
    tiles = jnp.stack([j0, j0 + 1], axis=-1)
    valid = jnp.stack([has, has & (j1 > j0)], axis=-1)
    lo = jnp.clip(start[..., None] - tiles * tile, 0, tile)
    hi = jnp.clip((start + counts)[..., None] - tiles * tile, 0, tile)
    win = min(MOE_WINDOW, tile)
    window = jnp.minimum(lo // ROW_ALIGN * ROW_ALIGN, tile - win)
    narrow = hi <= window + win

    n_pairs = nb * n_experts + n_tiles
    slot = jnp.arange(n_pairs)
    block_id = jnp.broadcast_to(jnp.arange(nb)[:, None, None], tiles.shape)

    def listed(order, group_key, n_runs, per_step):
        flat = lambda a: a.transpose(order).reshape(-1)
        n_live = jnp.sum(valid)
        idx = jnp.nonzero(flat(valid), size=n_pairs, fill_value=0)[0]
        idx = jnp.where(slot < n_live, idx, idx[n_live - 1])
        pair = {name: flat(a)[idx].astype(jnp.int32)
                for name, a in (("tile", tiles), ("block", block_id), ("window", window), ("narrow", narrow))}
        key = pair[group_key]
        live = slot < n_live
        first = live & (key != jnp.concatenate([jnp.full((1,), -1, key.dtype), key[:-1]]))
        last = live & ((slot == n_live - 1) | (key != jnp.concatenate([key[1:], key[-1:]])))
        run_start = lax.cummax(jnp.where(first, slot, 0), axis=0)
        opens = live & ((slot - run_start) % per_step == 0)
        n_steps = (n_pairs + (per_step - 1) * n_runs) // per_step + 1
        n_open = jnp.sum(opens)
        step = jnp.arange(n_steps)
        lead = jnp.nonzero(opens, size=n_steps, fill_value=0)[0]
        lead = jnp.where(step < n_open, lead, lead[n_open - 1])
        steps = {"tile": [], "block": [], "window": []}
        flag = STEP_FIRST * ((step < n_open) & first[lead])
        ends_run = jnp.zeros_like(step < n_open)
        for k in range(per_step):
            nxt = jnp.minimum(lead + k, n_pairs - 1)
            live_k = (step < n_open) & (lead + k < n_pairs) & live[nxt] & (key[nxt] == key[lead])
            at = jnp.where(live_k, nxt, lead)
            for name in steps:
                steps[name].append(pair[name][at])
            flag = flag + _pair_live(k) * live_k + _pair_narrow(k) * (live_k & (pair["narrow"][at] != 0))
            ends_run = ends_run | (live_k & last[at])
        steps["flag"] = (flag + STEP_LAST * ends_run).astype(jnp.int32)
        return steps

    by_expert = listed((1, 0, 2), "tile", n_tiles, GATHER_PAIRS)
    by_block = listed((0, 1, 2), "block", nb, COMBINE_PAIRS)
    n_used = tile_end[-1:].astype(jnp.int32)
    tile_expert = jnp.minimum(jnp.searchsorted(tile_end, jnp.arange(n_tiles), side="right"),
                              n_experts - 1).astype(jnp.int32)
    return dict(by_expert=by_expert, by_block=by_block, row_off=row_off.astype(jnp.int32),
                n_used=n_used, tile_expert=tile_expert)


def _with_positions(meta, row_off, field_axis):
    def take(first):
        return lax.slice_in_dim(meta, first, first + TOP_K, axis=field_axis)

    expert, offset = take(ROUTE_E1), jnp.zeros_like(take(ROUTE_E1))
    for e in range(row_off.shape[0]):
        offset = jnp.where(expert == float(e), row_off[e].astype(F32), offset)
    n_fields = meta.shape[field_axis]
    return jnp.concatenate([take(ROUTE_E1), take(ROUTE_RANK1) + offset,
                            lax.slice_in_dim(meta, ROUTE_RANK1 + TOP_K, n_fields, axis=field_axis)], axis=field_axis)


def _gather_kernel(tile_ref, flag_ref, *refs):
    n = GATHER_PAIRS
    window_refs = refs[n:2 * n]
    inputs = refs[2 * n:5 * n]
    xs_ref, gs_ref, acc_ref = refs[5 * n:]
    step = pl.program_id(0)
    flag = flag_ref[step]
    tile, tm = xs_ref.shape[0], inputs[2].shape[0]

    def place_pair(row_ref, gate_ref, x_ref, window_ref, narrow):
        pos1 = row_ref[0, ROUTE_RANK1:ROUTE_RANK1 + 1, :].astype(jnp.int32)
        pos2 = row_ref[0, ROUTE_RANK2:ROUTE_RANK2 + 1, :].astype(jnp.int32)

        def place(row0, n_rows):
            rows = tile_ref[step] * tile + row0 + lax.broadcasted_iota(jnp.int32, (n_rows, tm), 0)
            onehot = jnp.where(jnp.logical_or(rows == pos1, rows == pos2), 1.0, 0.0).astype(BF16)
            acc_ref[pl.ds(row0, n_rows), :] += _dot(onehot, x_ref[...])
            gs_ref[pl.ds(row0, n_rows), :] += _dot(onehot, gate_ref[...])

        @pl.when(narrow)
        def _():
            place(pl.multiple_of(window_ref[step], ROW_ALIGN), min(MOE_WINDOW, tile))

        @pl.when(jnp.logical_not(narrow))
        def _():
            place(0, tile)

    @pl.when((flag & STEP_FIRST) != 0)
    def _():
        acc_ref[...] = jnp.zeros_like(acc_ref)
        gs_ref[...] = jnp.zeros_like(gs_ref)

    for k in range(n):
        @pl.when((flag & _pair_live(k)) != 0)
        def _(k=k):
            place_pair(*inputs[3 * k:3 * k + 3], window_refs[k], (flag & _pair_narrow(k)) != 0)

    @pl.when((flag & STEP_LAST) != 0)
    def _():
        xs_ref[...] = acc_ref[...].astype(BF16)


def _moe_gather(plan, pos_row, gates, xb, tile, n_tiles):
    t, d = xb.shape
    _, _, tm = pos_row.shape
    steps = plan["by_expert"]

    def token_block(k):
        return [pl.BlockSpec((1, ROUTE_FIELDS, tm), lambda p, *s: (s[2 + k][p], 0, 0)),
                pl.BlockSpec((tm, 2 * LANES), lambda p, *s: (s[2 + k][p], 0)),
                pl.BlockSpec((tm, d), lambda p, *s: (s[2 + k][p], 0))]

    return pl.pallas_call(
        _gather_kernel,
        out_shape=[jax.ShapeDtypeStruct((n_tiles * tile, d), BF16),
                   jax.ShapeDtypeStruct((n_tiles * tile, 2 * LANES), F32)],
        grid_spec=pltpu.PrefetchScalarGridSpec(
            num_scalar_prefetch=2 + 2 * GATHER_PAIRS,
            grid=(steps["flag"].shape[0],),
            in_specs=[spec for k in range(GATHER_PAIRS) for spec in token_block(k)],
            out_specs=[pl.BlockSpec((tile, d), lambda p, tiles, *_: (tiles[p], 0)),
                       pl.BlockSpec((tile, 2 * LANES), lambda p, tiles, *_: (tiles[p], 0))],
            scratch_shapes=[pltpu.VMEM((tile, d), F32)]),
        compiler_params=_params("arbitrary"),
        name="moe_gather",
    )(steps["tile"][0], steps["flag"], *steps["block"], *steps["window"],
      *([pos_row, gates, xb] * GATHER_PAIRS))


def _moe_ffn_kernel(te_ref, used_ref, xs_ref, gs_ref, wg_ref, wu_ref, wd_ref, o_ref, acc_ref):
    f = pl.program_id(1)
    tile_id = pl.program_id(0)

    @pl.when(tile_id < used_ref[0])
    def _():
        @pl.when(f == 0)
        def _():
            acc_ref[...] = jnp.zeros_like(acc_ref)

        xs = xs_ref[...]
        gate = _dot(xs, wg_ref[0])
        up = _dot(xs, wu_ref[0])
        hidden = (gate / (1.0 + jnp.exp(-gate)) * up).astype(BF16)
        acc_ref[...] += _dot(hidden, wd_ref[0])

        @pl.when(f == pl.num_programs(1) - 1)
        def _():
            by_expert = gs_ref[:, 0:LANES] + gs_ref[:, LANES:2 * LANES]
            lane = lax.broadcasted_iota(jnp.int32, by_expert.shape, 1)
            row_gate = jnp.sum(jnp.where(lane == te_ref[tile_id], by_expert, 0.0), axis=1, keepdims=True)
            o_ref[...] = (acc_ref[...] * row_gate).astype(o_ref.dtype)


def _moe_ffn(plan, xs, gs, wg, wu, wd, tile, tf_target=1792):
    rows, d = xs.shape
    ff = wg.shape[2]
    tf = _tile(ff, tf_target)
    nf = ff // tf

    def tile_of(i, used):
        return jnp.minimum(i, used[0] - 1)

    def chunk_of(i, f, used):
        return jnp.where(i < used[0], f, nf - 1)

    return pl.pallas_call(
        _moe_ffn_kernel,
        out_shape=jax.ShapeDtypeStruct((rows, d), BF16),
        grid_spec=pltpu.PrefetchScalarGridSpec(
            num_scalar_prefetch=2,
            grid=(rows // tile, nf),
            in_specs=[pl.BlockSpec((tile, d), lambda i, f, te, used: (tile_of(i, used), 0)),
                      pl.BlockSpec((tile, 2 * LANES), lambda i, f, te, used: (tile_of(i, used), 0)),
                      pl.BlockSpec((1, d, tf), lambda i, f, te, used: (te[tile_of(i, used)], 0, chunk_of(i, f, used))),
                      pl.BlockSpec((1, d, tf), lambda i, f, te, used: (te[tile_of(i, used)], 0, chunk_of(i, f, used))),
                      pl.BlockSpec((1, tf, d), lambda i, f, te, used: (te[tile_of(i, used)], chunk_of(i, f, used), 0))],
            out_specs=pl.BlockSpec((tile, d), lambda i, f, te, used: (tile_of(i, used), 0)),
            scratch_shapes=[pltpu.VMEM((tile, d), F32)]),
        compiler_params=_params("arbitrary", "arbitrary"),
        name="moe_ffn",
    )(plan["tile_expert"], plan["n_used"], xs, gs, wg, wu, wd)


def _combine_ln_kernel(alpha, block_ref, flag_ref, *refs):
    n = COMBINE_PAIRS
    tile_refs, window_refs = refs[0:n], refs[n:2 * n]
    col_ref = refs[2 * n]
    os_refs = refs[2 * n + 1:3 * n + 1]
    x_ref, g_ref, b_ref, o_ref, acc_ref = refs[3 * n + 1:]
    step = pl.program_id(0)
    flag = flag_ref[step]
    tm, tile = x_ref.shape[0], os_refs[0].shape[0]

    def collect_pair(os_ref, tile_ref, window_ref, narrow):
        pos1 = col_ref[:, ROUTE_RANK1:ROUTE_RANK1 + 1].astype(jnp.int32)
        pos2 = col_ref[:, ROUTE_RANK2:ROUTE_RANK2 + 1].astype(jnp.int32)

        def collect(row0, n_rows):
            cols = tile_ref[step] * tile + row0 + lax.broadcasted_iota(jnp.int32, (tm, n_rows), 1)
            onehot = jnp.where(jnp.logical_or(cols == pos1, cols == pos2), 1.0, 0.0).astype(BF16)
            acc_ref[...] += _dot(onehot, os_ref[pl.ds(row0, n_rows), :])

        @pl.when(narrow)
        def _():
            collect(pl.multiple_of(window_ref[step], ROW_ALIGN), min(MOE_WINDOW, tile))

        @pl.when(jnp.logical_not(narrow))
        def _():
            collect(0, tile)

    @pl.when((flag & STEP_FIRST) != 0)
    def _():
        acc_ref[...] = jnp.zeros_like(acc_ref)

    for k in range(n):
        @pl.when((flag & _pair_live(k)) != 0)
        def _(k=k):
            collect_pair(os_refs[k], tile_refs[k], window_refs[k], (flag & _pair_narrow(k)) != 0)

    @pl.when((flag & STEP_LAST) != 0)
    def _():
        o_ref[...] = _layer_norm(alpha * x_ref[...] + acc_ref[...], g_ref[...], b_ref[...])


def _moe_combine_ln(plan, pos_col, o_sorted, x, g, b, alpha, tile, tm):
    t, d = x.shape
    steps = plan["by_block"]

    def sorted_tile(k):
        return pl.BlockSpec((tile, d), lambda p, *s: (s[2 + k][p], 0))

    return pl.pallas_call(
        functools.partial(_combine_ln_kernel, alpha),
        out_shape=jax.ShapeDtypeStruct((t, d), F32),
        grid_spec=pltpu.PrefetchScalarGridSpec(
            num_scalar_prefetch=2 + 2 * COMBINE_PAIRS,
            grid=(steps["flag"].shape[0],),
            in_specs=[pl.BlockSpec((tm, LANES), lambda p, blocks, *_: (blocks[p], 0))]
            + [sorted_tile(k) for k in range(COMBINE_PAIRS)]
            + [pl.BlockSpec((tm, d), lambda p, blocks, *_: (blocks[p], 0)),
               pl.BlockSpec((1, d), lambda p, *_: (0, 0)),
               pl.BlockSpec((1, d), lambda p, *_: (0, 0))],
            out_specs=pl.BlockSpec((tm, d), lambda p, blocks, *_: (blocks[p], 0)),
            scratch_shapes=[pltpu.VMEM((tm, d), F32)]),
        compiler_params=_params("arbitrary"),
        name="moe_combine_ln",
    )(steps["block"][0], steps["flag"], *steps["tile"], *steps["window"],
      pos_col, *([o_sorted] * COMBINE_PAIRS), x, g.reshape(1, d), b.reshape(1, d))


def _moe_ln(x, xb, w_router, b_router, wg, wu, wd, g, b, alpha):
    t, _ = x.shape
    n_experts = wg.shape[0]
    tile = _tile(t, MOE_TILE)
    n_tiles = TOP_K * t // tile + n_experts
    meta_col, meta_row, gates, cnt = _route(x, w_router, b_router, tile)
    plan = _moe_plan(cnt, n_experts, tile, n_tiles)
    pos_row = _with_positions(meta_row, plan["row_off"], 1)
    pos_col = _with_positions(meta_col, plan["row_off"], 1)
    xs, gs = _moe_gather(plan, pos_row, gates, xb, tile, n_tiles)
    o_sorted = _moe_ffn(plan, xs, gs, wg, wu, wd, tile)
    return _moe_combine_ln(plan, pos_col, o_sorted, x, g, b, alpha, tile, tile)


def _even_layer(x, bsz, s, alpha, ln1_g, ln1_b, w_in, ret_gn_g, w_out, ln2_g, ln2_b, w_gate, w_up, w_down):
    d = x.shape[1]
    n_heads = d // HEAD_DIM
    n_sb = n_heads // 2
    n_ret = n_heads - n_sb
    h = _proj(x, w_in.astype(BF16))
    h3 = h.reshape(bsz, s, h.shape[1])
    o_sb = _stick_breaking(h3, n_sb, 0)
    o_ret = _retention(h3, ret_gn_g, n_ret, 3 * n_sb * HEAD_DIM // LANES)
    x = _out_ln([o_sb.reshape(bsz * s, -1), o_ret.reshape(bsz * s, -1)], w_out.astype(BF16),
                x, ln1_g, ln1_b, alpha)
    return _swiglu_ln(x, w_gate.astype(BF16), w_up.astype(BF16), w_down.astype(BF16), ln2_g, ln2_b, alpha)


def _odd_layer(x, bsz, s, alpha, ln1_g, ln1_b, w_qkv, rel_bias, w_out, ln2_g, ln2_b, w_router, b_router,
               w_gate, w_up, w_down):
    d = x.shape[1]
    qkv = _proj(x, w_qkv.astype(BF16))
    o = _chunk_attention(qkv.reshape(bsz, s, 3 * d), rel_bias, d // HEAD_DIM)
    x, xb = _out_ln([o.reshape(bsz * s, d)], w_out.astype(BF16), x, ln1_g, ln1_b, alpha, also_bf16=True)
    return _moe_ln(x, xb, w_router, b_router, w_gate.astype(BF16), w_up.astype(BF16), w_down.astype(BF16),
                   ln2_g, ln2_b, alpha)


def kernel(x, even_ln1_g, even_ln1_b, even_w_in, even_ret_gn_g, even_w_out, even_ln2_g, even_ln2_b,
           even_w_gate, even_w_up, even_w_down, odd_ln1_g, odd_ln1_b, odd_w_qkv, odd_rel_bias, odd_w_out,
           odd_ln2_g, odd_ln2_b, odd_w_router, odd_b_router, odd_w_gate, odd_w_up, odd_w_down):
    bsz, s, d = x.shape
    depth = even_w_in.shape[0] + odd_w_qkv.shape[0]
    alpha = (2 * depth) ** 0.25
    xt = x.reshape(bsz * s, d)
    for layer in range(depth):
        i = layer // 2
        if layer % 2 == 0:
            xt = _even_layer(xt, bsz, s, alpha, even_ln1_g[i], even_ln1_b[i], even_w_in[i], even_ret_gn_g[i],
                             even_w_out[i], even_ln2_g[i], even_ln2_b[i], even_w_gate[i], even_w_up[i],
                             even_w_down[i])
        else:
            xt = _odd_layer(xt, bsz, s, alpha, odd_ln1_g[i], odd_ln1_b[i], odd_w_qkv[i], odd_rel_bias[i],
                            odd_w_out[i], odd_ln2_g[i], odd_ln2_b[i], odd_w_router[i], odd_b_router[i],
                            odd_w_gate[i], odd_w_up[i], odd_w_down[i])
    return xt.reshape(bsz, s, d)
```
